```python
import jax, jax.numpy as jnp
from jax import lax
import numpy as np

D_MODEL = 2048
BATCH = 16
SEQ = 256
DEPTH = 2
DEC_BATCH = 4
DEC_SEQ = 1024
PAST_LEN = 256

GRID_W = 64
HEAD_DIM = 128
ATT_HEADS = 8
ATT_KV_HEADS = 2
Q_BLOCK = 128
ROPE_THETA = 10000.0
GLA_HEADS = 4
GLA_DK = 128
GLA_DV = 128
GLA_RANK = 16
GLA_TAU = 16.0
GLA_CHUNK = 64
RWKV_HEADS = 8
RWKV_HD = 64
RWKV_W_RANK = 64
RWKV_A_RANK = 64
RWKV_G_RANK = 128
RWKV_DECAY_SCALE = 0.606531
RWKV_LN_EPS = 64e-5
FFN_HIDDEN = 5504
N_BRANCHES = 3
EPS = 1e-6

ATT_Q_W = ATT_HEADS * HEAD_DIM
ATT_KV_W = ATT_KV_HEADS * HEAD_DIM
GLA_W = GLA_HEADS * GLA_DK
GLA_V_W = GLA_HEADS * GLA_DV
RWKV_W = RWKV_HEADS * RWKV_HD
RWKV_COLS = 3 * RWKV_W + 2 * RWKV_W_RANK + RWKV_A_RANK + RWKV_G_RANK
IN_COLS = ATT_Q_W + 2 * ATT_KV_W + 2 * GLA_W + 2 * GLA_V_W + 2 * GLA_RANK + RWKV_COLS + N_BRANCHES * D_MODEL
F32 = jnp.float32

kernel_name = 'hybrid_diffusion_ctx_prefix_step'


def _split(x, sizes):
    idx = [int(i) for i in np.cumsum(sizes)[:-1]]
    return jnp.split(x, idx, axis=-1)


def _rms_norm(x, g, eps=EPS):
    xf = x.astype(F32)
    y = xf * lax.rsqrt(jnp.mean(xf * xf, axis=-1, keepdims=True) + eps)
    return (y * g.astype(F32)).astype(x.dtype)


def _head_layer_norm(x, eps):
    xf = x.astype(F32)
    mu = jnp.mean(xf, axis=-1, keepdims=True)
    xc = xf - mu
    return xc * lax.rsqrt(jnp.mean(xc * xc, axis=-1, keepdims=True) + eps)


def _heads(x, n):
    b, t, _ = x.shape
    return x.reshape(b, t, n, -1).transpose(0, 2, 1, 3)


def _merge(x):
    b, n, t, d = x.shape
    return x.transpose(0, 2, 1, 3).reshape(b, t, n * d)


def _shift_prev(x):
    return jnp.pad(x[:, :-1], ((0, 0), (1, 0), (0, 0)))


def _shift_next(x):
    return jnp.pad(x[:, 1:], ((0, 0), (0, 1), (0, 0)))


def _rope_tables(t):
    rows = t // GRID_W
    row = jnp.repeat(jnp.arange(rows, dtype=F32), GRID_W)
    col = jnp.tile(jnp.arange(GRID_W, dtype=F32), rows)
    half = HEAD_DIM // 2
    inv = ROPE_THETA ** (-jnp.arange(0, half, 2, dtype=F32) / half)

    def cos_sin(pos):
        ang = pos[:, None] * inv[None, :]
        ang = jnp.concatenate([ang, ang], axis=-1)
        return jnp.cos(ang), jnp.sin(ang)

    cos_r, sin_r = cos_sin(row)
    cos_c, sin_c = cos_sin(col)
    return (cos_r, sin_r, cos_c, sin_c)


def _rot_half(x):
    x1, x2 = jnp.split(x, 2, axis=-1)
    return jnp.concatenate([-x2, x1], axis=-1)


def _apply_rope(x, tabs):
    cos_r, sin_r, cos_c, sin_c = tabs
    xr, xc = jnp.split(x.astype(F32), 2, axis=-1)
    xr = xr * cos_r + _rot_half(xr) * sin_r
    xc = xc * cos_c + _rot_half(xc) * sin_c
    return jnp.concatenate([xr, xc], axis=-1).astype(x.dtype)


def _blocked_attention(q, k, v):
    b, hq, t, d = q.shape
    hkv = k.shape[1]
    g = hq // hkv
    nb = t // Q_BLOCK
    qb = q.reshape(b, hkv, g, nb, Q_BLOCK, d).transpose(3, 0, 1, 2, 4, 5)
    scale = d ** -0.5

    def one_block(qblk):
        s = jnp.einsum('bhgqd,bhkd->bhgqk', qblk, k).astype(F32) * scale
        p = jax.nn.softmax(s, axis=-1).astype(v.dtype)
        return jnp.einsum('bhgqk,bhkd->bhgqd', p, v)

    o = lax.map(one_block, qb)
    return o.transpose(1, 2, 3, 0, 4, 5).reshape(b, hq, t, d)


def _gla_chunked(q, k, v, log_a, s0):
    b, h, t, dk = q.shape
    dv = v.shape[-1]
    n = t // GLA_CHUNK
    qc = q.astype(F32).reshape(b, h, n, GLA_CHUNK, dk)
    kc = k.astype(F32).reshape(b, h, n, GLA_CHUNK, dk)
    vc = v.astype(F32).reshape(b, h, n, GLA_CHUNK, dv)
    cum = jnp.cumsum(log_a.astype(F32).reshape(b, h, n, GLA_CHUNK, dk), axis=3)
    cum_last = cum[:, :, :, -1:, :]
    q_e = qc * jnp.exp(cum)
    k_e = kc * jnp.exp(-cum)
    k_l = kc * jnp.exp(cum_last - cum)
    mask = jnp.tril(jnp.ones((GLA_CHUNK, GLA_CHUNK), dtype=bool))
    att = jnp.where(mask, jnp.einsum('bhnid,bhnjd->bhnij', q_e, k_e), 0.0)
    o_intra = jnp.einsum('bhnij,bhnjv->bhniv', att, vc)
    chunk_kv = jnp.einsum('bhnjd,bhnjv->bhndv', k_l, vc)
    decay = jnp.exp(cum_last[:, :, :, 0, :])

    def step(s, inp):
        d, kv_n = inp
        return d[..., None] * s + kv_n, s

    s_fin, s_starts = lax.scan(step, s0.astype(F32), (jnp.moveaxis(decay, 2, 0), jnp.moveaxis(chunk_kv, 2, 0)))
    s_starts = jnp.moveaxis(s_starts, 0, 2)
    o = o_intra + jnp.einsum('bhnid,bhndv->bhniv', q_e, s_starts)
    return o.reshape(b, h, t, dv), s_fin


def _rwkv_scan(r, log_w, k, v, kap, a, s0, reverse):
    xs = tuple(jnp.moveaxis(z, 2, 0) for z in (r, jnp.exp(log_w), k, v, kap, a))

    def step(s, inp):
        r_t, w_t, k_t, v_t, kap_t, a_t = inp
        sk = jnp.einsum('bhvk,bhk->bhv', s, kap_t)
        s = s * w_t[:, :, None, :] - sk[..., None] * (kap_t * a_t)[:, :, None, :] + v_t[..., None] * k_t[:, :, None, :]
        return s, jnp.einsum('bhvk,bhk->bhv', s, r_t)

    s_fin, y = lax.scan(step, s0.astype(F32), xs, reverse=reverse)
    return jnp.moveaxis(y, 0, 2), s_fin


def _mixer(h, p, rope, ctx_kv, gla_s0, rwkv_s0):
    proj = h @ p['w_in']
    aq, ak, av, gq, gk, gv, gg, gad, rw_cols, gates = _split(
        proj, (ATT_Q_W, ATT_KV_W, ATT_KV_W, GLA_W, GLA_W, GLA_V_W, GLA_V_W, 2 * GLA_RANK, RWKV_COLS, N_BRANCHES * D_MODEL))

    q = _rms_norm(_heads(aq, ATT_HEADS), p['q_norm'])
    k = _rms_norm(_heads(ak, ATT_KV_HEADS), p['k_norm'])
    v = _heads(av, ATT_KV_HEADS)
    if rope is not None:
        q = _apply_rope(q, rope)
        k = _apply_rope(k, rope)
    if ctx_kv is None:
        k_all, v_all = k, v
    else:
        k_all = jnp.concatenate([ctx_kv[0].astype(k.dtype), k], axis=2)
        v_all = jnp.concatenate([ctx_kv[1].astype(v.dtype), v], axis=2)
    o_att = _merge(_blocked_attention(q, k_all, v_all))

    gq_h = _heads(gq, GLA_HEADS) * (GLA_DK ** -0.5)
    gk_h = _heads(gk, GLA_HEADS)
    gv_h = _heads(gv, GLA_HEADS)
    gad_f, gad_b = jnp.split(gad, 2, axis=-1)
    la_f = _heads(jax.nn.log_sigmoid((gad_f @ p['gla_a_up'][0] + p['gla_a_bias'][0]).astype(F32)) / GLA_TAU, GLA_HEADS)
    la_b = _heads(jax.nn.log_sigmoid((gad_b @ p['gla_a_up'][1] + p['gla_a_bias'][1]).astype(F32)) / GLA_TAU, GLA_HEADS)
    o_f, s_f = _gla_chunked(gq_h, gk_h, gv_h, la_f, gla_s0[:, 0])
    o_b, s_b = _gla_chunked(jnp.flip(gq_h, 2), jnp.flip(gk_h, 2), jnp.flip(gv_h, 2), jnp.flip(la_b, 2), gla_s0[:, 1])
    o_gla = _merge(_rms_norm(o_f + jnp.flip(o_b, 2), p['gla_norm']))
    o_gla = (o_gla * jax.nn.silu(gg.astype(F32))).astype(h.dtype)

    rw = rw_cols + (0.5 * (_shift_prev(rw_cols) + _shift_next(rw_cols)) - rw_cols) * p['rwkv_mu']
    rr, rk, rv, rwd, rad, rgd = _split(rw, (RWKV_W, RWKV_W, RWKV_W, 2 * RWKV_W_RANK, RWKV_A_RANK, RWKV_G_RANK))
    rwd_f, rwd_b = jnp.split(rwd, 2, axis=-1)
    logw_f = -RWKV_DECAY_SCALE * jax.nn.sigmoid((p['rwkv_w0'][0] + jnp.tanh(rwd_f) @ p['rwkv_w_up'][0]).astype(F32))
    logw_b = -RWKV_DECAY_SCALE * jax.nn.sigmoid((p['rwkv_w0'][1] + jnp.tanh(rwd_b) @ p['rwkv_w_up'][1]).astype(F32))
    a = jax.nn.sigmoid((p['rwkv_a0'] + rad @ p['rwkv_a_up']).astype(F32))
    g_out = jax.nn.sigmoid(rgd) @ p['rwkv_g_up']
    rkf = rk.astype(F32)
    k_rep = rkf * (1.0 + (a - 1.0) * p['rwkv_k_alpha'].astype(F32))
    kap = _heads(rkf * p['rwkv_k_xi'].astype(F32), RWKV_HEADS)
    kap = kap * lax.rsqrt(jnp.sum(kap * kap, axis=-1, keepdims=True) + EPS)
    r_h = _heads(rr.astype(F32), RWKV_HEADS)
    v_h = _heads(rv.astype(F32), RWKV_HEADS)
    k_h = _heads(k_rep, RWKV_HEADS)
    a_h = _heads(a, RWKV_HEADS)
    y_f, st_f = _rwkv_scan(r_h, _heads(logw_f, RWKV_HEADS), k_h, v_h, kap, a_h, rwkv_s0[:, 0], False)
    y_b, st_b = _rwkv_scan(r_h, _heads(logw_b, RWKV_HEADS), k_h, v_h, kap, a_h, rwkv_s0[:, 1], True)
    rho = p['rwkv_bonus'].astype(F32).reshape(RWKV_HEADS, 1, RWKV_HD)
    bonus = jnp.sum(r_h * k_h * rho, axis=-1, keepdims=True) * v_h
    y = _merge(_head_layer_norm(y_f + y_b + bonus, RWKV_LN_EPS)) * p['rwkv_ln_w'].astype(F32) + p['rwkv_ln_b'].astype(F32)
    y = (y * g_out.astype(F32)).astype(h.dtype)

    g_att, g_gla, g_rwkv = jnp.split(jax.nn.sigmoid(gates), N_BRANCHES, axis=-1)
    merged = g_att * (o_att @ p['w_br_att']) + g_gla * (o_gla @ p['w_br_gla']) + g_rwkv * (y @ p['w_br_rwkv'])
    out = merged @ p['w_out']
    return out, k, v, jnp.stack([s_f, s_b], axis=1), jnp.stack([st_f, st_b], axis=1)


def _conv_ffn(h, p):
    u = h @ p['ffn_up']
    w = p['ffn_conv_w']
    u = w[0] * _shift_prev(u) + w[1] * u + w[2] * _shift_next(u) + p['ffn_conv_b']
    val, gate = jnp.split(u, 2, axis=-1)
    return (jax.nn.silu(gate) * val) @ p['ffn_down']


def _layer(x, mod, p, rope, ctx_kv, gla_s0, rwkv_s0):
    sh1, sc1, gt1, sh2, sc2, gt2 = jnp.split(mod, 6, axis=-1)
    h = _rms_norm(x, p['norm_mix']) * (1.0 + sc1) + sh1
    out, k, v, gla_st, rwkv_st = _mixer(h, p, rope, ctx_kv, gla_s0, rwkv_s0)
    x = x + gt1 * out
    h = _rms_norm(x, p['norm_ffn']) * (1.0 + sc2) + sh2
    x = x + gt2 * _conv_ffn(h, p)
    return x, k, v, gla_st, rwkv_st


def setup_inputs(seed: int = 0) -> dict:
    key = jax.random.key(seed)
    ctr = [0]

    def nk():
        ctr[0] += 1
        return jax.random.fold_in(key, ctr[0])

    def nrm(shape, scale=1.0):
        return jax.random.normal(nk(), shape, F32) * scale

    L = DEPTH
    return {
        'x_prompt': nrm((BATCH, SEQ, D_MODEL)),
        'x_sample': nrm((DEC_BATCH, DEC_SEQ, D_MODEL)),
        'cache_k': nrm((DEC_BATCH, L, ATT_KV_HEADS, PAST_LEN, HEAD_DIM)),
        'cache_v': nrm((DEC_BATCH, L, ATT_KV_HEADS, PAST_LEN, HEAD_DIM)),
        'state_gla': nrm((DEC_BATCH, L, 2, GLA_HEADS, GLA_DK, GLA_DV), 0.1),
        'state_rwkv': nrm((DEC_BATCH, L, 2, RWKV_HEADS, RWKV_HD, RWKV_HD), 0.1),
        'c': nrm((DEC_BATCH, D_MODEL)),
        'c_ctx': nrm((D_MODEL,)),
        'w_mod': nrm((L, D_MODEL, 6 * D_MODEL), 0.5 * D_MODEL ** -0.5),
        'b_mod': nrm((L, 6 * D_MODEL), 0.01),
        'norm_mix': 1.0 + nrm((L, D_MODEL), 0.01),
        'w_in': nrm((L, D_MODEL, IN_COLS), D_MODEL ** -0.5),
        'q_norm': 1.0 + nrm((L, HEAD_DIM), 0.01),
        'k_norm': 1.0 + nrm((L, HEAD_DIM), 0.01),
        'gla_a_up': nrm((L, 2, GLA_RANK, GLA_W), GLA_RANK ** -0.5),
        'gla_a_bias': nrm((L, 2, GLA_W), 0.1),
        'gla_norm': 1.0 + nrm((L, GLA_DV), 0.01),
        'rwkv_mu': jax.random.uniform(nk(), (L, RWKV_COLS), F32),
        'rwkv_w0': nrm((L, 2, RWKV_W), 0.5),
        'rwkv_w_up': nrm((L, 2, RWKV_W_RANK, RWKV_W), 0.5 * RWKV_W_RANK ** -0.5),
        'rwkv_a0': nrm((L, RWKV_W), 0.1),
        'rwkv_a_up': nrm((L, RWKV_A_RANK, RWKV_W), RWKV_A_RANK ** -0.5),
        'rwkv_g_up': nrm((L, RWKV_G_RANK, RWKV_W), RWKV_G_RANK ** -0.5),
        'rwkv_k_xi': 0.85 + nrm((L, RWKV_W), 0.05),
        'rwkv_k_alpha': 1.0 + nrm((L, RWKV_W), 0.05),
        'rwkv_bonus': nrm((L, RWKV_W), 0.1),
        'rwkv_ln_w': 1.0 + nrm((L, RWKV_W), 0.01),
        'rwkv_ln_b': nrm((L, RWKV_W), 0.01),
        'w_br_att': nrm((L, ATT_Q_W, D_MODEL), ATT_Q_W ** -0.5),
        'w_br_gla': nrm((L, GLA_V_W, D_MODEL), GLA_V_W ** -0.5),
        'w_br_rwkv': nrm((L, RWKV_W, D_MODEL), RWKV_W ** -0.5),
        'w_out': nrm((L, D_MODEL, D_MODEL), D_MODEL ** -0.5),
        'norm_ffn': 1.0 + nrm((L, D_MODEL), 0.01),
        'ffn_up': nrm((L, D_MODEL, 2 * FFN_HIDDEN), D_MODEL ** -0.5),
        'ffn_conv_w': nrm((L, 3, 2 * FFN_HIDDEN), 3 ** -0.5),
        'ffn_conv_b': nrm((L, 2 * FFN_HIDDEN), 0.01),
        'ffn_down': nrm((L, FFN_HIDDEN, D_MODEL), FFN_HIDDEN ** -0.5),
    }


def reference(x_prompt, x_sample, cache_k, cache_v, state_gla, state_rwkv, c, c_ctx,
              w_mod, b_mod, norm_mix, w_in, q_norm, k_norm, gla_a_up, gla_a_bias, gla_norm,
              rwkv_mu, rwkv_w0, rwkv_w_up, rwkv_a0, rwkv_a_up, rwkv_g_up, rwkv_k_xi, rwkv_k_alpha,
              rwkv_bonus, rwkv_ln_w, rwkv_ln_b, w_br_att, w_br_gla, w_br_rwkv, w_out, norm_ffn,
              ffn_up, ffn_conv_w, ffn_conv_b, ffn_down):
    def layer_params(l):
        return {
            'norm_mix': norm_mix[l], 'w_in': w_in[l], 'q_norm': q_norm[l], 'k_norm': k_norm[l],
            'gla_a_up': gla_a_up[l], 'gla_a_bias': gla_a_bias[l], 'gla_norm': gla_norm[l],
            'rwkv_mu': rwkv_mu[l], 'rwkv_w0': rwkv_w0[l], 'rwkv_w_up': rwkv_w_up[l],
            'rwkv_a0': rwkv_a0[l], 'rwkv_a_up': rwkv_a_up[l], 'rwkv_g_up': rwkv_g_up[l],
            'rwkv_k_xi': rwkv_k_xi[l], 'rwkv_k_alpha': rwkv_k_alpha[l], 'rwkv_bonus': rwkv_bonus[l],
            'rwkv_ln_w': rwkv_ln_w[l], 'rwkv_ln_b': rwkv_ln_b[l], 'w_br_att': w_br_att[l],
            'w_br_gla': w_br_gla[l], 'w_br_rwkv': w_br_rwkv[l], 'w_out': w_out[l],
            'norm_ffn': norm_ffn[l], 'ffn_up': ffn_up[l], 'ffn_conv_w': ffn_conv_w[l],
            'ffn_conv_b': ffn_conv_b[l], 'ffn_down': ffn_down[l],
        }

    b_ctx = x_prompt.shape[0]
    gla_zero = jnp.zeros((b_ctx, 2, GLA_HEADS, GLA_DK, GLA_DV), F32)
    rwkv_zero = jnp.zeros((b_ctx, 2, RWKV_HEADS, RWKV_HD, RWKV_HD), F32)
    x = x_prompt
    ks, vs, gla_states, rwkv_states = [], [], [], []
    for l in range(DEPTH):
        p = layer_params(l)
        mod = (jax.nn.silu(c_ctx) @ w_mod[l] + b_mod[l])[None, None, :]
        x, k_l, v_l, gs, rs = _layer(x, mod, p, None, None, gla_zero, rwkv_zero)
        ks.append(k_l)
        vs.append(v_l)
        gla_states.append(gs)
        rwkv_states.append(rs)
    y_prompt = x
    new_cache_k = jnp.stack(ks, axis=1)
    new_cache_v = jnp.stack(vs, axis=1)
    new_state_gla = jnp.stack(gla_states, axis=1)
    new_state_rwkv = jnp.stack(rwkv_states, axis=1)

    rope = _rope_tables(x_sample.shape[1])
    x = x_sample
    for l in range(DEPTH):
        p = layer_params(l)
        mod = (jax.nn.silu(c) @ w_mod[l] + b_mod[l])[:, None, :]
        x, _, _, _, _ = _layer(x, mod, p, rope, (cache_k[:, l], cache_v[:, l]), state_gla[:, l], state_rwkv[:, l])
    y_sample = x

    return (y_prompt, y_sample, new_cache_k, new_cache_v, new_state_gla, new_state_rwkv)
```

```python
import functools

import jax
import jax.numpy as jnp
from jax import lax
from jax.experimental import pallas as pl
from jax.experimental.pallas import tpu as pltpu

F32 = jnp.float32
BF16 = jnp.bfloat16

D_MODEL = 2048
BATCH = 16
SEQ = 256
DEPTH = 2
DEC_BATCH = 4
DEC_SEQ = 1024
PAST_LEN = 256
GRID_W = 64
HEAD_DIM = 128
ATT_HEADS = 8
ATT_KV_HEADS = 2
ROPE_THETA = 10000.0
GLA_HEADS = 4
GLA_DK = 128
GLA_RANK = 16
GLA_TAU = 16.0
RWKV_HEADS = 8
RWKV_HD = 64
RWKV_W_RANK = 64
RWKV_A_RANK = 64
RWKV_G_RANK = 128
RWKV_DECAY_SCALE = 0.606531
RWKV_LN_EPS = 64e-5
FFN_HIDDEN = 5504
EPS = 1e-6

ATT_Q_W = ATT_HEADS * HEAD_DIM
ATT_KV_W = ATT_KV_HEADS * HEAD_DIM
GLA_W = GLA_HEADS * GLA_DK
RWKV_W = RWKV_HEADS * RWKV_HD
RWKV_COLS = 3 * RWKV_W + 2 * RWKV_W_RANK + RWKV_A_RANK + RWKV_G_RANK

LANE = 128
CHUNK = 64
N_CTX = BATCH * SEQ
N_LAT = DEC_BATCH * DEC_SEQ
N_TOK = N_CTX + N_LAT
N_MODROWS = 8
FFN_PAD = 5632
ATT_SEG = ATT_Q_W + 2 * ATT_KV_W
GLA_SEG = 4 * GLA_W + LANE
RWKV_SEG = 3 * RWKV_W + 4 * LANE
GATE_SEG = 3 * D_MODEL
VMEM_LIMIT = 56 * 1024 * 1024

_NT = (((1,), (1,)), ((), ()))
_TN = (((0,), (0,)), ((), ()))
_NN = (((1,), (0,)), ((), ()))


def _bf(x):
    return x.astype(BF16)


def _dg(a, b, dims=_NN):
    return lax.dot_general(a, b, dims, preferred_element_type=F32)


def _mm(a, b, dims=_NN):
    return _dg(_bf(a), _bf(b), dims)


def _split2(x):
    h = x.astype(BF16)
    return h, (x - h.astype(F32)).astype(BF16)


def _split3(x):
    h = x.astype(BF16)
    r = x - h.astype(F32)
    m = r.astype(BF16)
    return h, m, (r - m.astype(F32)).astype(BF16)


def _mm3(a, b, dims=_NN):
    ah, al = _split2(a)
    bh, bl = _split2(b)
    return _dg(ah, bh, dims) + (_dg(al, bh, dims) + _dg(ah, bl, dims))


def _mm_exact_rhs(a, b_bf, dims=_NN):
    h, m, l = _split3(a)
    return _dg(h, b_bf, dims) + (_dg(m, b_bf, dims) + _dg(l, b_bf, dims))


def _rms(x, g):
    return x * lax.rsqrt(jnp.mean(x * x, axis=-1, keepdims=True) + EPS) * g


def _cparams(sem):
    return pltpu.CompilerParams(dimension_semantics=sem, vmem_limit_bytes=VMEM_LIMIT)


def _mod_row(i, tm):
    n_ctx = N_CTX // tm
    return jnp.where(i < n_ctx, 0, 1 + (i - n_ctx) // (DEC_SEQ // tm))


def _seq_pos(i, tm):
    g = i * tm + lax.broadcasted_iota(jnp.int32, (tm, 1), 0)
    in_ctx = g < N_CTX
    pos = jnp.where(in_ctx, g & (SEQ - 1), (g - N_CTX) & (DEC_SEQ - 1))
    last = jnp.where(in_ctx, SEQ - 1, DEC_SEQ - 1)
    return pos, last


def _mod_kernel(c_ref, w_ref, b_ref, o_ref):
    c = c_ref[...]
    s = c * jax.nn.sigmoid(c)
    o_ref[...] = _mm(s, w_ref[...]) + b_ref[...]


def _modulation(cvec, w_mod, b_mod):
    tn = 1024
    n = 6 * D_MODEL
    return pl.pallas_call(
        _mod_kernel,
        grid=(DEPTH, n // tn),
        in_specs=[
            pl.BlockSpec((N_MODROWS, D_MODEL), lambda l, j: (0, 0)),
            pl.BlockSpec((None, D_MODEL, tn), lambda l, j: (l, 0, j)),
            pl.BlockSpec((None, 1, tn), lambda l, j: (l, 0, j)),
        ],
        out_specs=pl.BlockSpec((None, N_MODROWS, tn), lambda l, j: (l, 0, j)),
        out_shape=jax.ShapeDtypeStruct((DEPTH, N_MODROWS, n), F32),
        compiler_params=_cparams(("arbitrary", "arbitrary")),
        name="modulation",
    )(cvec, w_mod, b_mod.reshape(DEPTH, 1, n))


def _prenorm_kernel(x_ref, g_ref, mod_ref, o_ref, *, shift_idx):
    y = _rms(x_ref[...], g_ref[...])
    sh = mod_ref[shift_idx:shift_idx + 1, :]
    sc = mod_ref[shift_idx + 1:shift_idx + 2, :]
    o_ref[...] = (y * (1.0 + sc) + sh).astype(BF16)


def _prenorm(x, g, mod, shift_idx):
    tm = 256
    return pl.pallas_call(
        functools.partial(_prenorm_kernel, shift_idx=shift_idx),
        grid=(N_TOK // tm,),
        in_specs=[
            pl.BlockSpec((tm, D_MODEL), lambda i: (i, 0)),
            pl.BlockSpec((1, D_MODEL), lambda i: (0, 0)),
            pl.BlockSpec((None, 6, D_MODEL), lambda i: (_mod_row(i, tm), 0, 0)),
        ],
        out_specs=pl.BlockSpec((tm, D_MODEL), lambda i: (i, 0)),
        out_shape=jax.ShapeDtypeStruct((N_TOK, D_MODEL), BF16),
        compiler_params=_cparams(("arbitrary",)),
        name="prenorm",
    )(x, g.reshape(1, D_MODEL), mod)


def _matmul_kernel(a_ref, b_ref, o_ref):
    o_ref[...] = _dg(a_ref[...], b_ref[...]).astype(o_ref.dtype)


def _matmul(a, b, tn, name):
    m, k = a.shape
    n = b.shape[1]
    tm = 1024
    return pl.pallas_call(
        _matmul_kernel,
        grid=(n // tn, m // tm),
        in_specs=[
            pl.BlockSpec((tm, k), lambda j, i: (i, 0)),
            pl.BlockSpec((k, tn), lambda j, i: (0, j)),
        ],
        out_specs=pl.BlockSpec((tm, tn), lambda j, i: (i, j)),
        out_shape=jax.ShapeDtypeStruct((m, n), F32),
        compiler_params=_cparams(("arbitrary", "arbitrary")),
        name=name,
    )(a, b)


_ATT_SCALE = HEAD_DIM ** -0.5


def _rope(x, cos, sin_signed):
    lane = lax.broadcasted_iota(jnp.int32, x.shape, 1)
    partner = jnp.where((lane & 63) < 32, pltpu.roll(x, 96, 1), pltpu.roll(x, 32, 1))
    return x * cos + partner * sin_signed


def _attn_ctx_kernel(q_ref, k_ref, v_ref, qn_ref, kn_ref, o_ref, ko_ref, vo_ref):
    q = _rms(q_ref[...], qn_ref[...])
    k = _rms(k_ref[...], kn_ref[...])
    v = v_ref[...]
    s = _mm(q, k, _NT) * _ATT_SCALE
    p = jnp.exp(s - jnp.max(s, axis=-1, keepdims=True))
    o = _mm(p, v) / jnp.sum(p, axis=-1, keepdims=True)
    o_ref[...] = o.astype(BF16)
    ko_ref[...] = k
    vo_ref[...] = v


def _attn_lat_kernel(q_ref, k_ref, v_ref, qn_ref, kn_ref, cos_ref, sin_ref, ck_ref, cv_ref, o_ref):
    cos = cos_ref[...]
    sin = sin_ref[...]
    q = _rope(_rms(q_ref[...], qn_ref[...]), cos, sin)
    k = _rope(_rms(k_ref[...], kn_ref[...]), cos, sin)
    qb = _bf(q)
    s1 = _dg(qb, _bf(ck_ref[...]), _NT) * _ATT_SCALE
    s2 = _dg(qb, _bf(k), _NT) * _ATT_SCALE
    m = jnp.maximum(jnp.max(s1, axis=-1, keepdims=True), jnp.max(s2, axis=-1, keepdims=True))
    p1 = jnp.exp(s1 - m)
    p2 = jnp.exp(s2 - m)
    den = jnp.sum(p1, axis=-1, keepdims=True) + jnp.sum(p2, axis=-1, keepdims=True)
    o = (_mm(p1, cv_ref[...]) + _mm(p2, v_ref[...])) / den
    o_ref[...] = o.astype(BF16)


def _attention_ctx(proj, qn, kn):
    g4 = ATT_HEADS // ATT_KV_HEADS
    blk = lambda f: pl.BlockSpec((SEQ, HEAD_DIM), f)
    vec = pl.BlockSpec((1, HEAD_DIM), lambda b, g, i: (0, 0))
    kv_out = pl.BlockSpec((None, None, SEQ, HEAD_DIM), lambda b, g, i: (b, g, 0, 0))
    return pl.pallas_call(
        _attn_ctx_kernel,
        grid=(BATCH, ATT_KV_HEADS, g4),
        in_specs=[
            blk(lambda b, g, i: (b, g * g4 + i)),
            blk(lambda b, g, i: (b, ATT_HEADS + g)),
            blk(lambda b, g, i: (b, ATT_HEADS + ATT_KV_HEADS + g)),
            vec, vec,
        ],
        out_specs=[blk(lambda b, g, i: (b, g * g4 + i)), kv_out, kv_out],
        out_shape=[
            jax.ShapeDtypeStruct((N_CTX, ATT_Q_W), BF16),
            jax.ShapeDtypeStruct((BATCH, ATT_KV_HEADS, SEQ, HEAD_DIM), F32),
            jax.ShapeDtypeStruct((BATCH, ATT_KV_HEADS, SEQ, HEAD_DIM), F32),
        ],
        compiler_params=_cparams(("arbitrary", "arbitrary", "arbitrary")),
        name="attention_ctx",
    )(proj, proj, proj, qn, kn)


def _attention_lat(proj, qn, kn, cos, sin, cache_k, cache_v, layer):
    g4 = ATT_HEADS // ATT_KV_HEADS
    rb = N_CTX // DEC_SEQ
    blk = lambda f: pl.BlockSpec((DEC_SEQ, HEAD_DIM), f)
    vec = pl.BlockSpec((1, HEAD_DIM), lambda b, g, i: (0, 0))
    tab = pl.BlockSpec((DEC_SEQ, HEAD_DIM), lambda b, g, i: (0, 0))
    cache = pl.BlockSpec((None, None, None, PAST_LEN, HEAD_DIM), lambda b, g, i: (b, layer, g, 0, 0))
    return pl.pallas_call(
        _attn_lat_kernel,
        grid=(DEC_BATCH, ATT_KV_HEADS, g4),
        in_specs=[
            blk(lambda b, g, i: (rb + b, g * g4 + i)),
            blk(lambda b, g, i: (rb + b, ATT_HEADS + g)),
            blk(lambda b, g, i: (rb + b, ATT_HEADS + ATT_KV_HEADS + g)),
            vec, vec, tab, tab, cache, cache,
        ],
        out_specs=blk(lambda b, g, i: (b, g * g4 + i)),
        out_shape=jax.ShapeDtypeStruct((N_LAT, ATT_Q_W), BF16),
        compiler_params=_cparams(("arbitrary", "arbitrary", "arbitrary")),
        name="attention_lat",
    )(proj, proj, proj, qn, kn, cos, sin, cache_k, cache_v)


def _tri(n, upper):
    r = lax.broadcasted_iota(jnp.int32, (n, n), 0)
    c = lax.broadcasted_iota(jnp.int32, (n, n), 1)
    return (r <= c) if upper else (r >= c)


def _gla_kernel(*refs, seq, has_state):
    if has_state:
        (q_ref, k_ref, v_ref, gg_ref, gad_ref, up_ref, bias_ref, gn_ref, s0_ref,
         o_ref, st_ref, la_scr, of_scr) = refs
    else:
        (q_ref, k_ref, v_ref, gg_ref, gad_ref, up_ref, bias_ref, gn_ref,
         o_ref, st_ref, la_scr, of_scr) = refs
        s0_ref = None
    n_chunks = seq // CHUNK
    gad = _bf(gad_ref[...])
    for d in range(2):
        z = _dg(gad, _bf(up_ref[d])) + bias_ref[d]
        la_scr[d] = jax.nn.log_sigmoid(z) / GLA_TAU

    def run(d):
        upper = d == 1
        incl = _tri(CHUNK, upper)
        tri = jnp.where(incl, 1.0, 0.0).astype(BF16)
        s_init = s0_ref[d].T if has_state else jnp.zeros((GLA_DK, GLA_DK), F32)

        def body(i, st):
            n = (n_chunks - 1 - i) if upper else i
            rows = pl.ds(pl.multiple_of(n * CHUNK, CHUNK), CHUNK)
            la = la_scr[d, rows, :]
            cum = _cumsum(tri, la)
            tot = cum[0:1, :] if upper else cum[CHUNK - 1:CHUNK, :]
            q = q_ref[rows, :] * (GLA_DK ** -0.5)
            k = k_ref[rows, :]
            v = v_ref[rows, :]
            qe = q * jnp.exp(cum)
            ke = k * jnp.exp(-cum)
            kl = k * jnp.exp(tot - cum)
            att = jnp.where(incl, _mm(qe, ke, _NT), 0.0)
            o = _mm(att, v) + _mm(qe, st, _NT)
            if upper:
                o_ref[rows, :] = o + of_scr[rows, :]
            else:
                of_scr[rows, :] = o
            return st * jnp.exp(tot) + _mm(v, kl, _TN)

        st = lax.fori_loop(0, n_chunks, body, s_init)
        st_ref[d] = st.T

    run(0)
    run(1)
    o = _rms(o_ref[...], gn_ref[...])
    g = gg_ref[...]
    o_ref[...] = o * (g * jax.nn.sigmoid(g))


def _cumsum(tri_bf, x):
    h, m, l = _split3(x)
    return _dg(tri_bf, h) + (_dg(tri_bf, m) + _dg(tri_bf, l))


def _gla(proj, up_pad, bias, gnorm, s0, layer, seq, nb, row0):
    rb = row0 // seq
    blk = lambda c: pl.BlockSpec((seq, LANE), lambda b, h: (rb + b, c * GLA_HEADS + h))
    has_state = s0 is not None
    in_specs = [
        blk(0), blk(1), blk(2), blk(3),
        pl.BlockSpec((seq, LANE), lambda b, h: (rb + b, 4 * GLA_HEADS)),
        pl.BlockSpec((2, None, LANE, LANE), lambda b, h: (0, h, 0, 0)),
        pl.BlockSpec((2, None, 1, LANE), lambda b, h: (0, h, 0, 0)),
        pl.BlockSpec((1, LANE), lambda b, h: (0, 0)),
    ]
    args = [proj, proj, proj, proj, proj, up_pad, bias, gnorm]
    if has_state:
        in_specs.append(pl.BlockSpec((None, None, 2, None, GLA_DK, GLA_DK),
                                     lambda b, h: (b, layer, 0, h, 0, 0)))
        args.append(s0)
    return pl.pallas_call(
        functools.partial(_gla_kernel, seq=seq, has_state=has_state),
        grid=(nb, GLA_HEADS),
        in_specs=in_specs,
        out_specs=[
            pl.BlockSpec((seq, LANE), lambda b, h: (b, h)),
            pl.BlockSpec((None, 2, None, GLA_DK, GLA_DK), lambda b, h: (b, 0, h, 0, 0)),
        ],
        out_shape=[
            jax.ShapeDtypeStruct((nb * seq, GLA_W), F32),
            jax.ShapeDtypeStruct((nb, 2, GLA_HEADS, GLA_DK, GLA_DK), F32),
        ],
        scratch_shapes=[pltpu.VMEM((2, seq, LANE), F32), pltpu.VMEM((seq, LANE), F32)],
        compiler_params=_cparams(("arbitrary", "arbitrary")),
        name="gla_lat" if has_state else "gla_ctx",
    )(*args)


def _segsum(x, bd_ref):
    return _mm_exact_rhs(x, bd_ref[...])


def _rwkv_prep_kernel(x_ref, xp_ref, xn_ref, mu_ref, wuf_ref, wub_ref, au_ref, gu_ref, w0_ref,
                      a0_ref, xi_ref, al_ref, bd_ref,
                      r_ref, k_ref, v_ref, kap_ref, beta_ref, lw_ref, g_ref):
    tm = x_ref.shape[0]
    i = pl.program_id(0)
    pos, last = _seq_pos(i, tm)
    x = x_ref[...]
    row = lax.broadcasted_iota(jnp.int32, (tm, 1), 0)
    prev = jnp.where(row == 0, xp_ref[7:8, :], pltpu.roll(x, 1, 0))
    nxt = jnp.where(row == tm - 1, xn_ref[0:1, :], pltpu.roll(x, tm - 1, 0))
    prev = jnp.where(pos == 0, 0.0, prev)
    nxt = jnp.where(pos == last, 0.0, nxt)
    rw = x + (0.5 * (prev + nxt) - x) * mu_ref[...]
    w = RWKV_W
    rr = rw[:, 0:w]
    rk = rw[:, w:2 * w]
    rv = rw[:, 2 * w:3 * w]
    rwd_f = rw[:, 3 * w:3 * w + LANE]
    rwd_b = rw[:, 3 * w + LANE:3 * w + 2 * LANE]
    rad = rw[:, 3 * w + 2 * LANE:3 * w + 3 * LANE]
    rgd = rw[:, 3 * w + 3 * LANE:3 * w + 4 * LANE]
    lw_ref[0] = -RWKV_DECAY_SCALE * jax.nn.sigmoid(w0_ref[0:1, :] + _dg(_bf(jnp.tanh(rwd_f)), wuf_ref[...]))
    lw_ref[1] = -RWKV_DECAY_SCALE * jax.nn.sigmoid(w0_ref[1:2, :] + _dg(_bf(jnp.tanh(rwd_b)), wub_ref[...]))
    a = jax.nn.sigmoid(a0_ref[...] + _dg(_bf(rad), au_ref[...]))
    g_ref[...] = _dg(_bf(jax.nn.sigmoid(rgd)), gu_ref[...])
    kap = rk * xi_ref[...]
    kap = kap * lax.rsqrt(_segsum(kap * kap, bd_ref) + EPS)
    r_ref[...] = rr
    k_ref[...] = rk * (1.0 + (a - 1.0) * al_ref[...])
    v_ref[...] = rv
    kap_ref[...] = kap
    beta_ref[...] = kap * a


def _rwkv_prep(proj, mu, wuf, wub, au, gu, w0, a0, xi, al, bd):
    tm = 256
    nt = N_TOK // tm
    hb = tm // 8
    full = lambda shape: pl.BlockSpec(shape, lambda i: (0,) * len(shape))
    tok = pl.BlockSpec((tm, RWKV_W), lambda i: (i, 0))
    tshape = jax.ShapeDtypeStruct((N_TOK, RWKV_W), F32)
    return pl.pallas_call(
        _rwkv_prep_kernel,
        grid=(nt,),
        in_specs=[
            pl.BlockSpec((tm, RWKV_SEG), lambda i: (i, 0)),
            pl.BlockSpec((8, RWKV_SEG), lambda i: (jnp.maximum(i * hb - 1, 0), 0)),
            pl.BlockSpec((8, RWKV_SEG), lambda i: (jnp.minimum((i + 1) * hb, N_TOK // 8 - 1), 0)),
            full((1, RWKV_SEG)),
            full((LANE, RWKV_W)), full((LANE, RWKV_W)), full((LANE, RWKV_W)), full((LANE, RWKV_W)),
            full((2, RWKV_W)), full((1, RWKV_W)), full((1, RWKV_W)), full((1, RWKV_W)),
            full((RWKV_W, RWKV_W)),
        ],
        out_specs=[tok, tok, tok, tok, tok,
                   pl.BlockSpec((2, tm, RWKV_W), lambda i: (0, i, 0)), tok],
        out_shape=[tshape, tshape, tshape, tshape, tshape,
                   jax.ShapeDtypeStruct((2, N_TOK, RWKV_W), F32), tshape],
        compiler_params=_cparams(("arbitrary",)),
        name="rwkv_prep",
    )(proj, proj, proj, mu, wuf, wub, au, gu, w0, a0, xi, al, bd)


def _unit_tri_inverse(low):
    r = lax.broadcasted_iota(jnp.int32, (CHUNK, CHUNK), 0)
    c = lax.broadcasted_iota(jnp.int32, (CHUNK, CHUNK), 1)
    eye = jnp.where(r == c, 1.0, 0.0).astype(F32)
    inv = eye - low
    p = _mm3(low, low)
    span = 2
    while span < CHUNK:
        inv = _mm3(inv, eye + p)
        span *= 2
        if span < CHUNK:
            p = _mm3(p, p)
    return inv


def _rwkv_scan_kernel(*refs, seq, has_state):
    if has_state:
        r_ref, k_ref, v_ref, kap_ref, beta_ref, lw_ref, s0_ref, y_ref, st_ref, s_scr = refs
    else:
        r_ref, k_ref, v_ref, kap_ref, beta_ref, lw_ref, y_ref, st_ref, s_scr = refs
    n_chunks = seq // CHUNK
    d = pl.program_id(1)
    upper = d == 1
    rr = lax.broadcasted_iota(jnp.int32, (CHUNK, CHUNK), 0)
    cc = lax.broadcasted_iota(jnp.int32, (CHUNK, CHUNK), 1)
    diff = (rr - cc) * (1 - 2 * d)
    incl = diff >= 0
    strict = diff > 0
    tri = jnp.where(incl, 1.0, 0.0).astype(BF16)
    if has_state:
        s_scr[...] = s0_ref[...]
    else:
        s_scr[...] = jnp.zeros_like(s_scr)

    def body(i, carry):
        n = jnp.where(upper, n_chunks - 1 - i, i)
        rows = pl.ds(pl.multiple_of(n * CHUNK, CHUNK), CHUNK)
        lw = lw_ref[rows, :]
        cum = _cumsum(tri, lw)
        tot = jnp.where(upper, cum[0:1, :], cum[CHUNK - 1:CHUNK, :])
        e_neg = jnp.exp(-cum)
        e_tot = jnp.exp(tot)
        kt = kap_ref[rows, :] * jnp.exp(cum - lw)
        bt = beta_ref[rows, :] * e_neg
        kk = k_ref[rows, :] * e_neg
        rt = r_ref[rows, :] * jnp.exp(cum)
        kh = kk * e_tot
        bh = bt * e_tot
        vv = v_ref[rows, :]
        ys = []
        for h in range(RWKV_HEADS):
            sl = slice(h * RWKV_HD, (h + 1) * RWKV_HD)
            s = s_scr[h]
            kt_h, bt_h, kk_h, rt_h, v_h = kt[:, sl], bt[:, sl], kk[:, sl], rt[:, sl], vv[:, sl]
            low = jnp.where(strict, _mm3(kt_h, bt_h, _NT), 0.0)
            inv = _unit_tri_inverse(low)
            g1 = jnp.where(strict, _mm3(kt_h, kk_h, _NT), 0.0)
            z = _mm3(kt_h, s, _NT) + _mm3(g1, v_h)
            u = _mm3(inv, z)
            g2 = jnp.where(incl, _mm3(rt_h, kk_h, _NT), 0.0)
            g3 = jnp.where(incl, _mm3(rt_h, bt_h, _NT), 0.0)
            ys.append(_mm3(rt_h, s, _NT) + _mm3(g2, v_h) - _mm3(g3, u))
            s_scr[h] = s * e_tot[:, sl] + _mm3(v_h, kh[:, sl], _TN) - _mm3(u, bh[:, sl], _TN)
        y_ref[rows, :] = jnp.concatenate(ys, axis=1)
        return carry

    lax.fori_loop(0, n_chunks, body, 0)
    st_ref[...] = s_scr[...]


def _rwkv_scan(r, k, v, kap, beta, lw, s0, layer, seq, nb, row0):
    rb = row0 // seq
    has_state = s0 is not None
    tok = pl.BlockSpec((seq, RWKV_W), lambda b, d: (rb + b, 0))
    in_specs = [tok, tok, tok, tok, tok,
                pl.BlockSpec((None, seq, RWKV_W), lambda b, d: (d, rb + b, 0))]
    args = [r, k, v, kap, beta, lw]
    st_shape = (RWKV_HEADS, RWKV_HD, RWKV_HD)
    if has_state:
        in_specs.append(pl.BlockSpec((None, None, None) + st_shape, lambda b, d: (b, layer, d, 0, 0, 0)))
        args.append(s0)
    return pl.pallas_call(
        functools.partial(_rwkv_scan_kernel, seq=seq, has_state=has_state),
        grid=(nb, 2),
        in_specs=in_specs,
        out_specs=[
            pl.BlockSpec((None, seq, RWKV_W), lambda b, d: (d, b, 0)),
            pl.BlockSpec((None, None) + st_shape, lambda b, d: (b, d, 0, 0, 0)),
        ],
        out_shape=[
            jax.ShapeDtypeStruct((2, nb * seq, RWKV_W), F32),
            jax.ShapeDtypeStruct((nb, 2) + st_shape, F32),
        ],
        scratch_shapes=[pltpu.VMEM(st_shape, F32)],
        compiler_params=_cparams(("arbitrary", "arbitrary")),
        name="rwkv_scan_lat" if has_state else "rwkv_scan_ctx",
    )(*args)


def _rwkv_post_kernel(y_ref, r_ref, k_ref, v_ref, g_ref, rho_ref, lnw_ref, lnb_ref, bd_ref, o_ref):
    v = v_ref[...]
    bonus = _segsum(r_ref[...] * k_ref[...] * rho_ref[...], bd_ref) * v
    y = y_ref[0] + y_ref[1] + bonus
    inv_n = 1.0 / RWKV_HD
    yc = y - _segsum(y, bd_ref) * inv_n
    var = _segsum(yc * yc, bd_ref) * inv_n
    yn = yc * lax.rsqrt(var + RWKV_LN_EPS)
    o_ref[...] = ((yn * lnw_ref[...] + lnb_ref[...]) * g_ref[...]).astype(BF16)


def _rwkv_post(y, r, k, v, g, rho, lnw, lnb, bd):
    tm = 256
    tok = pl.BlockSpec((tm, RWKV_W), lambda i: (i, 0))
    vec = pl.BlockSpec((1, RWKV_W), lambda i: (0, 0))
    return pl.pallas_call(
        _rwkv_post_kernel,
        grid=(N_TOK // tm,),
        in_specs=[pl.BlockSpec((2, tm, RWKV_W), lambda i: (0, i, 0)), tok, tok, tok, tok,
                  vec, vec, vec, pl.BlockSpec((RWKV_W, RWKV_W), lambda i: (0, 0))],
        out_specs=tok,
        out_shape=jax.ShapeDtypeStruct((N_TOK, RWKV_W), BF16),
        compiler_params=_cparams(("arbitrary",)),
        name="rwkv_post",
    )(y, r, k, v, g, rho, lnw, lnb, bd)


def _merge_kernel(oa_ref, og_ref, orw_ref, gate_ref, x_ref, mod_ref, wa_ref, wg_ref, wr_ref, wo_ref, o_ref):
    d = D_MODEL
    merged = jax.nn.sigmoid(gate_ref[:, 0:d]) * _dg(oa_ref[...], wa_ref[...])
    merged += jax.nn.sigmoid(gate_ref[:, d:2 * d]) * _dg(og_ref[...], wg_ref[...])
    merged += jax.nn.sigmoid(gate_ref[:, 2 * d:3 * d]) * _dg(orw_ref[...], wr_ref[...])
    out = _dg(_bf(merged), wo_ref[...])
    o_ref[...] = x_ref[...] + mod_ref[2:3, :] * out


def _merge(o_att, o_gla, o_rwkv, gates, x, mod, wa, wg, wr, wo):
    tm = 256
    const = lambda shape: pl.BlockSpec(shape, lambda i: (0, 0), pipeline_mode=pl.Buffered(1))
    return pl.pallas_call(
        _merge_kernel,
        grid=(N_TOK // tm,),
        in_specs=[
            pl.BlockSpec((tm, ATT_Q_W), lambda i: (i, 0)),
            pl.BlockSpec((tm, GLA_W), lambda i: (i, 0)),
            pl.BlockSpec((tm, RWKV_W), lambda i: (i, 0)),
            pl.BlockSpec((tm, GATE_SEG), lambda i: (i, 0)),
            pl.BlockSpec((tm, D_MODEL), lambda i: (i, 0)),
            pl.BlockSpec((None, 6, D_MODEL), lambda i: (_mod_row(i, tm), 0, 0)),
            const((ATT_Q_W, D_MODEL)), const((GLA_W, D_MODEL)), const((RWKV_W, D_MODEL)),
            const((D_MODEL, D_MODEL)),
        ],
        out_specs=pl.BlockSpec((tm, D_MODEL), lambda i: (i, 0)),
        out_shape=jax.ShapeDtypeStruct((N_TOK, D_MODEL), F32),
        compiler_params=_cparams(("arbitrary",)),
        name="merge_out",
    )(o_att, o_gla, o_rwkv, gates, x, mod, wa, wg, wr, wo)


def _ffn_kernel(h_ref, hp_ref, hn_ref, wv_ref, wg_ref, cwv_ref, cwg_ref, cbv_ref, cbg_ref, wd_ref,
                x_ref, mod_ref, o_ref, acc_ref):
    tm = h_ref.shape[0]
    i = pl.program_id(0)
    j = pl.program_id(1)
    pos, last = _seq_pos(i, tm)
    row = lax.broadcasted_iota(jnp.int32, (tm, 1), 0)
    h = h_ref[...]
    hp = hp_ref[...]
    hn = hn_ref[...]

    def conv(w_ref, cw_ref, cb_ref):
        w = w_ref[...]
        u = _dg(h, w)
        up = _dg(hp, w)[7:8, :]
        un = _dg(hn, w)[0:1, :]
        prev = jnp.where(row == 0, up, pltpu.roll(u, 1, 0))
        nxt = jnp.where(row == tm - 1, un, pltpu.roll(u, tm - 1, 0))
        prev = jnp.where(pos == 0, 0.0, prev)
        nxt = jnp.where(pos == last, 0.0, nxt)
        return cw_ref[0:1, :] * prev + cw_ref[1:2, :] * u + cw_ref[2:3, :] * nxt + cb_ref[...]

    val = conv(wv_ref, cwv_ref, cbv_ref)
    gate = conv(wg_ref, cwg_ref, cbg_ref)
    act = gate * jax.nn.sigmoid(gate) * val
    part = _dg(_bf(act), wd_ref[...])

    @pl.when(j == 0)
    def _():
        acc_ref[...] = part

    @pl.when(j > 0)
    def _():
        acc_ref[...] += part

    @pl.when(j == pl.num_programs(1) - 1)
    def _():
        o_ref[...] = x_ref[...] + mod_ref[5:6, :] * acc_ref[...]


def _ffn(h, x, mod, w_up, conv_w, conv_b, w_down):
    tm = 512
    tn = 512
    nj = FFN_PAD // tn
    hb = tm // 8
    return pl.pallas_call(
        _ffn_kernel,
        grid=(N_TOK // tm, nj),
        in_specs=[
            pl.BlockSpec((tm, D_MODEL), lambda i, j: (i, 0)),
            pl.BlockSpec((8, D_MODEL), lambda i, j: (jnp.maximum(i * hb - 1, 0), 0)),
            pl.BlockSpec((8, D_MODEL), lambda i, j: (jnp.minimum((i + 1) * hb, N_TOK // 8 - 1), 0)),
            pl.BlockSpec((D_MODEL, tn), lambda i, j: (0, j)),
            pl.BlockSpec((D_MODEL, tn), lambda i, j: (0, nj + j)),
            pl.BlockSpec((3, tn), lambda i, j: (0, j)),
            pl.BlockSpec((3, tn), lambda i, j: (0, nj + j)),
            pl.BlockSpec((1, tn), lambda i, j: (0, j)),
            pl.BlockSpec((1, tn), lambda i, j: (0, nj + j)),
            pl.BlockSpec((tn, D_MODEL), lambda i, j: (j, 0)),
            pl.BlockSpec((tm, D_MODEL), lambda i, j: (i, 0)),
            pl.BlockSpec((None, 6, D_MODEL), lambda i, j: (_mod_row(i, tm), 0, 0)),
        ],
        out_specs=pl.BlockSpec((tm, D_MODEL), lambda i, j: (i, 0)),
        out_shape=jax.ShapeDtypeStruct((N_TOK, D_MODEL), F32),
        scratch_shapes=[pltpu.VMEM((tm, D_MODEL), F32)],
        compiler_params=_cparams(("arbitrary", "arbitrary")),
        name="conv_ffn",
    )(h, h, h, w_up, w_up, conv_w, conv_w, conv_b, conv_b, w_down, x, mod)


def _pad_cols(x, n):
    return jnp.pad(x, ((0, 0), (0, n - x.shape[1])))


def _pad_rows(x, n, at=0):
    return jnp.pad(x, ((at, n - x.shape[0] - at), (0, 0)))


def _rwkv_cols(x):
    w = RWKV_W
    pieces = [x[:, :3 * w]]
    off = 3 * w
    for width in (RWKV_W_RANK, RWKV_W_RANK, RWKV_A_RANK, RWKV_G_RANK):
        pieces.append(_pad_cols(x[:, off:off + width], LANE))
        off += width
    return jnp.concatenate(pieces, axis=1)


def _rope_tables(t):
    rows = t // GRID_W
    row = jnp.repeat(jnp.arange(rows, dtype=F32), GRID_W)
    col = jnp.tile(jnp.arange(GRID_W, dtype=F32), rows)
    half = HEAD_DIM // 2
    inv = ROPE_THETA ** (-jnp.arange(0, half, 2, dtype=F32) / half)

    def cos_sin(pos):
        ang = pos[:, None] * inv[None, :]
        ang = jnp.concatenate([ang, ang], axis=-1)
        return jnp.cos(ang), jnp.sin(ang)

    cos_r, sin_r = cos_sin(row)
    cos_c, sin_c = cos_sin(col)
    cos = jnp.concatenate([cos_r, cos_c], axis=-1)
    sin = jnp.concatenate([sin_r, sin_c], axis=-1)
    sign = jnp.where((jnp.arange(HEAD_DIM) % half) < half // 2, -1.0, 1.0).astype(F32)
    return cos, sin * sign[None, :]


def kernel(x_prompt, x_sample, cache_k, cache_v, state_gla, state_rwkv, c, c_ctx, w_mod, b_mod, norm_mix, w_in, q_norm, k_norm, gla_a_up, gla_a_bias, gla_norm, rwkv_mu, rwkv_w0, rwkv_w_up, rwkv_a0, rwkv_a_up, rwkv_g_up, rwkv_k_xi, rwkv_k_alpha, rwkv_bonus, rwkv_ln_w, rwkv_ln_b, w_br_att, w_br_gla, w_br_rwkv, w_out, norm_ffn, ffn_up, ffn_conv_w, ffn_conv_b, ffn_down):
    x = jnp.concatenate([x_prompt.reshape(N_CTX, D_MODEL), x_sample.reshape(N_LAT, D_MODEL)], axis=0)
    cvec = jnp.concatenate([c_ctx[None, :], c, jnp.zeros((N_MODROWS - 1 - DEC_BATCH, D_MODEL), F32)], axis=0)
    mod_all = _modulation(cvec, w_mod, b_mod).reshape(DEPTH, N_MODROWS, 6, D_MODEL)
    cos, sin = _rope_tables(DEC_SEQ)
    hr = lax.broadcasted_iota(jnp.int32, (RWKV_W, RWKV_W), 0) // RWKV_HD
    hc = lax.broadcasted_iota(jnp.int32, (RWKV_W, RWKV_W), 1) // RWKV_HD
    bd = (hr == hc).astype(BF16)

    ks, vs, gla_states, rwkv_states = [], [], [], []
    for l in range(DEPTH):
        mod = mod_all[l]
        wi = w_in[l]
        o_gla0 = ATT_SEG
        o_rw0 = o_gla0 + 4 * GLA_W + 2 * GLA_RANK
        o_gate0 = o_rw0 + RWKV_COLS
        w_att = _bf(wi[:, :ATT_SEG])
        w_gla = _bf(_pad_cols(wi[:, o_gla0:o_rw0], GLA_SEG))
        w_rw = _bf(_rwkv_cols(wi[:, o_rw0:o_gate0]))
        w_gate = _bf(wi[:, o_gate0:])

        h = _prenorm(x, norm_mix[l], mod, 0)
        p_att = _matmul(h, w_att, 512, "proj_att")
        p_gla = _matmul(h, w_gla, GLA_SEG, "proj_gla")
        p_rw = _matmul(h, w_rw, RWKV_SEG, "proj_rwkv")
        gates = _matmul(h, w_gate, 1024, "proj_gates")

        qn = q_norm[l].reshape(1, HEAD_DIM)
        kn = k_norm[l].reshape(1, HEAD_DIM)
        oa_c, k_c, v_c = _attention_ctx(p_att, qn, kn)
        oa_l = _attention_lat(p_att, qn, kn, cos, sin, cache_k, cache_v, l)
        o_att = jnp.concatenate([oa_c, oa_l], axis=0)
        ks.append(k_c)
        vs.append(v_c)

        up = gla_a_up[l].reshape(2, GLA_RANK, GLA_HEADS, GLA_DK).transpose(0, 2, 1, 3)
        up_pad = jnp.stack([
            jnp.pad(up[0], ((0, 0), (0, LANE - GLA_RANK), (0, 0))),
            jnp.pad(up[1], ((0, 0), (GLA_RANK, LANE - 2 * GLA_RANK), (0, 0))),
        ])
        gbias = gla_a_bias[l].reshape(2, GLA_HEADS, 1, GLA_DK)
        gnorm = gla_norm[l].reshape(1, GLA_DK)
        og_c, gs_c = _gla(p_gla, up_pad, gbias, gnorm, None, l, SEQ, BATCH, 0)
        og_l, _ = _gla(p_gla, up_pad, gbias, gnorm, state_gla, l, DEC_SEQ, DEC_BATCH, N_CTX)
        o_gla = _bf(jnp.concatenate([og_c, og_l], axis=0))
        gla_states.append(gs_c)

        row = lambda a: a.reshape(1, -1)
        mu = _rwkv_cols(row(rwkv_mu[l]))
        wuf = _bf(_pad_rows(rwkv_w_up[l, 0], LANE))
        wub = _bf(_pad_rows(rwkv_w_up[l, 1], LANE))
        au = _bf(_pad_rows(rwkv_a_up[l], LANE))
        gu = _bf(rwkv_g_up[l])
        r_, k_, v_, kap_, beta_, lw_, g_ = _rwkv_prep(
            p_rw, mu, wuf, wub, au, gu, rwkv_w0[l], row(rwkv_a0[l]), row(rwkv_k_xi[l]),
            row(rwkv_k_alpha[l]), bd)
        y_c, rs_c = _rwkv_scan(r_, k_, v_, kap_, beta_, lw_, None, l, SEQ, BATCH, 0)
        y_l, _ = _rwkv_scan(r_, k_, v_, kap_, beta_, lw_, state_rwkv, l, DEC_SEQ, DEC_BATCH, N_CTX)
        y = jnp.concatenate([y_c, y_l], axis=1)
        o_rw = _rwkv_post(y, r_, k_, v_, g_, row(rwkv_bonus[l]), row(rwkv_ln_w[l]), row(rwkv_ln_b[l]), bd)
        rwkv_states.append(rs_c)

        x = _merge(o_att, o_gla, o_rw, gates, x, mod, _bf(w_br_att[l]), _bf(w_br_gla[l]),
                   _bf(w_br_rwkv[l]), _bf(w_out[l]))

        h2 = _prenorm(x, norm_ffn[l], mod, 3)
        fu = ffn_up[l]
        w_up = _bf(jnp.concatenate([_pad_cols(fu[:, :FFN_HIDDEN], FFN_PAD),
                                    _pad_cols(fu[:, FFN_HIDDEN:], FFN_PAD)], axis=1))
        cw = ffn_conv_w[l]
        conv_w = jnp.concatenate([_pad_cols(cw[:, :FFN_HIDDEN], FFN_PAD),
                                  _pad_cols(cw[:, FFN_HIDDEN:], FFN_PAD)], axis=1)
        cb = row(ffn_conv_b[l])
        conv_b = jnp.concatenate([_pad_cols(cb[:, :FFN_HIDDEN], FFN_PAD),
                                  _pad_cols(cb[:, FFN_HIDDEN:], FFN_PAD)], axis=1)
        w_down = _bf(_pad_rows(ffn_down[l], FFN_PAD))
        x = _ffn(h2, x, mod, w_up, conv_w, conv_b, w_down)

    y_prompt = x[:N_CTX].reshape(BATCH, SEQ, D_MODEL)
    y_sample = x[N_CTX:].reshape(DEC_BATCH, DEC_SEQ, D_MODEL)
    new_cache_k = jnp.stack(ks, axis=1)
    new_cache_v = jnp.stack(vs, axis=1)
    new_state_gla = jnp.stack(gla_states, axis=1)
    new_state_rwkv = jnp.stack(rwkv_states, axis=1)
    return (y_prompt, y_sample, new_cache_k, new_cache_v, new_state_gla, new_state_rwkv)
```

```python
import functools

import jax
import jax.numpy as jnp
from jax import lax
from jax.experimental import pallas as pl
from jax.experimental.pallas import tpu as pltpu

F32 = jnp.float32
BF16 = jnp.bfloat16

D_MODEL = 2048
BATCH = 16
SEQ = 256
DEPTH = 2
DEC_BATCH = 4
DEC_SEQ = 1024
PAST_LEN = 256
GRID_W = 64
HEAD_DIM = 128
ATT_HEADS = 8
ATT_KV_HEADS = 2
ROPE_THETA = 10000.0
GLA_HEADS = 4
GLA_DK = 128
GLA_RANK = 16
GLA_TAU = 16.0
RWKV_HEADS = 8
RWKV_HD = 64
RWKV_W_RANK = 64
RWKV_A_RANK = 64
RWKV_G_RANK = 128
RWKV_DECAY_SCALE = 0.606531
RWKV_LN_EPS = 64e-5
FFN_HIDDEN = 5504
EPS = 1e-6

ATT_Q_W = ATT_HEADS * HEAD_DIM
ATT_KV_W = ATT_KV_HEADS * HEAD_DIM
GLA_W = GLA_HEADS * GLA_DK
RWKV_W = RWKV_HEADS * RWKV_HD
RWKV_COLS = 3 * RWKV_W + 2 * RWKV_W_RANK + RWKV_A_RANK + RWKV_G_RANK

LANE = 128
CHUNK = 64
N_CTX = BATCH * SEQ
N_LAT = DEC_BATCH * DEC_SEQ
N_TOK = N_CTX + N_LAT
N_MODROWS = 8
FFN_PAD = 5632
ATT_SEG = ATT_Q_W + 2 * ATT_KV_W
GLA_SEG = 4 * GLA_W + LANE
RWKV_SEG = 3 * RWKV_W + 4 * LANE
GATE_SEG = 3 * D_MODEL
VMEM_LIMIT = 56 * 1024 * 1024

_NT = (((1,), (1,)), ((), ()))
_TN = (((0,), (0,)), ((), ()))
_NN = (((1,), (0,)), ((), ()))


def _bf(x):
    return x.astype(BF16)


def _dg(a, b, dims=_NN):
    return lax.dot_general(a, b, dims, preferred_element_type=F32)


def _mm(a, b, dims=_NN):
    return _dg(_bf(a), _bf(b), dims)


def _split2(x):
    h = x.astype(BF16)
    return h, (x - h.astype(F32)).astype(BF16)


def _split3(x):
    h = x.astype(BF16)
    r = x - h.astype(F32)
    m = r.astype(BF16)
    return h, m, (r - m.astype(F32)).astype(BF16)


def _mm3(a, b, dims=_NN):
    ah, al = _split2(a)
    bh, bl = _split2(b)
    return _dg(ah, bh, dims) + (_dg(al, bh, dims) + _dg(ah, bl, dims))


def _mm_exact_rhs(a, b_bf, dims=_NN):
    h, m, l = _split3(a)
    return _dg(h, b_bf, dims) + (_dg(m, b_bf, dims) + _dg(l, b_bf, dims))


def _rms(x, g):
    return x * lax.rsqrt(jnp.mean(x * x, axis=-1, keepdims=True) + EPS) * g


def _cparams(sem):
    return pltpu.CompilerParams(dimension_semantics=sem, vmem_limit_bytes=VMEM_LIMIT)


def _mod_row(i, tm):
    n_ctx = N_CTX // tm
    return jnp.where(i < n_ctx, 0, 1 + (i - n_ctx) // (DEC_SEQ // tm))


def _seq_pos(i, tm):
    g = i * tm + lax.broadcasted_iota(jnp.int32, (tm, 1), 0)
    in_ctx = g < N_CTX
    pos = jnp.where(in_ctx, g & (SEQ - 1), (g - N_CTX) & (DEC_SEQ - 1))
    last = jnp.where(in_ctx, SEQ - 1, DEC_SEQ - 1)
    return pos, last


def _mod_kernel(c_ref, w_ref, b_ref, o_ref):
    c = c_ref[...]
    s = c * jax.nn.sigmoid(c)
    o_ref[...] = _mm(s, w_ref[...]) + b_ref[...]


def _modulation(cvec, w_mod, b_mod):
    tn = 1024
    n = 6 * D_MODEL
    return pl.pallas_call(
        _mod_kernel,
        grid=(DEPTH, n // tn),
        in_specs=[
            pl.BlockSpec((N_MODROWS, D_MODEL), lambda l, j: (0, 0)),
            pl.BlockSpec((None, D_MODEL, tn), lambda l, j: (l, 0, j)),
            pl.BlockSpec((None, 1, tn), lambda l, j: (l, 0, j)),
        ],
        out_specs=pl.BlockSpec((None, N_MODROWS, tn), lambda l, j: (l, 0, j)),
        out_shape=jax.ShapeDtypeStruct((DEPTH, N_MODROWS, n), F32),
        compiler_params=_cparams(("arbitrary", "arbitrary")),
        name="modulation",
    )(cvec, w_mod, b_mod.reshape(DEPTH, 1, n))


def _prenorm_kernel(x_ref, g_ref, mod_ref, o_ref, *, shift_idx):
    y = _rms(x_ref[...], g_ref[...])
    sh = mod_ref[shift_idx:shift_idx + 1, :]
    sc = mod_ref[shift_idx + 1:shift_idx + 2, :]
    o_ref[...] = (y * (1.0 + sc) + sh).astype(BF16)


def _prenorm(x, g, mod, shift_idx):
    tm = 256
    return pl.pallas_call(
        functools.partial(_prenorm_kernel, shift_idx=shift_idx),
        grid=(N_TOK // tm,),
        in_specs=[
            pl.BlockSpec((tm, D_MODEL), lambda i: (i, 0)),
            pl.BlockSpec((1, D_MODEL), lambda i: (0, 0)),
            pl.BlockSpec((None, 6, D_MODEL), lambda i: (_mod_row(i, tm), 0, 0)),
        ],
        out_specs=pl.BlockSpec((tm, D_MODEL), lambda i: (i, 0)),
        out_shape=jax.ShapeDtypeStruct((N_TOK, D_MODEL), BF16),
        compiler_params=_cparams(("arbitrary",)),
        name="prenorm",
    )(x, g.reshape(1, D_MODEL), mod)


def _matmul_kernel(a_ref, b_ref, o_ref):
    o_ref[...] = _dg(a_ref[...], b_ref[...]).astype(o_ref.dtype)


def _matmul(a, b, tn, name):
    m, k = a.shape
    n = b.shape[1]
    tm = 1024
    return pl.pallas_call(
        _matmul_kernel,
        grid=(n // tn, m // tm),
        in_specs=[
            pl.BlockSpec((tm, k), lambda j, i: (i, 0)),
            pl.BlockSpec((k, tn), lambda j, i: (0, j)),
        ],
        out_specs=pl.BlockSpec((tm, tn), lambda j, i: (i, j)),
        out_shape=jax.ShapeDtypeStruct((m, n), F32),
        compiler_params=_cparams(("arbitrary", "arbitrary")),
        name=name,
    )(a, b)


_ATT_SCALE = HEAD_DIM ** -0.5


def _rope(x, cos, sin_signed):
    lane = lax.broadcasted_iota(jnp.int32, x.shape, 1)
    partner = jnp.where((lane & 63) < 32, pltpu.roll(x, 96, 1), pltpu.roll(x, 32, 1))
    return x * cos + partner * sin_signed


def _attn_ctx_kernel(q_ref, k_ref, v_ref, qn_ref, kn_ref, o_ref, ko_ref, vo_ref):
    q = _rms(q_ref[...], qn_ref[...])
    k = _rms(k_ref[...], kn_ref[...])
    v = v_ref[...]
    s = _mm(q, k, _NT) * _ATT_SCALE
    p = jnp.exp(s - jnp.max(s, axis=-1, keepdims=True))
    o = _mm(p, v) / jnp.sum(p, axis=-1, keepdims=True)
    o_ref[...] = o.astype(BF16)
    ko_ref[...] = k
    vo_ref[...] = v


def _attn_lat_kernel(q_ref, k_ref, v_ref, qn_ref, kn_ref, cos_ref, sin_ref, ck_ref, cv_ref, o_ref):
    cos = cos_ref[...]
    sin = sin_ref[...]
    q = _rope(_rms(q_ref[...], qn_ref[...]), cos, sin)
    k = _rope(_rms(k_ref[...], kn_ref[...]), cos, sin)
    qb = _bf(q)
    s1 = _dg(qb, _bf(ck_ref[...]), _NT) * _ATT_SCALE
    s2 = _dg(qb, _bf(k), _NT) * _ATT_SCALE
    m = jnp.maximum(jnp.max(s1, axis=-1, keepdims=True), jnp.max(s2, axis=-1, keepdims=True))
    p1 = jnp.exp(s1 - m)
    p2 = jnp.exp(s2 - m)
    den = jnp.sum(p1, axis=-1, keepdims=True) + jnp.sum(p2, axis=-1, keepdims=True)
    o = (_mm(p1, cv_ref[...]) + _mm(p2, v_ref[...])) / den
    o_ref[...] = o.astype(BF16)


def _attention_ctx(proj, qn, kn):
    g4 = ATT_HEADS // ATT_KV_HEADS
    blk = lambda f: pl.BlockSpec((SEQ, HEAD_DIM), f)
    vec = pl.BlockSpec((1, HEAD_DIM), lambda b, g, i: (0, 0))
    kv_out = pl.BlockSpec((None, None, SEQ, HEAD_DIM), lambda b, g, i: (b, g, 0, 0))
    return pl.pallas_call(
        _attn_ctx_kernel,
        grid=(BATCH, ATT_KV_HEADS, g4),
        in_specs=[
            blk(lambda b, g, i: (b, g * g4 + i)),
            blk(lambda b, g, i: (b, ATT_HEADS + g)),
            blk(lambda b, g, i: (b, ATT_HEADS + ATT_KV_HEADS + g)),
            vec, vec,
        ],
        out_specs=[blk(lambda b, g, i: (b, g * g4 + i)), kv_out, kv_out],
        out_shape=[
            jax.ShapeDtypeStruct((N_CTX, ATT_Q_W), BF16),
            jax.ShapeDtypeStruct((BATCH, ATT_KV_HEADS, SEQ, HEAD_DIM), F32),
            jax.ShapeDtypeStruct((BATCH, ATT_KV_HEADS, SEQ, HEAD_DIM), F32),
        ],
        compiler_params=_cparams(("arbitrary", "arbitrary", "arbitrary")),
        name="attention_ctx",
    )(proj, proj, proj, qn, kn)


def _attention_lat(proj, qn, kn, cos, sin, cache_k, cache_v, layer):
    g4 = ATT_HEADS // ATT_KV_HEADS
    rb = N_CTX // DEC_SEQ
    blk = lambda f: pl.BlockSpec((DEC_SEQ, HEAD_DIM), f)
    vec = pl.BlockSpec((1, HEAD_DIM), lambda b, g, i: (0, 0))
    tab = pl.BlockSpec((DEC_SEQ, HEAD_DIM), lambda b, g, i: (0, 0))
    cache = pl.BlockSpec((None, None, None, PAST_LEN, HEAD_DIM), lambda b, g, i: (b, layer, g, 0, 0))
    return pl.pallas_call(
        _attn_lat_kernel,
        grid=(DEC_BATCH, ATT_KV_HEADS, g4),
        in_specs=[
            blk(lambda b, g, i: (rb + b, g * g4 + i)),
            blk(lambda b, g, i: (rb + b, ATT_HEADS + g)),
            blk(lambda b, g, i: (rb + b, ATT_HEADS + ATT_KV_HEADS + g)),
            vec, vec, tab, tab, cache, cache,
        ],
        out_specs=blk(lambda b, g, i: (b, g * g4 + i)),
        out_shape=jax.ShapeDtypeStruct((N_LAT, ATT_Q_W), BF16),
        compiler_params=_cparams(("arbitrary", "arbitrary", "arbitrary")),
        name="attention_lat",
    )(proj, proj, proj, qn, kn, cos, sin, cache_k, cache_v)


def _tri(n, upper):
    r = lax.broadcasted_iota(jnp.int32, (n, n), 0)
    c = lax.broadcasted_iota(jnp.int32, (n, n), 1)
    return (r <= c) if upper else (r >= c)


def _gla_kernel(*refs, seq, has_state):
    if has_state:
        q_ref, k_ref, v_ref, gg_ref, gad_ref, up_ref, bias_ref, gn_ref, s0_ref, o_ref, st_ref = refs
    else:
        q_ref, k_ref, v_ref, gg_ref, gad_ref, up_ref, bias_ref, gn_ref, o_ref, st_ref = refs
        s0_ref = None
    nc = seq // CHUNK
    c3 = (nc, CHUNK, LANE)
    gad = _bf(gad_ref[...])
    q3 = (q_ref[...] * (GLA_DK ** -0.5)).reshape(c3)
    k3 = k_ref[...].reshape(c3)
    v3 = _bf(v_ref[...]).reshape(c3)
    bdot = lambda a, b, ca, cb: lax.dot_general(a, b, (((ca,), (cb,)), ((0,), (0,))), preferred_element_type=F32)

    qe, oi, kv, dec = [], [], [], []
    for d in range(2):
        incl = _tri(CHUNK, d == 1)
        tri = jnp.broadcast_to(jnp.where(incl, 1.0, 0.0).astype(BF16)[None], (nc, CHUNK, CHUNK))
        la = jax.nn.log_sigmoid(_dg(gad, _bf(up_ref[d])) + bias_ref[d]) / GLA_TAU
        hi, mid, lo = _split3(la.reshape(c3))
        cum = bdot(tri, hi, 2, 1) + (bdot(tri, mid, 2, 1) + bdot(tri, lo, 2, 1))
        tot = cum[:, 0:1, :] if d else cum[:, CHUNK - 1:CHUNK, :]
        qe_d = _bf(q3 * jnp.exp(cum))
        ke = _bf(k3 * jnp.exp(-cum))
        kl = _bf(k3 * jnp.exp(tot - cum))
        att = jnp.where(incl[None], bdot(qe_d, ke, 2, 2), 0.0)
        qe.append(qe_d)
        oi.append(bdot(_bf(att), v3, 2, 1))
        kv.append(bdot(v3, kl, 1, 1))
        dec.append(jnp.exp(tot))

    st = [s0_ref[d].T if has_state else jnp.zeros((GLA_DK, GLA_DK), F32) for d in range(2)]
    o_f, o_b = [None] * nc, [None] * nc
    for t in range(nc):
        nf, nb = t, nc - 1 - t
        o_f[nf] = oi[0][nf] + _dg(qe[0][nf], _bf(st[0]), _NT)
        o_b[nb] = oi[1][nb] + _dg(qe[1][nb], _bf(st[1]), _NT)
        st[0] = st[0] * dec[0][nf] + kv[0][nf]
        st[1] = st[1] * dec[1][nb] + kv[1][nb]
    st_ref[0] = st[0].T
    st_ref[1] = st[1].T
    o = jnp.concatenate([a + b for a, b in zip(o_f, o_b)], axis=0)
    g = gg_ref[...]
    o_ref[...] = (_rms(o, gn_ref[...]) * (g * jax.nn.sigmoid(g))).astype(o_ref.dtype)


def _cumsum(tri_bf, x):
    h, m, l = _split3(x)
    return _dg(tri_bf, h) + (_dg(tri_bf, m) + _dg(tri_bf, l))


def _gla(proj, up_pad, bias, gnorm, s0, layer, seq, nb, row0):
    rb = row0 // seq
    blk = lambda c: pl.BlockSpec((seq, LANE), lambda b, h: (rb + b, c * GLA_HEADS + h))
    has_state = s0 is not None
    in_specs = [
        blk(0), blk(1), blk(2), blk(3),
        pl.BlockSpec((seq, LANE), lambda b, h: (rb + b, 4 * GLA_HEADS)),
        pl.BlockSpec((2, None, LANE, LANE), lambda b, h: (0, h, 0, 0)),
        pl.BlockSpec((2, None, 1, LANE), lambda b, h: (0, h, 0, 0)),
        pl.BlockSpec((1, LANE), lambda b, h: (0, 0)),
    ]
    args = [proj, proj, proj, proj, proj, up_pad, bias, gnorm]
    if has_state:
        in_specs.append(pl.BlockSpec((None, None, 2, None, GLA_DK, GLA_DK),
                                     lambda b, h: (b, layer, 0, h, 0, 0)))
        args.append(s0)
    return pl.pallas_call(
        functools.partial(_gla_kernel, seq=seq, has_state=has_state),
        grid=(nb, GLA_HEADS),
        in_specs=in_specs,
        out_specs=[
            pl.BlockSpec((seq, LANE), lambda b, h: (b, h)),
            pl.BlockSpec((None, 2, None, GLA_DK, GLA_DK), lambda b, h: (b, 0, h, 0, 0)),
        ],
        out_shape=[
            jax.ShapeDtypeStruct((nb * seq, GLA_W), BF16),
            jax.ShapeDtypeStruct((nb, 2, GLA_HEADS, GLA_DK, GLA_DK), F32),
        ],
        compiler_params=_cparams(("arbitrary", "arbitrary")),
        name="gla_lat" if has_state else "gla_ctx",
    )(*args)


def _segsum(x, bd_ref):
    return _mm_exact_rhs(x, bd_ref[...])


def _rwkv_prep_kernel(x_ref, xp_ref, xn_ref, mu_ref, wuf_ref, wub_ref, au_ref, gu_ref, w0_ref,
                      a0_ref, xi_ref, al_ref, bd_ref,
                      r_ref, k_ref, v_ref, kap_ref, beta_ref, lw_ref, g_ref):
    tm = x_ref.shape[0]
    i = pl.program_id(0)
    pos, last = _seq_pos(i, tm)
    x = x_ref[...]
    row = lax.broadcasted_iota(jnp.int32, (tm, 1), 0)
    prev = jnp.where(row == 0, xp_ref[7:8, :], pltpu.roll(x, 1, 0))
    nxt = jnp.where(row == tm - 1, xn_ref[0:1, :], pltpu.roll(x, tm - 1, 0))
    prev = jnp.where(pos == 0, 0.0, prev)
    nxt = jnp.where(pos == last, 0.0, nxt)
    rw = x + (0.5 * (prev + nxt) - x) * mu_ref[...]
    w = RWKV_W
    rr = rw[:, 0:w]
    rk = rw[:, w:2 * w]
    rv = rw[:, 2 * w:3 * w]
    rwd_f = rw[:, 3 * w:3 * w + LANE]
    rwd_b = rw[:, 3 * w + LANE:3 * w + 2 * LANE]
    rad = rw[:, 3 * w + 2 * LANE:3 * w + 3 * LANE]
    rgd = rw[:, 3 * w + 3 * LANE:3 * w + 4 * LANE]
    lw_ref[0] = -RWKV_DECAY_SCALE * jax.nn.sigmoid(w0_ref[0:1, :] + _dg(_bf(jnp.tanh(rwd_f)), wuf_ref[...]))
    lw_ref[1] = -RWKV_DECAY_SCALE * jax.nn.sigmoid(w0_ref[1:2, :] + _dg(_bf(jnp.tanh(rwd_b)), wub_ref[...]))
    a = jax.nn.sigmoid(a0_ref[...] + _dg(_bf(rad), au_ref[...]))
    g_ref[...] = _dg(_bf(jax.nn.sigmoid(rgd)), gu_ref[...])
    kap = rk * xi_ref[...]
    kap = kap * lax.rsqrt(_segsum(kap * kap, bd_ref) + EPS)
    r_ref[...] = rr
    k_ref[...] = rk * (1.0 + (a - 1.0) * al_ref[...])
    v_ref[...] = rv
    kap_ref[...] = kap
    beta_ref[...] = kap * a


def _rwkv_prep(proj, mu, wuf, wub, au, gu, w0, a0, xi, al, bd):
    tm = 256
    nt = N_TOK // tm
    hb = tm // 8
    full = lambda shape: pl.BlockSpec(shape, lambda i: (0,) * len(shape))
    tok = pl.BlockSpec((tm, RWKV_W), lambda i: (i, 0))
    tshape = jax.ShapeDtypeStruct((N_TOK, RWKV_W), F32)
    return pl.pallas_call(
        _rwkv_prep_kernel,
        grid=(nt,),
        in_specs=[
            pl.BlockSpec((tm, RWKV_SEG), lambda i: (i, 0)),
            pl.BlockSpec((8, RWKV_SEG), lambda i: (jnp.maximum(i * hb - 1, 0), 0)),
            pl.BlockSpec((8, RWKV_SEG), lambda i: (jnp.minimum((i + 1) * hb, N_TOK // 8 - 1), 0)),
            full((1, RWKV_SEG)),
            full((LANE, RWKV_W)), full((LANE, RWKV_W)), full((LANE, RWKV_W)), full((LANE, RWKV_W)),
            full((2, RWKV_W)), full((1, RWKV_W)), full((1, RWKV_W)), full((1, RWKV_W)),
            full((RWKV_W, RWKV_W)),
        ],
        out_specs=[tok, tok, tok, tok, tok,
                   pl.BlockSpec((2, tm, RWKV_W), lambda i: (0, i, 0)), tok],
        out_shape=[tshape, tshape, tshape, tshape, tshape,
                   jax.ShapeDtypeStruct((2, N_TOK, RWKV_W), F32), tshape],
        compiler_params=_cparams(("arbitrary",)),
        name="rwkv_prep",
    )(proj, proj, proj, mu, wuf, wub, au, gu, w0, a0, xi, al, bd)


def _unit_tri_inverse_corr(lows):
    c = [-low for low in lows]
    pb = [_bf(low) for low in lows]
    p = [_dg(b, b) for b in pb]
    span = 2
    while span < CHUNK:
        pb = [_bf(x) for x in p]
        c = [a + x + _dg(_bf(a), xb) for a, x, xb in zip(c, p, pb)]
        span *= 2
        if span < CHUNK:
            p = [_dg(xb, xb) for xb in pb]
    return c


def _rwkv_scan_kernel(*refs, seq, has_state):
    if has_state:
        r_ref, k_ref, v_ref, kap_ref, beta_ref, lw_ref, s0_ref, y_ref, st_ref, s_scr = refs
    else:
        r_ref, k_ref, v_ref, kap_ref, beta_ref, lw_ref, y_ref, st_ref, s_scr = refs
    n_chunks = seq // CHUNK
    c64 = CHUNK
    rr = lax.broadcasted_iota(jnp.int32, (c64, c64), 0)
    cc = lax.broadcasted_iota(jnp.int32, (c64, c64), 1)
    incl = (rr >= cc, rr <= cc)
    strict = (rr > cc, rr < cc)
    tri = tuple(jnp.where(m, 1.0, 0.0).astype(BF16) for m in incl)
    if has_state:
        s_scr[...] = s0_ref[...]
    else:
        s_scr[...] = jnp.zeros_like(s_scr)
    heads = range(RWKV_HEADS)
    cut = lambda x: [x[:, h * RWKV_HD:(h + 1) * RWKV_HD] for h in heads]

    def body(i, carry):
        lhs, rhs, upd, v_h, et_h, s, ms, mi, rows_d = [], [], [], [], [], [], [], [], []
        for d in range(2):
            n = (n_chunks - 1 - i) if d else i
            rows = pl.ds(pl.multiple_of(n * c64, c64), c64)
            lw = lw_ref[d, rows, :]
            cum = _cumsum(tri[d], lw)
            tot = cum[0:1, :] if d else cum[c64 - 1:c64, :]
            e_neg = jnp.exp(-cum)
            e_tot = jnp.exp(tot)
            kt = kap_ref[rows, :] * jnp.exp(cum - lw)
            bt = beta_ref[rows, :] * e_neg
            kk = k_ref[rows, :] * e_neg
            rt = r_ref[rows, :] * jnp.exp(cum)
            lhs += cut(_bf(jnp.concatenate([kt, rt], axis=0)))
            rhs += cut(_bf(jnp.concatenate([bt, kk], axis=0)))
            upd += cut(_bf(jnp.concatenate([kk * e_tot, -(bt * e_tot)], axis=0)))
            v_h += cut(v_ref[rows, :])
            et_h += cut(e_tot)
            s += [s_scr[d, h] for h in heads]
            ms += [strict[d]] * RWKV_HEADS
            mi += [incl[d]] * RWKV_HEADS
            rows_d.append(rows)
        chains = range(2 * RWKV_HEADS)
        a1 = [_dg(lhs[c], jnp.concatenate([rhs[c], _bf(s[c])], axis=0), _NT) for c in chains]
        low = [jnp.where(ms[c], a1[c][:c64, 0:c64], 0.0) for c in chains]
        g1 = [jnp.where(ms[c], a1[c][:c64, c64:2 * c64], 0.0) for c in chains]
        corr = _unit_tri_inverse_corr(low)
        z = [a1[c][:c64, 2 * c64:] + _mm(g1[c], v_h[c]) for c in chains]
        u = [z[c] + _mm(corr[c], z[c]) for c in chains]
        vu = [_bf(jnp.concatenate([v_h[c], u[c]], axis=0)) for c in chains]
        g23 = [_bf(jnp.concatenate([jnp.where(mi[c], a1[c][c64:, c64:2 * c64], 0.0),
                                    jnp.where(mi[c], -a1[c][c64:, 0:c64], 0.0)], axis=1)) for c in chains]
        y = [a1[c][c64:, 2 * c64:] + _dg(g23[c], vu[c]) for c in chains]
        for c in chains:
            s_scr[c // RWKV_HEADS, c % RWKV_HEADS] = s[c] * et_h[c] + _dg(vu[c], upd[c], _TN)
        for d in range(2):
            y_ref[d, rows_d[d], :] = jnp.concatenate(y[d * RWKV_HEADS:(d + 1) * RWKV_HEADS], axis=1)
        return carry

    lax.fori_loop(0, n_chunks, body, 0)
    st_ref[...] = s_scr[...]


def _rwkv_scan(r, k, v, kap, beta, lw, s0, layer, seq, nb, row0):
    rb = row0 // seq
    has_state = s0 is not None
    tok = pl.BlockSpec((seq, RWKV_W), lambda b: (rb + b, 0))
    in_specs = [tok, tok, tok, tok, tok,
                pl.BlockSpec((2, seq, RWKV_W), lambda b: (0, rb + b, 0))]
    args = [r, k, v, kap, beta, lw]
    st_shape = (2, RWKV_HEADS, RWKV_HD, RWKV_HD)
    if has_state:
        in_specs.append(pl.BlockSpec((None, None) + st_shape, lambda b: (b, layer, 0, 0, 0, 0)))
        args.append(s0)
    return pl.pallas_call(
        functools.partial(_rwkv_scan_kernel, seq=seq, has_state=has_state),
        grid=(nb,),
        in_specs=in_specs,
        out_specs=[
            pl.BlockSpec((2, seq, RWKV_W), lambda b: (0, b, 0)),
            pl.BlockSpec((None,) + st_shape, lambda b: (b, 0, 0, 0, 0)),
        ],
        out_shape=[
            jax.ShapeDtypeStruct((2, nb * seq, RWKV_W), F32),
            jax.ShapeDtypeStruct((nb,) + st_shape, F32),
        ],
        scratch_shapes=[pltpu.VMEM(st_shape, F32)],
        compiler_params=_cparams(("arbitrary",)),
        name="rwkv_scan_lat" if has_state else "rwkv_scan_ctx",
    )(*args)


def _rwkv_post_kernel(y_ref, r_ref, k_ref, v_ref, g_ref, rho_ref, lnw_ref, lnb_ref, bd_ref, o_ref):
    v = v_ref[...]
    bonus = _segsum(r_ref[...] * k_ref[...] * rho_ref[...], bd_ref) * v
    y = y_ref[0] + y_ref[1] + bonus
    inv_n = 1.0 / RWKV_HD
    yc = y - _segsum(y, bd_ref) * inv_n
    var = _segsum(yc * yc, bd_ref) * inv_n
    yn = yc * lax.rsqrt(var + RWKV_LN_EPS)
    o_ref[...] = ((yn * lnw_ref[...] + lnb_ref[...]) * g_ref[...]).astype(BF16)


def _rwkv_post(y, r, k, v, g, rho, lnw, lnb, bd):
    tm = 256
    tok = pl.BlockSpec((tm, RWKV_W), lambda i: (i, 0))
    vec = pl.BlockSpec((1, RWKV_W), lambda i: (0, 0))
    return pl.pallas_call(
        _rwkv_post_kernel,
        grid=(N_TOK // tm,),
        in_specs=[pl.BlockSpec((2, tm, RWKV_W), lambda i: (0, i, 0)), tok, tok, tok, tok,
                  vec, vec, vec, pl.BlockSpec((RWKV_W, RWKV_W), lambda i: (0, 0))],
        out_specs=tok,
        out_shape=jax.ShapeDtypeStruct((N_TOK, RWKV_W), BF16),
        compiler_params=_cparams(("arbitrary",)),
        name="rwkv_post",
    )(y, r, k, v, g, rho, lnw, lnb, bd)


def _merge_kernel(oa_ref, og_ref, orw_ref, gate_ref, x_ref, mod_ref, wa_ref, wg_ref, wr_ref, wo_ref, o_ref):
    d = D_MODEL
    merged = jax.nn.sigmoid(gate_ref[:, 0:d]) * _dg(oa_ref[...], wa_ref[...])
    merged += jax.nn.sigmoid(gate_ref[:, d:2 * d]) * _dg(og_ref[...], wg_ref[...])
    merged += jax.nn.sigmoid(gate_ref[:, 2 * d:3 * d]) * _dg(orw_ref[...], wr_ref[...])
    out = _dg(_bf(merged), wo_ref[...])
    o_ref[...] = x_ref[...] + mod_ref[2:3, :] * out


def _merge(o_att, o_gla, o_rwkv, gates, x, mod, wa, wg, wr, wo):
    tm = 256
    const = lambda shape: pl.BlockSpec(shape, lambda i: (0, 0), pipeline_mode=pl.Buffered(1))
    return pl.pallas_call(
        _merge_kernel,
        grid=(N_TOK // tm,),
        in_specs=[
            pl.BlockSpec((tm, ATT_Q_W), lambda i: (i, 0)),
            pl.BlockSpec((tm, GLA_W), lambda i: (i, 0)),
            pl.BlockSpec((tm, RWKV_W), lambda i: (i, 0)),
            pl.BlockSpec((tm, GATE_SEG), lambda i: (i, 0)),
            pl.BlockSpec((tm, D_MODEL), lambda i: (i, 0)),
            pl.BlockSpec((None, 6, D_MODEL), lambda i: (_mod_row(i, tm), 0, 0)),
            const((ATT_Q_W, D_MODEL)), const((GLA_W, D_MODEL)), const((RWKV_W, D_MODEL)),
            const((D_MODEL, D_MODEL)),
        ],
        out_specs=pl.BlockSpec((tm, D_MODEL), lambda i: (i, 0)),
        out_shape=jax.ShapeDtypeStruct((N_TOK, D_MODEL), F32),
        compiler_params=_cparams(("arbitrary",)),
        name="merge_out",
    )(o_att, o_gla, o_rwkv, gates, x, mod, wa, wg, wr, wo)


def _ffn_up_kernel(h_ref, hp_ref, hn_ref, wv_ref, wg_ref, cwv_ref, cwg_ref, cbv_ref, cbg_ref, o_ref):
    tm = h_ref.shape[0]
    i = pl.program_id(1)
    pos, last = _seq_pos(i, tm)
    row = lax.broadcasted_iota(jnp.int32, (tm, 1), 0)
    h = h_ref[...]
    hp = hp_ref[...]
    hn = hn_ref[...]

    first_row = row == 0
    last_row = row == tm - 1
    seq_start = pos == 0
    seq_end = pos == last
    sub = 256
    n_sub = wv_ref.shape[1] // sub

    def up(w_ref, s):
        w = w_ref[:, s * sub:(s + 1) * sub]
        return _dg(h, w), _dg(hp, w)[7:8, :], _dg(hn, w)[0:1, :]

    def conv(us, cw_ref, cb_ref, s):
        u, u_before, u_after = us
        cols = slice(s * sub, (s + 1) * sub)
        prev = jnp.where(seq_start, 0.0, jnp.where(first_row, u_before, pltpu.roll(u, 1, 0)))
        nxt = jnp.where(seq_end, 0.0, jnp.where(last_row, u_after, pltpu.roll(u, tm - 1, 0)))
        return cw_ref[0:1, cols] * prev + cw_ref[1:2, cols] * u + cw_ref[2:3, cols] * nxt + cb_ref[:, cols]

    ups = [(up(wv_ref, s), up(wg_ref, s)) for s in range(n_sub)]
    for s in range(n_sub):
        val = conv(ups[s][0], cwv_ref, cbv_ref, s)
        gate = conv(ups[s][1], cwg_ref, cbg_ref, s)
        o_ref[:, s * sub:(s + 1) * sub] = _bf(gate * jax.nn.sigmoid(gate) * val)


def _ffn_down_kernel(a_ref, w_ref, x_ref, mod_ref, o_ref):
    o_ref[...] = x_ref[...] + mod_ref[5:6, :] * _dg(a_ref[...], w_ref[...])


def _ffn(h, x, mod, w_up, conv_w, conv_b, w_down):
    tm = 1024
    tn = 512
    nj = FFN_PAD // tn
    hb = tm // 8
    act = pl.pallas_call(
        _ffn_up_kernel,
        grid=(nj, N_TOK // tm),
        in_specs=[
            pl.BlockSpec((tm, D_MODEL), lambda j, i: (i, 0)),
            pl.BlockSpec((8, D_MODEL), lambda j, i: (jnp.maximum(i * hb - 1, 0), 0)),
            pl.BlockSpec((8, D_MODEL), lambda j, i: (jnp.minimum((i + 1) * hb, N_TOK // 8 - 1), 0)),
            pl.BlockSpec((D_MODEL, tn), lambda j, i: (0, j)),
            pl.BlockSpec((D_MODEL, tn), lambda j, i: (0, nj + j)),
            pl.BlockSpec((3, tn), lambda j, i: (0, j)),
            pl.BlockSpec((3, tn), lambda j, i: (0, nj + j)),
            pl.BlockSpec((1, tn), lambda j, i: (0, j)),
            pl.BlockSpec((1, tn), lambda j, i: (0, nj + j)),
        ],
        out_specs=pl.BlockSpec((tm, tn), lambda j, i: (i, j)),
        out_shape=jax.ShapeDtypeStruct((N_TOK, FFN_PAD), BF16),
        compiler_params=_cparams(("arbitrary", "arbitrary")),
        name="ffn_up",
    )(h, h, h, w_up, w_up, conv_w, conv_w, conv_b, conv_b)
    to = 512
    return pl.pallas_call(
        _ffn_down_kernel,
        grid=(D_MODEL // to, N_TOK // tm),
        in_specs=[
            pl.BlockSpec((tm, FFN_PAD), lambda n, i: (i, 0)),
            pl.BlockSpec((FFN_PAD, to), lambda n, i: (0, n)),
            pl.BlockSpec((tm, to), lambda n, i: (i, n)),
            pl.BlockSpec((None, 6, to), lambda n, i: (_mod_row(i, tm), 0, n)),
        ],
        out_specs=pl.BlockSpec((tm, to), lambda n, i: (i, n)),
        out_shape=jax.ShapeDtypeStruct((N_TOK, D_MODEL), F32),
        compiler_params=_cparams(("arbitrary", "arbitrary")),
        name="ffn_down",
    )(act, w_down, x, mod)


def _pad_cols(x, n):
    return jnp.pad(x, ((0, 0), (0, n - x.shape[1])))


def _pad_rows(x, n, at=0):
    return jnp.pad(x, ((at, n - x.shape[0] - at), (0, 0)))


def _rwkv_cols(x):
    w = RWKV_W
    pieces = [x[:, :3 * w]]
    off = 3 * w
    for width in (RWKV_W_RANK, RWKV_W_RANK, RWKV_A_RANK, RWKV_G_RANK):
        pieces.append(_pad_cols(x[:, off:off + width], LANE))
        off += width
    return jnp.concatenate(pieces, axis=1)


def _rope_tables(t):
    rows = t // GRID_W
    row = jnp.repeat(jnp.arange(rows, dtype=F32), GRID_W)
    col = jnp.tile(jnp.arange(GRID_W, dtype=F32), rows)
    half = HEAD_DIM // 2
    inv = ROPE_THETA ** (-jnp.arange(0, half, 2, dtype=F32) / half)

    def cos_sin(pos):
        ang = pos[:, None] * inv[None, :]
        ang = jnp.concatenate([ang, ang], axis=-1)
        return jnp.cos(ang), jnp.sin(ang)

    cos_r, sin_r = cos_sin(row)
    cos_c, sin_c = cos_sin(col)
    cos = jnp.concatenate([cos_r, cos_c], axis=-1)
    sin = jnp.concatenate([sin_r, sin_c], axis=-1)
    sign = jnp.where((jnp.arange(HEAD_DIM) % half) < half // 2, -1.0, 1.0).astype(F32)
    return cos, sin * sign[None, :]


def kernel(x_prompt, x_sample, cache_k, cache_v, state_gla, state_rwkv, c, c_ctx, w_mod, b_mod, norm_mix, w_in, q_norm, k_norm, gla_a_up, gla_a_bias, gla_norm, rwkv_mu, rwkv_w0, rwkv_w_up, rwkv_a0, rwkv_a_up, rwkv_g_up, rwkv_k_xi, rwkv_k_alpha, rwkv_bonus, rwkv_ln_w, rwkv_ln_b, w_br_att, w_br_gla, w_br_rwkv, w_out, norm_ffn, ffn_up, ffn_conv_w, ffn_conv_b, ffn_down):
    x = jnp.concatenate([x_prompt.reshape(N_CTX, D_MODEL), x_sample.reshape(N_LAT, D_MODEL)], axis=0)
    cvec = jnp.concatenate([c_ctx[None, :], c, jnp.zeros((N_MODROWS - 1 - DEC_BATCH, D_MODEL), F32)], axis=0)
    mod_all = _modulation(cvec, w_mod, b_mod).reshape(DEPTH, N_MODROWS, 6, D_MODEL)
    cos, sin = _rope_tables(DEC_SEQ)
    hr = lax.broadcasted_iota(jnp.int32, (RWKV_W, RWKV_W), 0) // RWKV_HD
    hc = lax.broadcasted_iota(jnp.int32, (RWKV_W, RWKV_W), 1) // RWKV_HD
    bd = (hr == hc).astype(BF16)

    ks, vs, gla_states, rwkv_states = [], [], [], []
    for l in range(DEPTH):
        mod = mod_all[l]
        wi = w_in[l]
        o_gla0 = ATT_SEG
        o_rw0 = o_gla0 + 4 * GLA_W + 2 * GLA_RANK
        o_gate0 = o_rw0 + RWKV_COLS
        w_att = _bf(wi[:, :ATT_SEG])
        w_gla = _bf(_pad_cols(wi[:, o_gla0:o_rw0], GLA_SEG))
        w_rw = _bf(_rwkv_cols(wi[:, o_rw0:o_gate0]))
        w_gate = _bf(wi[:, o_gate0:])

        h = _prenorm(x, norm_mix[l], mod, 0)
        p_att = _matmul(h, w_att, 512, "proj_att")
        p_gla = _matmul(h, w_gla, GLA_SEG, "proj_gla")
        p_rw = _matmul(h, w_rw, RWKV_SEG, "proj_rwkv")
        gates = _matmul(h, w_gate, 1024, "proj_gates")

        qn = q_norm[l].reshape(1, HEAD_DIM)
        kn = k_norm[l].reshape(1, HEAD_DIM)
        oa_c, k_c, v_c = _attention_ctx(p_att, qn, kn)
        oa_l = _attention_lat(p_att, qn, kn, cos, sin, cache_k, cache_v, l)
        o_att = jnp.concatenate([oa_c, oa_l], axis=0)
        ks.append(k_c)
        vs.append(v_c)

        up = gla_a_up[l].reshape(2, GLA_RANK, GLA_HEADS, GLA_DK).transpose(0, 2, 1, 3)
        up_pad = jnp.stack([
            jnp.pad(up[0], ((0, 0), (0, LANE - GLA_RANK), (0, 0))),
            jnp.pad(up[1], ((0, 0), (GLA_RANK, LANE - 2 * GLA_RANK), (0, 0))),
        ])
        gbias = gla_a_bias[l].reshape(2, GLA_HEADS, 1, GLA_DK)
        gnorm = gla_norm[l].reshape(1, GLA_DK)
        og_c, gs_c = _gla(p_gla, up_pad, gbias, gnorm, None, l, SEQ, BATCH, 0)
        og_l, _ = _gla(p_gla, up_pad, gbias, gnorm, state_gla, l, DEC_SEQ, DEC_BATCH, N_CTX)
        o_gla = jnp.concatenate([og_c, og_l], axis=0)
        gla_states.append(gs_c)

        row = lambda a: a.reshape(1, -1)
        mu = _rwkv_cols(row(rwkv_mu[l]))
        wuf = _bf(_pad_rows(rwkv_w_up[l, 0], LANE))
        wub = _bf(_pad_rows(rwkv_w_up[l, 1], LANE))
        au = _bf(_pad_rows(rwkv_a_up[l], LANE))
        gu = _bf(rwkv_g_up[l])
        r_, k_, v_, kap_, beta_, lw_, g_ = _rwkv_prep(
            p_rw, mu, wuf, wub, au, gu, rwkv_w0[l], row(rwkv_a0[l]), row(rwkv_k_xi[l]),
            row(rwkv_k_alpha[l]), bd)
        y_c, rs_c = _rwkv_scan(r_, k_, v_, kap_, beta_, lw_, None, l, SEQ, BATCH, 0)
        y_l, _ = _rwkv_scan(r_, k_, v_, kap_, beta_, lw_, state_rwkv, l, DEC_SEQ, DEC_BATCH, N_CTX)
        y = jnp.concatenate([y_c, y_l], axis=1)
        o_rw = _rwkv_post(y, r_, k_, v_, g_, row(rwkv_bonus[l]), row(rwkv_ln_w[l]), row(rwkv_ln_b[l]), bd)
        rwkv_states.append(rs_c)

        x = _merge(o_att, o_gla, o_rw, gates, x, mod, _bf(w_br_att[l]), _bf(w_br_gla[l]),
                   _bf(w_br_rwkv[l]), _bf(w_out[l]))

        h2 = _prenorm(x, norm_ffn[l], mod, 3)
        fu = ffn_up[l]
        w_up = _bf(jnp.concatenate([_pad_cols(fu[:, :FFN_HIDDEN], FFN_PAD),
                                    _pad_cols(fu[:, FFN_HIDDEN:], FFN_PAD)], axis=1))
        cw = ffn_conv_w[l]
        conv_w = jnp.concatenate([_pad_cols(cw[:, :FFN_HIDDEN], FFN_PAD),
                                  _pad_cols(cw[:, FFN_HIDDEN:], FFN_PAD)], axis=1)
        cb = row(ffn_conv_b[l])
        conv_b = jnp.concatenate([_pad_cols(cb[:, :FFN_HIDDEN], FFN_PAD),
                                  _pad_cols(cb[:, FFN_HIDDEN:], FFN_PAD)], axis=1)
        w_down = _bf(_pad_rows(ffn_down[l], FFN_PAD))
        x = _ffn(h2, x, mod, w_up, conv_w, conv_b, w_down)

    y_prompt = x[:N_CTX].reshape(BATCH, SEQ, D_MODEL)
    y_sample = x[N_CTX:].reshape(DEC_BATCH, DEC_SEQ, D_MODEL)
    new_cache_k = jnp.stack(ks, axis=1)
    new_cache_v = jnp.stack(vs, axis=1)
    new_state_gla = jnp.stack(gla_states, axis=1)
    new_state_rwkv = jnp.stack(rwkv_states, axis=1)
    return (y_prompt, y_sample, new_cache_k, new_cache_v, new_state_gla, new_state_rwkv)
```

```python
import functools

import jax
import jax.numpy as jnp
from jax import lax
from jax.experimental import pallas as pl
from jax.experimental.pallas import tpu as pltpu

F32 = jnp.float32
BF16 = jnp.bfloat16

D_MODEL = 2048
BATCH = 16
SEQ = 256
DEPTH = 2
DEC_BATCH = 4
DEC_SEQ = 1024
PAST_LEN = 256
GRID_W = 64
HEAD_DIM = 128
ATT_HEADS = 8
ATT_KV_HEADS = 2
ROPE_THETA = 10000.0
GLA_HEADS = 4
GLA_DK = 128
GLA_RANK = 16
GLA_TAU = 16.0
RWKV_HEADS = 8
RWKV_HD = 64
RWKV_W_RANK = 64
RWKV_A_RANK = 64
RWKV_G_RANK = 128
RWKV_DECAY_SCALE = 0.606531
RWKV_LN_EPS = 64e-5
FFN_HIDDEN = 5504
EPS = 1e-6

ATT_Q_W = ATT_HEADS * HEAD_DIM
ATT_KV_W = ATT_KV_HEADS * HEAD_DIM
GLA_W = GLA_HEADS * GLA_DK
RWKV_W = RWKV_HEADS * RWKV_HD
RWKV_COLS = 3 * RWKV_W + 2 * RWKV_W_RANK + RWKV_A_RANK + RWKV_G_RANK
IN_COLS = ATT_Q_W + 2 * ATT_KV_W + 4 * GLA_W + 2 * GLA_RANK + RWKV_COLS + 3 * D_MODEL

LANE = 128
CHUNK = 64
N_CTX = BATCH * SEQ
N_LAT = DEC_BATCH * DEC_SEQ
N_TOK = N_CTX + N_LAT
N_MODROWS = 8
VMEM_LIMIT = 56 * 1024 * 1024

COL_GLA = ATT_Q_W + 2 * ATT_KV_W
COL_GAD = COL_GLA + 4 * GLA_W
COL_RW = COL_GAD + 2 * GLA_RANK
COL_GATE = COL_RW + RWKV_COLS
MAIN_W = 4096
RW_W = 2048
GATE_TN = 768
GATE_W = 3 * D_MODEL

_NT = (((1,), (1,)), ((), ()))
_TN = (((0,), (0,)), ((), ()))
_NN = (((1,), (0,)), ((), ()))


def _bf(x):
    return x.astype(BF16)


def _dg(a, b, dims=_NN):
    return lax.dot_general(a, b, dims, preferred_element_type=F32)


def _mm(a, b, dims=_NN):
    return _dg(_bf(a), _bf(b), dims)


def _split3(x):
    h = x.astype(BF16)
    r = x - h.astype(F32)
    m = r.astype(BF16)
    return h, m, (r - m.astype(F32)).astype(BF16)


def _mm_exact_rhs(a, b_bf, dims=_NN):
    h, m, l = _split3(a)
    return _dg(h, b_bf, dims) + (_dg(m, b_bf, dims) + _dg(l, b_bf, dims))


def _sigmoid(x):
    return 0.5 * jnp.tanh(0.5 * x) + 0.5


def _rms(x, g):
    return x * lax.rsqrt(jnp.mean(x * x, axis=-1, keepdims=True) + EPS) * g


def _cparams(sem):
    return pltpu.CompilerParams(dimension_semantics=sem, vmem_limit_bytes=VMEM_LIMIT)


def _mod_row(i, tm):
    n_ctx = N_CTX // tm
    return jnp.where(i < n_ctx, 0, 1 + (i - n_ctx) // (DEC_SEQ // tm))


def _ctx_lat_specs(tm, width, grid_rank=1):
    na = N_CTX // tm
    if grid_rank == 1:
        return (pl.BlockSpec((tm, width), lambda i: (jnp.minimum(i, na - 1), 0)),
                pl.BlockSpec((tm, width), lambda i: (jnp.maximum(i - na, 0), 0)))
    return (pl.BlockSpec((tm, width), lambda n, i: (jnp.minimum(i, na - 1), n)),
            pl.BlockSpec((tm, width), lambda n, i: (jnp.maximum(i - na, 0), n)))


def _is_ctx(i, tm):
    return i < N_CTX // tm


def _seq_edges(g, rows):
    lat = g - N_CTX
    start = (g < N_CTX) | ((lat & (DEC_SEQ - 1)) == 0)
    end = (g < N_CTX) | (((lat + rows) & (DEC_SEQ - 1)) == 0)
    if rows < SEQ:
        start = jnp.where(g < N_CTX, (g & (SEQ - 1)) == 0, start)
        end = jnp.where(g < N_CTX, ((g + rows) & (SEQ - 1)) == 0, end)
    return start, end


def _shift_rows(x, before, after):
    n = x.shape[0]
    row = lax.broadcasted_iota(jnp.int32, (n, 1), 0)
    prev = jnp.where(row == 0, before, pltpu.roll(x, 1, 0))
    nxt = jnp.where(row == n - 1, after, pltpu.roll(x, n - 1, 0))
    return prev, nxt


def _mod_kernel(c_ref, w_ref, b_ref, o_ref):
    c = c_ref[...]
    o_ref[...] = _mm(c * _sigmoid(c), w_ref[...]) + b_ref[...]


def _modulation(cvec, w_mod, b_mod):
    tn = 1024
    n = 6 * D_MODEL
    return pl.pallas_call(
        _mod_kernel,
        grid=(DEPTH, n // tn),
        in_specs=[
            pl.BlockSpec((N_MODROWS, D_MODEL), lambda l, j: (0, 0)),
            pl.BlockSpec((None, D_MODEL, tn), lambda l, j: (l, 0, j)),
            pl.BlockSpec((None, 1, tn), lambda l, j: (l, 0, j)),
        ],
        out_specs=pl.BlockSpec((None, N_MODROWS, tn), lambda l, j: (l, 0, j)),
        out_shape=jax.ShapeDtypeStruct((DEPTH, N_MODROWS, n), F32),
        compiler_params=_cparams(("arbitrary", "arbitrary")),
        name="modulation",
    )(cvec, w_mod, b_mod.reshape(DEPTH, 1, n))


def _prenorm_kernel(xa_ref, xb_ref, g_ref, mod_ref, o_ref, *, shift_idx):
    tm = o_ref.shape[0]
    x = jnp.where(_is_ctx(pl.program_id(0), tm), xa_ref[...], xb_ref[...])
    y = _rms(x, g_ref[...])
    sh = mod_ref[shift_idx:shift_idx + 1, :]
    sc = mod_ref[shift_idx + 1:shift_idx + 2, :]
    o_ref[...] = (y * (1.0 + sc) + sh).astype(BF16)


def _prenorm(x, g, mod, shift_idx):
    tm = 256
    xa, xb = _ctx_lat_specs(tm, D_MODEL)
    return pl.pallas_call(
        functools.partial(_prenorm_kernel, shift_idx=shift_idx),
        grid=(N_TOK // tm,),
        in_specs=[
            xa, xb,
            pl.BlockSpec((1, D_MODEL), lambda i: (0, 0)),
            pl.BlockSpec((None, 6, D_MODEL), lambda i: (_mod_row(i, tm), 0, 0)),
        ],
        out_specs=pl.BlockSpec((tm, D_MODEL), lambda i: (i, 0)),
        out_shape=jax.ShapeDtypeStruct((N_TOK, D_MODEL), BF16),
        compiler_params=_cparams(("arbitrary",)),
        name="prenorm",
    )(x[0], x[1], g.reshape(1, D_MODEL), mod)


def _proj_kernel(*refs, n_w, lane_shift):
    h_ref = refs[0]
    w_refs = refs[1:1 + n_w]
    o_ref, w_scr = refs[1 + n_w:]

    @pl.when(pl.program_id(1) == 0)
    def _():
        w = w_refs[0][...] if n_w == 1 else jnp.concatenate([r[...] for r in w_refs], axis=1)
        w_scr[...] = w[:, lane_shift:lane_shift + w_scr.shape[1]].astype(BF16)

    o_ref[...] = _dg(h_ref[...], w_scr[...]).astype(o_ref.dtype)


def _proj_main(h, w_in, layer):
    tm, tn = 1024, 512
    return pl.pallas_call(
        functools.partial(_proj_kernel, n_w=1, lane_shift=0),
        grid=(MAIN_W // tn, N_TOK // tm),
        in_specs=[
            pl.BlockSpec((tm, D_MODEL), lambda j, i: (i, 0)),
            pl.BlockSpec((None, D_MODEL, tn), lambda j, i: (layer, 0, j)),
        ],
        out_specs=pl.BlockSpec((tm, tn), lambda j, i: (i, j)),
        out_shape=jax.ShapeDtypeStruct((N_TOK, MAIN_W), F32),
        scratch_shapes=[pltpu.VMEM((D_MODEL, tn), BF16)],
        compiler_params=_cparams(("arbitrary", "arbitrary")),
        name="proj_main",
    )(h, w_in)


def _proj_shifted(h, w_in, layer, col0, width, tn, name, dtype=F32):
    tm = 1024
    base = col0 - col0 % LANE
    assert base % tn == 0 and width % tn == 0
    per = tn // LANE
    last_blk = (IN_COLS - 1) // LANE
    return pl.pallas_call(
        functools.partial(_proj_kernel, n_w=2, lane_shift=col0 - base),
        grid=(width // tn, N_TOK // tm),
        in_specs=[
            pl.BlockSpec((tm, D_MODEL), lambda j, i: (i, 0)),
            pl.BlockSpec((None, D_MODEL, tn), lambda j, i: (layer, 0, base // tn + j)),
            pl.BlockSpec((None, D_MODEL, LANE),
                         lambda j, i: (layer, 0, jnp.minimum(base // LANE + per * (j + 1), last_blk))),
        ],
        out_specs=pl.BlockSpec((tm, tn), lambda j, i: (i, j)),
        out_shape=jax.ShapeDtypeStruct((N_TOK, width), dtype),
        scratch_shapes=[pltpu.VMEM((D_MODEL, tn), BF16)],
        compiler_params=_cparams(("arbitrary", "arbitrary")),
        name=name,
    )(h, w_in, w_in)


_ATT_SCALE = HEAD_DIM ** -0.5
_Q_PER_KV = ATT_HEADS // ATT_KV_HEADS


def _rope(x, cos, sin_signed):
    lane = lax.broadcasted_iota(jnp.int32, x.shape, 1)
    partner = jnp.where((lane & 63) < 32, pltpu.roll(x, 96, 1), pltpu.roll(x, 32, 1))
    return x * cos + partner * sin_signed


def _softmax_pv(s, v_ones):
    p = _bf(jnp.exp(s - jnp.max(s, axis=-1, keepdims=True)))
    oa = _dg(p, v_ones)
    return oa[:, :HEAD_DIM] / oa[:, HEAD_DIM:]


def _attn_ctx_kernel(q_ref, k_ref, v_ref, qn_ref, kn_ref, o_ref, ko_ref, vo_ref):
    k = _rms(k_ref[...], kn_ref[...])
    v = v_ref[...]
    ko_ref[...] = k
    vo_ref[...] = v
    kb = _bf(k)
    v_ones = jnp.concatenate([_bf(v), jnp.ones(v.shape, BF16)], axis=1)
    heads = range(_Q_PER_KV)
    qs = [_bf(_rms(q_ref[:, h * HEAD_DIM:(h + 1) * HEAD_DIM], qn_ref[...])) for h in heads]
    ss = [_dg(qs[h], kb, _NT) * _ATT_SCALE for h in heads]
    o_ref[...] = jnp.concatenate([_softmax_pv(s, v_ones) for s in ss], axis=1).astype(BF16)


def _attn_lat_kernel(q_ref, k_ref, v_ref, qn_ref, kn_ref, cos_ref, sin_ref, ck_ref, cv_ref, o_ref):
    cos = cos_ref[...]
    sin = sin_ref[...]
    k = _rope(_rms(k_ref[...], kn_ref[...]), cos, sin)
    kb = _bf(jnp.concatenate([ck_ref[...], k], axis=0))
    vb = _bf(jnp.concatenate([cv_ref[...], v_ref[...]], axis=0))
    v_ones = jnp.concatenate([vb, jnp.ones(vb.shape, BF16)], axis=1)
    q = _bf(_rope(_rms(q_ref[...], qn_ref[...]), cos, sin))
    rb = 256
    blocks = range(DEC_SEQ // rb)
    ss = [_dg(q[r * rb:(r + 1) * rb], kb, _NT) * _ATT_SCALE for r in blocks]
    o_ref[...] = jnp.concatenate([_softmax_pv(s, v_ones) for s in ss], axis=0).astype(BF16)


def _attention_ctx(proj, qn, kn):
    blk = lambda w, f: pl.BlockSpec((SEQ, w), f)
    vec = pl.BlockSpec((1, HEAD_DIM), lambda b, g: (0, 0))
    kv_out = pl.BlockSpec((None, None, SEQ, HEAD_DIM), lambda b, g: (b, g, 0, 0))
    qw = _Q_PER_KV * HEAD_DIM
    return pl.pallas_call(
        _attn_ctx_kernel,
        grid=(BATCH, ATT_KV_HEADS),
        in_specs=[
            blk(qw, lambda b, g: (b, g)),
            blk(HEAD_DIM, lambda b, g: (b, ATT_HEADS + g)),
            blk(HEAD_DIM, lambda b, g: (b, ATT_HEADS + ATT_KV_HEADS + g)),
            vec, vec,
        ],
        out_specs=[blk(qw, lambda b, g: (b, g)), kv_out, kv_out],
        out_shape=[
            jax.ShapeDtypeStruct((N_CTX, ATT_Q_W), BF16),
            jax.ShapeDtypeStruct((BATCH, ATT_KV_HEADS, SEQ, HEAD_DIM), F32),
            jax.ShapeDtypeStruct((BATCH, ATT_KV_HEADS, SEQ, HEAD_DIM), F32),
        ],
        compiler_params=_cparams(("arbitrary", "arbitrary")),
        name="attention_ctx",
    )(proj, proj, proj, qn, kn)


def _attention_lat(proj, qn, kn, cos, sin, cache_k, cache_v, layer):
    rb = N_CTX // DEC_SEQ
    blk = lambda f: pl.BlockSpec((DEC_SEQ, HEAD_DIM), f)
    vec = pl.BlockSpec((1, HEAD_DIM), lambda b, g, i: (0, 0))
    tab = pl.BlockSpec((DEC_SEQ, HEAD_DIM), lambda b, g, i: (0, 0))
    cache = pl.BlockSpec((None, None, None, PAST_LEN, HEAD_DIM), lambda b, g, i: (b, layer, g, 0, 0))
    return pl.pallas_call(
        _attn_lat_kernel,
        grid=(DEC_BATCH, ATT_KV_HEADS, _Q_PER_KV),
        in_specs=[
            blk(lambda b, g, i: (rb + b, g * _Q_PER_KV + i)),
            blk(lambda b, g, i: (rb + b, ATT_HEADS + g)),
            blk(lambda b, g, i: (rb + b, ATT_HEADS + ATT_KV_HEADS + g)),
            vec, vec, tab, tab, cache, cache,
        ],
        out_specs=blk(lambda b, g, i: (b, g * _Q_PER_KV + i)),
        out_shape=jax.ShapeDtypeStruct((N_LAT, ATT_Q_W), BF16),
        compiler_params=_cparams(("arbitrary", "arbitrary", "arbitrary")),
        name="attention_lat",
    )(proj, proj, proj, qn, kn, cos, sin, cache_k, cache_v)


def _tri(n, upper):
    r = lax.broadcasted_iota(jnp.int32, (n, n), 0)
    c = lax.broadcasted_iota(jnp.int32, (n, n), 1)
    return (r <= c) if upper else (r >= c)


def _gla_kernel(*refs, seq, has_state):
    if has_state:
        q_ref, k_ref, v_ref, gg_ref, gad_ref, up_ref, bias_ref, gn_ref, s0_ref, o_ref, st_ref = refs
    else:
        q_ref, k_ref, v_ref, gg_ref, gad_ref, up_ref, bias_ref, gn_ref, o_ref, st_ref = refs
        s0_ref = None
    nc = seq // CHUNK
    c3 = (nc, CHUNK, LANE)
    gad = _bf(gad_ref[...])
    q3 = (q_ref[...] * (GLA_DK ** -0.5)).reshape(c3)
    k3 = k_ref[...].reshape(c3)
    v3 = _bf(v_ref[...]).reshape(c3)
    bdot = lambda a, b, ca, cb: lax.dot_general(a, b, (((ca,), (cb,)), ((0,), (0,))), preferred_element_type=F32)

    qe, oi, kv, dec = [], [], [], []
    for d in range(2):
        incl = _tri(CHUNK, d == 1)
        tri = jnp.broadcast_to(jnp.where(incl, 1.0, 0.0).astype(BF16)[None], (nc, CHUNK, CHUNK))
        la = jax.nn.log_sigmoid(_dg(gad, _bf(up_ref[d])) + bias_ref[d]) / GLA_TAU
        hi, mid, lo = _split3(la.reshape(c3))
        cum = bdot(tri, hi, 2, 1) + (bdot(tri, mid, 2, 1) + bdot(tri, lo, 2, 1))
        tot = cum[:, 0:1, :] if d else cum[:, CHUNK - 1:CHUNK, :]
        qe_d = _bf(q3 * jnp.exp(cum))
        ke = _bf(k3 * jnp.exp(-cum))
        kl = _bf(k3 * jnp.exp(tot - cum))
        att = jnp.where(incl[None], bdot(qe_d, ke, 2, 2), 0.0)
        qe.append(qe_d)
        oi.append(bdot(_bf(att), v3, 2, 1))
        kv.append(bdot(v3, kl, 1, 1))
        dec.append(jnp.exp(tot))

    st = [s0_ref[d].T if has_state else jnp.zeros((GLA_DK, GLA_DK), F32) for d in range(2)]
    o_f, o_b = [None] * nc, [None] * nc
    for t in range(nc):
        nf, nb = t, nc - 1 - t
        o_f[nf] = oi[0][nf] + _dg(qe[0][nf], _bf(st[0]), _NT)
        o_b[nb] = oi[1][nb] + _dg(qe[1][nb], _bf(st[1]), _NT)
        st[0] = st[0] * dec[0][nf] + kv[0][nf]
        st[1] = st[1] * dec[1][nb] + kv[1][nb]
    st_ref[0] = st[0].T
    st_ref[1] = st[1].T
    o = jnp.concatenate([a + b for a, b in zip(o_f, o_b)], axis=0)
    g = gg_ref[...]
    o_ref[...] = (_rms(o, gn_ref[...]) * (g * _sigmoid(g))).astype(o_ref.dtype)


def _gla(proj, up_pad, bias, gnorm, s0, layer, seq, nb, row0):
    rb = row0 // seq
    c0 = COL_GLA // LANE
    blk = lambda c: pl.BlockSpec((seq, LANE), lambda b, h: (rb + b, c0 + c * GLA_HEADS + h))
    has_state = s0 is not None
    in_specs = [
        blk(0), blk(1), blk(2), blk(3),
        pl.BlockSpec((seq, LANE), lambda b, h: (rb + b, COL_GAD // LANE)),
        pl.BlockSpec((2, None, LANE, LANE), lambda b, h: (0, h, 0, 0)),
        pl.BlockSpec((2, None, 1, LANE), lambda b, h: (0, h, 0, 0)),
        pl.BlockSpec((1, LANE), lambda b, h: (0, 0)),
    ]
    args = [proj, proj, proj, proj, proj, up_pad, bias, gnorm]
    if has_state:
        in_specs.append(pl.BlockSpec((None, None, 2, None, GLA_DK, GLA_DK),
                                     lambda b, h: (b, layer, 0, h, 0, 0)))
        args.append(s0)
    return pl.pallas_call(
        functools.partial(_gla_kernel, seq=seq, has_state=has_state),
        grid=(nb, GLA_HEADS),
        in_specs=in_specs,
        out_specs=[
            pl.BlockSpec((seq, LANE), lambda b, h: (b, h)),
            pl.BlockSpec((None, 2, None, GLA_DK, GLA_DK), lambda b, h: (b, 0, h, 0, 0)),
        ],
        out_shape=[
            jax.ShapeDtypeStruct((nb * seq, GLA_W), BF16),
            jax.ShapeDtypeStruct((nb, 2, GLA_HEADS, GLA_DK, GLA_DK), F32),
        ],
        compiler_params=_cparams(("arbitrary", "arbitrary")),
        name="gla_lat" if has_state else "gla_ctx",
    )(*args)


def _segsum(x, bd_ref):
    return _mm_exact_rhs(x, bd_ref[...])


def _cumsum(tri_bf, x):
    h, m, l = _split3(x)
    return _dg(tri_bf, h) + (_dg(tri_bf, m) + _dg(tri_bf, l))


def _rwkv_prep_kernel(x_ref, xp_ref, xn_ref, mu_ref, wuf_ref, wub_ref, au_ref,
                      gu_ref, w0_ref, a0_ref, xi_ref, al_ref, bd_ref,
                      r_ref, k_ref, v_ref, kap_ref, beta_ref, lw_ref, g_ref):
    tm = x_ref.shape[0]
    start, end = _seq_edges(pl.program_id(0) * tm, tm)
    x = x_ref[...]
    before = jnp.where(start, 0.0, xp_ref[7:8, :])
    after = jnp.where(end, 0.0, xn_ref[0:1, :])
    prev, nxt = _shift_rows(x, before, after)
    rw = x + (0.5 * (prev + nxt) - x) * mu_ref[...]
    w = RWKV_W
    rr = rw[:, 0:w]
    rk = rw[:, w:2 * w]
    rv = rw[:, 2 * w:3 * w]
    rwd = _bf(jnp.tanh(rw[:, 3 * w:3 * w + LANE]))
    rad = _bf(rw[:, 3 * w + LANE:3 * w + 2 * LANE])
    rgd = _bf(_sigmoid(rw[:, 3 * w + LANE:3 * w + 3 * LANE]))
    lw_ref[0] = -RWKV_DECAY_SCALE * _sigmoid(w0_ref[0:1, :] + _dg(rwd, wuf_ref[...]))
    lw_ref[1] = -RWKV_DECAY_SCALE * _sigmoid(w0_ref[1:2, :] + _dg(rwd, wub_ref[...]))
    a = _sigmoid(a0_ref[...] + _dg(rad, au_ref[...]))
    g_ref[...] = _dg(rgd, gu_ref[...])
    kap = rk * xi_ref[...]
    kap = kap * lax.rsqrt(_segsum(kap * kap, bd_ref) + EPS)
    r_ref[...] = rr
    k_ref[...] = rk * (1.0 + (a - 1.0) * al_ref[...])
    v_ref[...] = rv
    kap_ref[...] = kap
    beta_ref[...] = kap * a


def _rwkv_prep(proj, mu, wuf, wub, au, gu, w0, a0, xi, al, bd):
    tm = 256
    nt = N_TOK // tm
    hb = tm // 8
    full = lambda shape: pl.BlockSpec(shape, lambda i: (0,) * len(shape))
    tok = pl.BlockSpec((tm, RWKV_W), lambda i: (i, 0))
    tshape = jax.ShapeDtypeStruct((N_TOK, RWKV_W), F32)
    return pl.pallas_call(
        _rwkv_prep_kernel,
        grid=(nt,),
        in_specs=[
            pl.BlockSpec((tm, RW_W), lambda i: (i, 0)),
            pl.BlockSpec((8, RW_W), lambda i: (jnp.maximum(i * hb - 1, 0), 0)),
            pl.BlockSpec((8, RW_W), lambda i: (jnp.minimum((i + 1) * hb, N_TOK // 8 - 1), 0)),
            full((1, RW_W)),
            full((LANE, RWKV_W)), full((LANE, RWKV_W)), full((LANE, RWKV_W)), full((2 * LANE, RWKV_W)),
            full((2, RWKV_W)), full((1, RWKV_W)), full((1, RWKV_W)), full((1, RWKV_W)),
            full((RWKV_W, RWKV_W)),
        ],
        out_specs=[tok, tok, tok, tok, tok,
                   pl.BlockSpec((2, tm, RWKV_W), lambda i: (0, i, 0)), tok],
        out_shape=[tshape, tshape, tshape, tshape, tshape,
                   jax.ShapeDtypeStruct((2, N_TOK, RWKV_W), F32), tshape],
        compiler_params=_cparams(("arbitrary",)),
        name="rwkv_prep",
    )(proj, proj, proj, mu, wuf, wub, au, gu, w0, a0, xi, al, bd)


def _unit_tri_inverse_corr(lows):
    c = [-low for low in lows]
    pb = [_bf(low) for low in lows]
    p = [_dg(b, b) for b in pb]
    span = 2
    while span < CHUNK:
        pb = [_bf(x) for x in p]
        c = [a + x + _dg(_bf(a), xb) for a, x, xb in zip(c, p, pb)]
        span *= 2
        if span < CHUNK:
            p = [_dg(xb, xb) for xb in pb]
    return c


def _rwkv_scan_kernel(*refs, seq, has_state):
    if has_state:
        r_ref, k_ref, v_ref, kap_ref, beta_ref, lw_ref, s0_ref, y_ref, st_ref, s_scr = refs
    else:
        r_ref, k_ref, v_ref, kap_ref, beta_ref, lw_ref, y_ref, st_ref, s_scr = refs
    n_chunks = seq // CHUNK
    c64 = CHUNK
    rr = lax.broadcasted_iota(jnp.int32, (c64, c64), 0)
    cc = lax.broadcasted_iota(jnp.int32, (c64, c64), 1)
    incl = (rr >= cc, rr <= cc)
    strict = (rr > cc, rr < cc)
    tri = tuple(jnp.where(m, 1.0, 0.0).astype(BF16) for m in incl)
    if has_state:
        s_scr[...] = s0_ref[...]
    else:
        s_scr[...] = jnp.zeros_like(s_scr)
    heads = range(RWKV_HEADS)
    cut = lambda x: [x[:, h * RWKV_HD:(h + 1) * RWKV_HD] for h in heads]

    def body(i, carry):
        lhs, rhs, upd, v_h, et_h, s, ms, mi, rows_d = [], [], [], [], [], [], [], [], []
        for d in range(2):
            n = (n_chunks - 1 - i) if d else i
            rows = pl.ds(pl.multiple_of(n * c64, c64), c64)
            lw = lw_ref[d, rows, :]
            cum = _cumsum(tri[d], lw)
            tot = cum[0:1, :] if d else cum[c64 - 1:c64, :]
            e_neg = jnp.exp(-cum)
            e_tot = jnp.exp(tot)
            kt = kap_ref[rows, :] * jnp.exp(cum - lw)
            bt = beta_ref[rows, :] * e_neg
            kk = k_ref[rows, :] * e_neg
            rt = r_ref[rows, :] * jnp.exp(cum)
            lhs += cut(_bf(jnp.concatenate([kt, rt], axis=0)))
            rhs += cut(_bf(jnp.concatenate([bt, kk], axis=0)))
            upd += cut(_bf(jnp.concatenate([kk * e_tot, -(bt * e_tot)], axis=0)))
            v_h += cut(v_ref[rows, :])
            et_h += cut(e_tot)
            s += [s_scr[d, h] for h in heads]
            ms += [strict[d]] * RWKV_HEADS
            mi += [incl[d]] * RWKV_HEADS
            rows_d.append(rows)
        chains = range(2 * RWKV_HEADS)
        a1 = [_dg(lhs[c], jnp.concatenate([rhs[c], _bf(s[c])], axis=0), _NT) for c in chains]
        low = [jnp.where(ms[c], a1[c][:c64, 0:c64], 0.0) for c in chains]
        g1 = [jnp.where(ms[c], a1[c][:c64, c64:2 * c64], 0.0) for c in chains]
        corr = _unit_tri_inverse_corr(low)
        z = [a1[c][:c64, 2 * c64:] + _mm(g1[c], v_h[c]) for c in chains]
        u = [z[c] + _mm(corr[c], z[c]) for c in chains]
        vu = [_bf(jnp.concatenate([v_h[c], u[c]], axis=0)) for c in chains]
        g23 = [_bf(jnp.concatenate([jnp.where(mi[c], a1[c][c64:, c64:2 * c64], 0.0),
                                    jnp.where(mi[c], -a1[c][c64:, 0:c64], 0.0)], axis=1)) for c in chains]
        y = [a1[c][c64:, 2 * c64:] + _dg(g23[c], vu[c]) for c in chains]
        for c in chains:
            s_scr[c // RWKV_HEADS, c % RWKV_HEADS] = s[c] * et_h[c] + _dg(vu[c], upd[c], _TN)
        for d in range(2):
            y_ref[d, rows_d[d], :] = jnp.concatenate(y[d * RWKV_HEADS:(d + 1) * RWKV_HEADS], axis=1)
        return carry

    lax.fori_loop(0, n_chunks, body, 0)
    st_ref[...] = s_scr[...]


def _rwkv_scan(r, k, v, kap, beta, lw, s0, layer, seq, nb, row0):
    rb = row0 // seq
    has_state = s0 is not None
    tok = pl.BlockSpec((seq, RWKV_W), lambda b: (rb + b, 0))
    in_specs = [tok, tok, tok, tok, tok,
                pl.BlockSpec((2, seq, RWKV_W), lambda b: (0, rb + b, 0))]
    args = [r, k, v, kap, beta, lw]
    st_shape = (2, RWKV_HEADS, RWKV_HD, RWKV_HD)
    if has_state:
        in_specs.append(pl.BlockSpec((None, None) + st_shape, lambda b: (b, layer, 0, 0, 0, 0)))
        args.append(s0)
    return pl.pallas_call(
        functools.partial(_rwkv_scan_kernel, seq=seq, has_state=has_state),
        grid=(nb,),
        in_specs=in_specs,
        out_specs=[
            pl.BlockSpec((2, seq, RWKV_W), lambda b: (0, b, 0)),
            pl.BlockSpec((None,) + st_shape, lambda b: (b, 0, 0, 0, 0)),
        ],
        out_shape=[
            jax.ShapeDtypeStruct((2, nb * seq, RWKV_W), F32),
            jax.ShapeDtypeStruct((nb,) + st_shape, F32),
        ],
        scratch_shapes=[pltpu.VMEM(st_shape, F32)],
        compiler_params=_cparams(("arbitrary",)),
        name="rwkv_scan_lat" if has_state else "rwkv_scan_ctx",
    )(*args)


def _rwkv_post_kernel(ya_ref, yb_ref, r_ref, k_ref, v_ref, g_ref, rho_ref, lnw_ref, lnb_ref, bd_ref, o_ref):
    tm = o_ref.shape[0]
    ys = jnp.where(_is_ctx(pl.program_id(0), tm), ya_ref[...], yb_ref[...])
    v = v_ref[...]
    bonus = _segsum(r_ref[...] * k_ref[...] * rho_ref[...], bd_ref) * v
    y = ys[0] + ys[1] + bonus
    inv_n = 1.0 / RWKV_HD
    yc = y - _segsum(y, bd_ref) * inv_n
    var = _segsum(yc * yc, bd_ref) * inv_n
    yn = yc * lax.rsqrt(var + RWKV_LN_EPS)
    o_ref[...] = ((yn * lnw_ref[...] + lnb_ref[...]) * g_ref[...]).astype(BF16)


def _rwkv_post(y_ctx, y_lat, r, k, v, g, rho, lnw, lnb, bd):
    tm = 256
    na = N_CTX // tm
    tok = pl.BlockSpec((tm, RWKV_W), lambda i: (i, 0))
    vec = pl.BlockSpec((1, RWKV_W), lambda i: (0, 0))
    return pl.pallas_call(
        _rwkv_post_kernel,
        grid=(N_TOK // tm,),
        in_specs=[pl.BlockSpec((2, tm, RWKV_W), lambda i: (0, jnp.minimum(i, na - 1), 0)),
                  pl.BlockSpec((2, tm, RWKV_W), lambda i: (0, jnp.maximum(i - na, 0), 0)),
                  tok, tok, tok, tok, vec, vec, vec,
                  pl.BlockSpec((RWKV_W, RWKV_W), lambda i: (0, 0))],
        out_specs=tok,
        out_shape=jax.ShapeDtypeStruct((N_TOK, RWKV_W), BF16),
        compiler_params=_cparams(("arbitrary",)),
        name="rwkv_post",
    )(y_ctx, y_lat, r, k, v, g, rho, lnw, lnb, bd)


def _merge_kernel(oaa_ref, oab_ref, oga_ref, ogb_ref, orw_ref, gate_ref, xa_ref, xb_ref, mod_ref,
                  wa_ref, wg_ref, wr_ref, wo_ref, ya_ref, yb_ref):
    d = D_MODEL
    tm = orw_ref.shape[0]
    ctx = _is_ctx(pl.program_id(0), tm)
    pick = lambda a, b: jnp.where(ctx, a[...], b[...])
    gate = lambda k: _sigmoid(gate_ref[:, k * d:(k + 1) * d].astype(F32))
    merged = gate(0) * _dg(pick(oaa_ref, oab_ref), wa_ref[...])
    merged += gate(1) * _dg(pick(oga_ref, ogb_ref), wg_ref[...])
    merged += gate(2) * _dg(orw_ref[...], wr_ref[...])
    y = pick(xa_ref, xb_ref) + mod_ref[2:3, :] * _dg(_bf(merged), wo_ref[...])

    @pl.when(ctx)
    def _():
        ya_ref[...] = y

    @pl.when(jnp.logical_not(ctx))
    def _():
        yb_ref[...] = y


def _merge(o_att, o_gla, o_rwkv, gates, x, mod, wa, wg, wr, wo, layer):
    tm = 256
    const = lambda r, c: pl.BlockSpec((None, r, c), lambda i: (layer, 0, 0), pipeline_mode=pl.Buffered(1))
    xa, xb = _ctx_lat_specs(tm, D_MODEL)
    return pl.pallas_call(
        _merge_kernel,
        grid=(N_TOK // tm,),
        in_specs=[
            *_ctx_lat_specs(tm, ATT_Q_W),
            *_ctx_lat_specs(tm, GLA_W),
            pl.BlockSpec((tm, RWKV_W), lambda i: (i, 0)),
            pl.BlockSpec((tm, GATE_W), lambda i: (i, 0)),
            xa, xb,
            pl.BlockSpec((None, 6, D_MODEL), lambda i: (_mod_row(i, tm), 0, 0)),
            const(ATT_Q_W, D_MODEL), const(GLA_W, D_MODEL), const(RWKV_W, D_MODEL), const(D_MODEL, D_MODEL),
        ],
        out_specs=[xa, xb],
        out_shape=[jax.ShapeDtypeStruct((N_CTX, D_MODEL), F32), jax.ShapeDtypeStruct((N_LAT, D_MODEL), F32)],
        compiler_params=_cparams(("arbitrary",)),
        name="merge_out",
    )(o_att[0], o_att[1], o_gla[0], o_gla[1], o_rwkv, gates, x[0], x[1], mod, wa, wg, wr, wo)


_FFN_TN = 512
_FFN_SUB = 256
_FFN_RB = 256


def _ffn_up_kernel(h_ref, hp_ref, hn_ref, wv_ref, wg0_ref, wg1_ref, wg2_ref, wg3_ref,
                   cwv_ref, cwg_ref, cbv_ref, cbg_ref, o_ref, wv_scr, wg_scr):
    tm = h_ref.shape[0]
    i = pl.program_id(1)

    @pl.when(i == 0)
    def _():
        wv_scr[...] = wv_ref[...].astype(BF16)
        wg_scr[...] = jnp.concatenate([r[...] for r in (wg0_ref, wg1_ref, wg2_ref, wg3_ref)], axis=1).astype(BF16)

    h = h_ref[...]
    hp = hp_ref[...]
    hn = hn_ref[...]
    n_sub = _FFN_TN // _FFN_SUB
    n_rb = tm // _FFN_RB
    edges = [_seq_edges(i * tm + r * _FFN_RB, _FFN_RB) for r in range(n_rb)]

    def up(w_scr, s):
        w = w_scr[:, s * _FFN_SUB:(s + 1) * _FFN_SUB]
        return _dg(h, w), _dg(hp, w)[7:8, :], _dg(hn, w)[0:1, :]

    def conv(us, cw_ref, cb_ref, s, r):
        u_all, u_before, u_after = us
        cols = slice(s * _FFN_SUB, (s + 1) * _FFN_SUB)
        lo, hi = r * _FFN_RB, (r + 1) * _FFN_RB
        u = u_all[lo:hi]
        start, end = edges[r]
        before = jnp.where(start, 0.0, u_before if r == 0 else u_all[lo - 1:lo])
        after = jnp.where(end, 0.0, u_after if r == n_rb - 1 else u_all[hi:hi + 1])
        prev, nxt = _shift_rows(u, before, after)
        return cw_ref[0:1, cols] * prev + cw_ref[1:2, cols] * u + cw_ref[2:3, cols] * nxt + cb_ref[:, cols]

    ups = [(up(wv_scr, s), up(wg_scr, s)) for s in range(n_sub)]
    for s in range(n_sub):
        for r in range(n_rb):
            val = conv(ups[s][0], cwv_ref, cbv_ref, s, r)
            gate = conv(ups[s][1], cwg_ref, cbg_ref, s, r)
            o_ref[r * _FFN_RB:(r + 1) * _FFN_RB, s * _FFN_SUB:(s + 1) * _FFN_SUB] = _bf(gate * _sigmoid(gate) * val)


def _ffn_down_kernel(a_ref, w_ref, xa_ref, xb_ref, mod_ref, ya_ref, yb_ref):
    tm = a_ref.shape[0]
    ctx = _is_ctx(pl.program_id(1), tm)
    y = jnp.where(ctx, xa_ref[...], xb_ref[...]) + mod_ref[5:6, :] * _dg(a_ref[...], w_ref[...])

    @pl.when(ctx)
    def _():
        ya_ref[...] = y

    @pl.when(jnp.logical_not(ctx))
    def _():
        yb_ref[...] = y


def _ffn(h, x, mod, ffn_up, conv_wv, conv_wg, conv_bv, conv_bg, w_down, layer):
    tm, tn = 1024, _FFN_TN
    nj = pl.cdiv(FFN_HIDDEN, tn)
    hb = tm // 8
    gate0 = FFN_HIDDEN // LANE
    last_blk = 2 * FFN_HIDDEN // LANE - 1
    gate_blk = lambda q: pl.BlockSpec(
        (None, D_MODEL, LANE),
        lambda j, i: (layer, 0, jnp.minimum(gate0 + (tn // LANE) * j + q, last_blk)))
    act = pl.pallas_call(
        _ffn_up_kernel,
        grid=(nj, N_TOK // tm),
        in_specs=[
            pl.BlockSpec((tm, D_MODEL), lambda j, i: (i, 0)),
            pl.BlockSpec((8, D_MODEL), lambda j, i: (jnp.maximum(i * hb - 1, 0), 0)),
            pl.BlockSpec((8, D_MODEL), lambda j, i: (jnp.minimum((i + 1) * hb, N_TOK // 8 - 1), 0)),
            pl.BlockSpec((None, D_MODEL, tn), lambda j, i: (layer, 0, j)),
            gate_blk(0), gate_blk(1), gate_blk(2), gate_blk(3),
            pl.BlockSpec((3, tn), lambda j, i: (0, j)),
            pl.BlockSpec((3, tn), lambda j, i: (0, j)),
            pl.BlockSpec((1, tn), lambda j, i: (0, j)),
            pl.BlockSpec((1, tn), lambda j, i: (0, j)),
        ],
        out_specs=pl.BlockSpec((tm, tn), lambda j, i: (i, j)),
        out_shape=jax.ShapeDtypeStruct((N_TOK, FFN_HIDDEN), BF16),
        scratch_shapes=[pltpu.VMEM((D_MODEL, tn), BF16), pltpu.VMEM((D_MODEL, tn), BF16)],
        compiler_params=_cparams(("arbitrary", "arbitrary")),
        name="ffn_up",
    )(h, h, h, ffn_up, ffn_up, ffn_up, ffn_up, ffn_up, conv_wv, conv_wg, conv_bv, conv_bg)
    to = 512
    xa, xb = _ctx_lat_specs(tm, to, grid_rank=2)
    return pl.pallas_call(
        _ffn_down_kernel,
        grid=(D_MODEL // to, N_TOK // tm),
        in_specs=[
            pl.BlockSpec((tm, FFN_HIDDEN), lambda n, i: (i, 0)),
            pl.BlockSpec((None, FFN_HIDDEN, to), lambda n, i: (layer, 0, n)),
            xa, xb,
            pl.BlockSpec((None, 6, to), lambda n, i: (_mod_row(i, tm), 0, n)),
        ],
        out_specs=[xa, xb],
        out_shape=[jax.ShapeDtypeStruct((N_CTX, D_MODEL), F32), jax.ShapeDtypeStruct((N_LAT, D_MODEL), F32)],
        compiler_params=_cparams(("arbitrary", "arbitrary")),
        name="ffn_down",
    )(act, w_down, x[0], x[1], mod)


def _pad_cols(x, n):
    return jnp.pad(x, ((0, 0), (0, n - x.shape[1])))


def _pad_rows(x, n, at=0):
    return jnp.pad(x, ((at, n - x.shape[0] - at), (0, 0)))


def _rope_tables(t):
    rows = t // GRID_W
    row = jnp.repeat(jnp.arange(rows, dtype=F32), GRID_W)
    col = jnp.tile(jnp.arange(GRID_W, dtype=F32), rows)
    half = HEAD_DIM // 2
    inv = ROPE_THETA ** (-jnp.arange(0, half, 2, dtype=F32) / half)

    def cos_sin(pos):
        ang = pos[:, None] * inv[None, :]
        ang = jnp.concatenate([ang, ang], axis=-1)
        return jnp.cos(ang), jnp.sin(ang)

    cos_r, sin_r = cos_sin(row)
    cos_c, sin_c = cos_sin(col)
    cos = jnp.concatenate([cos_r, cos_c], axis=-1)
    sin = jnp.concatenate([sin_r, sin_c], axis=-1)
    sign = jnp.where((jnp.arange(HEAD_DIM) % half) < half // 2, -1.0, 1.0).astype(F32)
    return cos, sin * sign[None, :]


def kernel(x_prompt, x_sample, cache_k, cache_v, state_gla, state_rwkv, c, c_ctx, w_mod, b_mod, norm_mix, w_in, q_norm, k_norm, gla_a_up, gla_a_bias, gla_norm, rwkv_mu, rwkv_w0, rwkv_w_up, rwkv_a0, rwkv_a_up, rwkv_g_up, rwkv_k_xi, rwkv_k_alpha, rwkv_bonus, rwkv_ln_w, rwkv_ln_b, w_br_att, w_br_gla, w_br_rwkv, w_out, norm_ffn, ffn_up, ffn_conv_w, ffn_conv_b, ffn_down):
    x = (x_prompt.reshape(N_CTX, D_MODEL), x_sample.reshape(N_LAT, D_MODEL))
    cvec = jnp.concatenate([c_ctx[None, :], c, jnp.zeros((N_MODROWS - 1 - DEC_BATCH, D_MODEL), F32)], axis=0)
    mod_all = _modulation(cvec, w_mod, b_mod).reshape(DEPTH, N_MODROWS, 6, D_MODEL)
    cos, sin = _rope_tables(DEC_SEQ)
    hr = lax.broadcasted_iota(jnp.int32, (RWKV_W, RWKV_W), 0) // RWKV_HD
    hc = lax.broadcasted_iota(jnp.int32, (RWKV_W, RWKV_W), 1) // RWKV_HD
    bd = (hr == hc).astype(BF16)
    wa_all, wg_all, wr_all, wo_all, wd_all = (_bf(w) for w in (w_br_att, w_br_gla, w_br_rwkv, w_out, ffn_down))
    ffn_pad = pl.cdiv(FFN_HIDDEN, _FFN_TN) * _FFN_TN

    ks, vs, gla_states, rwkv_states = [], [], [], []
    for l in range(DEPTH):
        mod = mod_all[l]
        h = _prenorm(x, norm_mix[l], mod, 0)
        proj = _proj_main(h, w_in, l)
        p_rw = _proj_shifted(h, w_in, l, COL_RW, RW_W, 512, "proj_rwkv")
        gates = _proj_shifted(h, w_in, l, COL_GATE, GATE_W, GATE_TN, "proj_gates", BF16)

        qn = q_norm[l].reshape(1, HEAD_DIM)
        kn = k_norm[l].reshape(1, HEAD_DIM)
        oa_c, k_c, v_c = _attention_ctx(proj, qn, kn)
        oa_l = _attention_lat(proj, qn, kn, cos, sin, cache_k, cache_v, l)
        ks.append(k_c)
        vs.append(v_c)

        up = gla_a_up[l].reshape(2, GLA_RANK, GLA_HEADS, GLA_DK).transpose(0, 2, 1, 3)
        up_pad = jnp.stack([
            jnp.pad(up[0], ((0, 0), (0, LANE - GLA_RANK), (0, 0))),
            jnp.pad(up[1], ((0, 0), (GLA_RANK, LANE - 2 * GLA_RANK), (0, 0))),
        ])
        gbias = gla_a_bias[l].reshape(2, GLA_HEADS, 1, GLA_DK)
        gnorm = gla_norm[l].reshape(1, GLA_DK)
        og_c, gs_c = _gla(proj, up_pad, gbias, gnorm, None, l, SEQ, BATCH, 0)
        og_l, _ = _gla(proj, up_pad, gbias, gnorm, state_gla, l, DEC_SEQ, DEC_BATCH, N_CTX)
        gla_states.append(gs_c)

        row = lambda a: a.reshape(1, -1)
        mu = _pad_cols(row(rwkv_mu[l]), RW_W)
        wuf = _bf(_pad_rows(rwkv_w_up[l, 0], LANE))
        wub = _bf(_pad_rows(rwkv_w_up[l, 1], LANE, at=RWKV_W_RANK))
        au = _bf(_pad_rows(rwkv_a_up[l], LANE))
        gu = _bf(_pad_rows(rwkv_g_up[l], 2 * LANE, at=RWKV_A_RANK))
        r_, k_, v_, kap_, beta_, lw_, g_ = _rwkv_prep(
            p_rw, mu, wuf, wub, au, gu, rwkv_w0[l], row(rwkv_a0[l]), row(rwkv_k_xi[l]),
            row(rwkv_k_alpha[l]), bd)
        y_c, rs_c = _rwkv_scan(r_, k_, v_, kap_, beta_, lw_, None, l, SEQ, BATCH, 0)
        y_l, _ = _rwkv_scan(r_, k_, v_, kap_, beta_, lw_, state_rwkv, l, DEC_SEQ, DEC_BATCH, N_CTX)
        o_rw = _rwkv_post(y_c, y_l, r_, k_, v_, g_, row(rwkv_bonus[l]), row(rwkv_ln_w[l]), row(rwkv_ln_b[l]), bd)
        rwkv_states.append(rs_c)

        x = _merge((oa_c, oa_l), (og_c, og_l), o_rw, gates, x, mod, wa_all, wg_all, wr_all, wo_all, l)

        h2 = _prenorm(x, norm_ffn[l], mod, 3)
        cw = ffn_conv_w[l]
        cb = row(ffn_conv_b[l])
        x = _ffn(h2, x, mod, ffn_up,
                 _pad_cols(cw[:, :FFN_HIDDEN], ffn_pad), _pad_cols(cw[:, FFN_HIDDEN:], ffn_pad),
                 _pad_cols(cb[:, :FFN_HIDDEN], ffn_pad), _pad_cols(cb[:, FFN_HIDDEN:], ffn_pad),
                 wd_all, l)

    y_prompt = x[0].reshape(BATCH, SEQ, D_MODEL)
    y_sample = x[1].reshape(DEC_BATCH, DEC_SEQ, D_MODEL)
    new_cache_k = jnp.stack(ks, axis=1)
    new_cache_v = jnp.stack(vs, axis=1)
    new_state_gla = jnp.stack(gla_states, axis=1)
    new_state_rwkv = jnp.stack(rwkv_states, axis=1)
    return (y_prompt, y_sample, new_cache_k, new_cache_v, new_state_gla, new_state_rwkv)
```

```python
import functools

import jax
import jax.numpy as jnp
from jax import lax
from jax.experimental import pallas as pl
from jax.experimental.pallas import tpu as pltpu

F32 = jnp.float32
BF16 = jnp.bfloat16

D_MODEL = 2048
BATCH = 16
SEQ = 256
DEPTH = 2
DEC_BATCH = 4
DEC_SEQ = 1024
PAST_LEN = 256
GRID_W = 64
HEAD_DIM = 128
ATT_HEADS = 8
ATT_KV_HEADS = 2
ROPE_THETA = 10000.0
GLA_HEADS = 4
GLA_DK = 128
GLA_RANK = 16
GLA_TAU = 16.0
RWKV_HEADS = 8
RWKV_HD = 64
RWKV_W_RANK = 64
RWKV_A_RANK = 64
RWKV_G_RANK = 128
RWKV_DECAY_SCALE = 0.606531
RWKV_LN_EPS = 64e-5
FFN_HIDDEN = 5504
EPS = 1e-6

ATT_Q_W = ATT_HEADS * HEAD_DIM
ATT_KV_W = ATT_KV_HEADS * HEAD_DIM
GLA_W = GLA_HEADS * GLA_DK
RWKV_W = RWKV_HEADS * RWKV_HD
RWKV_COLS = 3 * RWKV_W + 2 * RWKV_W_RANK + RWKV_A_RANK + RWKV_G_RANK
IN_COLS = ATT_Q_W + 2 * ATT_KV_W + 4 * GLA_W + 2 * GLA_RANK + RWKV_COLS + 3 * D_MODEL

LANE = 128
CHUNK = 64
N_CTX = BATCH * SEQ
N_LAT = DEC_BATCH * DEC_SEQ
N_TOK = N_CTX + N_LAT
N_MODROWS = 8
VMEM_LIMIT = 56 * 1024 * 1024

COL_GLA = ATT_Q_W + 2 * ATT_KV_W
COL_GAD = COL_GLA + 4 * GLA_W
COL_RW = COL_GAD + 2 * GLA_RANK
COL_GATE = COL_RW + RWKV_COLS
MAIN_W = 4096
RW_W = 2048
GATE_TN = 768
GATE_W = 3 * D_MODEL

_NT = (((1,), (1,)), ((), ()))
_TN = (((0,), (0,)), ((), ()))
_NN = (((1,), (0,)), ((), ()))


def _bf(x):
    return x.astype(BF16)


def _dg(a, b, dims=_NN):
    return lax.dot_general(a, b, dims, preferred_element_type=F32)


def _mm(a, b, dims=_NN):
    return _dg(_bf(a), _bf(b), dims)


def _split3(x):
    h = x.astype(BF16)
    r = x - h.astype(F32)
    m = r.astype(BF16)
    return h, m, (r - m.astype(F32)).astype(BF16)


def _mm_exact_rhs(a, b_bf, dims=_NN):
    h, m, l = _split3(a)
    return _dg(h, b_bf, dims) + (_dg(m, b_bf, dims) + _dg(l, b_bf, dims))


def _sigmoid(x):
    return 0.5 * jnp.tanh(0.5 * x) + 0.5


def _rms(x, g):
    return x * lax.rsqrt(jnp.mean(x * x, axis=-1, keepdims=True) + EPS) * g


def _cparams(sem):
    return pltpu.CompilerParams(dimension_semantics=sem, vmem_limit_bytes=VMEM_LIMIT)


def _mod_row(i, tm):
    n_ctx = N_CTX // tm
    return jnp.where(i < n_ctx, 0, 1 + (i - n_ctx) // (DEC_SEQ // tm))


def _ctx_lat_specs(tm, width, grid_rank=1):
    na = N_CTX // tm
    if grid_rank == 1:
        return (pl.BlockSpec((tm, width), lambda i: (jnp.minimum(i, na - 1), 0)),
                pl.BlockSpec((tm, width), lambda i: (jnp.maximum(i - na, 0), 0)))
    return (pl.BlockSpec((tm, width), lambda n, i: (jnp.minimum(i, na - 1), n)),
            pl.BlockSpec((tm, width), lambda n, i: (jnp.maximum(i - na, 0), n)))


def _is_ctx(i, tm):
    return i < N_CTX // tm


def _seq_edges(g, rows):
    lat = g - N_CTX
    start = (g < N_CTX) | ((lat & (DEC_SEQ - 1)) == 0)
    end = (g < N_CTX) | (((lat + rows) & (DEC_SEQ - 1)) == 0)
    if rows < SEQ:
        start = jnp.where(g < N_CTX, (g & (SEQ - 1)) == 0, start)
        end = jnp.where(g < N_CTX, ((g + rows) & (SEQ - 1)) == 0, end)
    return start, end


def _shift_rows(x, before, after):
    n = x.shape[0]
    row = lax.broadcasted_iota(jnp.int32, (n, 1), 0)
    prev = jnp.where(row == 0, before, pltpu.roll(x, 1, 0))
    nxt = jnp.where(row == n - 1, after, pltpu.roll(x, n - 1, 0))
    return prev, nxt


def _mod_kernel(c_ref, w_ref, b_ref, o_ref):
    c = c_ref[...]
    o_ref[...] = _mm(c * _sigmoid(c), w_ref[...]) + b_ref[...]


def _modulation(cvec, w_mod, b_mod):
    tn = 1024
    n = 6 * D_MODEL
    return pl.pallas_call(
        _mod_kernel,
        grid=(DEPTH, n // tn),
        in_specs=[
            pl.BlockSpec((N_MODROWS, D_MODEL), lambda l, j: (0, 0)),
            pl.BlockSpec((None, D_MODEL, tn), lambda l, j: (l, 0, j)),
            pl.BlockSpec((None, 1, tn), lambda l, j: (l, 0, j)),
        ],
        out_specs=pl.BlockSpec((None, N_MODROWS, tn), lambda l, j: (l, 0, j)),
        out_shape=jax.ShapeDtypeStruct((DEPTH, N_MODROWS, n), F32),
        compiler_params=_cparams(("arbitrary", "arbitrary")),
        name="modulation",
    )(cvec, w_mod, b_mod.reshape(DEPTH, 1, n))


def _prenorm_kernel(xa_ref, xb_ref, g_ref, mod_ref, o_ref, *, shift_idx):
    tm = o_ref.shape[0]
    x = jnp.where(_is_ctx(pl.program_id(0), tm), xa_ref[...], xb_ref[...])
    y = _rms(x, g_ref[...])
    sh = mod_ref[shift_idx:shift_idx + 1, :]
    sc = mod_ref[shift_idx + 1:shift_idx + 2, :]
    o_ref[...] = (y * (1.0 + sc) + sh).astype(BF16)


def _prenorm(x, g, mod, shift_idx):
    tm = 512
    xa, xb = _ctx_lat_specs(tm, D_MODEL)
    return pl.pallas_call(
        functools.partial(_prenorm_kernel, shift_idx=shift_idx),
        grid=(N_TOK // tm,),
        in_specs=[
            xa, xb,
            pl.BlockSpec((1, D_MODEL), lambda i: (0, 0)),
            pl.BlockSpec((None, 6, D_MODEL), lambda i: (_mod_row(i, tm), 0, 0)),
        ],
        out_specs=pl.BlockSpec((tm, D_MODEL), lambda i: (i, 0)),
        out_shape=jax.ShapeDtypeStruct((N_TOK, D_MODEL), BF16),
        compiler_params=_cparams(("arbitrary",)),
        name="prenorm",
    )(x[0], x[1], g.reshape(1, D_MODEL), mod)


def _proj_kernel(*refs, n_w, row_shift):
    h_ref = refs[0]
    w_refs = refs[1:1 + n_w]
    o_ref, w_scr = refs[1 + n_w:]

    @pl.when(pl.program_id(1) == 0)
    def _():
        w = w_refs[0][...] if n_w == 1 else jnp.concatenate([r[...] for r in w_refs], axis=0)
        w_scr[...] = w[row_shift:row_shift + w_scr.shape[0], :].astype(BF16)

    o_ref[...] = _dg(h_ref[...], w_scr[...], _NT).astype(o_ref.dtype)


def _proj(h, w_in_t, layer, col0, width, tn, name, dtype=F32):
    tm = 1024
    off = col0 % tn
    base = col0 - off
    assert width % tn == 0 and off % 8 == 0 and (off == 0 or (tn % off == 0 and IN_COLS % off == 0))
    in_specs = [
        pl.BlockSpec((tm, D_MODEL), lambda j, i: (i, 0)),
        pl.BlockSpec((None, tn, D_MODEL), lambda j, i: (layer, base // tn + j, 0)),
    ]
    if off:
        in_specs.append(pl.BlockSpec((None, off, D_MODEL), lambda j, i: (layer, (base + tn * (j + 1)) // off, 0)))
    return pl.pallas_call(
        functools.partial(_proj_kernel, n_w=len(in_specs) - 1, row_shift=off),
        grid=(width // tn, N_TOK // tm),
        in_specs=in_specs,
        out_specs=pl.BlockSpec((tm, tn), lambda j, i: (i, j)),
        out_shape=jax.ShapeDtypeStruct((N_TOK, width), dtype),
        scratch_shapes=[pltpu.VMEM((tn, D_MODEL), BF16)],
        compiler_params=_cparams(("arbitrary", "arbitrary")),
        name=name,
    )(h, *([w_in_t] * (len(in_specs) - 1)))


_ATT_SCALE = HEAD_DIM ** -0.5
_Q_PER_KV = ATT_HEADS // ATT_KV_HEADS


def _rope(x, cos, sin_signed):
    lane = lax.broadcasted_iota(jnp.int32, x.shape, 1)
    partner = jnp.where((lane & 63) < 32, pltpu.roll(x, 96, 1), pltpu.roll(x, 32, 1))
    return x * cos + partner * sin_signed


def _softmax_pv(s, v_ones):
    p = _bf(jnp.exp(s - jnp.max(s, axis=-1, keepdims=True)))
    oa = _dg(p, v_ones)
    return oa[:, :HEAD_DIM] / oa[:, HEAD_DIM:]


def _attn_ctx_kernel(q_ref, k_ref, v_ref, qn_ref, kn_ref, o_ref, ko_ref, vo_ref):
    k = _rms(k_ref[...], kn_ref[...])
    v = v_ref[...]
    ko_ref[...] = k
    vo_ref[...] = v
    kb = _bf(k)
    v_ones = jnp.concatenate([_bf(v), jnp.ones(v.shape, BF16)], axis=1)
    heads = range(_Q_PER_KV)
    qs = [_bf(_rms(q_ref[:, h * HEAD_DIM:(h + 1) * HEAD_DIM], qn_ref[...])) for h in heads]
    ss = [_dg(qs[h], kb, _NT) * _ATT_SCALE for h in heads]
    o_ref[...] = jnp.concatenate([_softmax_pv(s, v_ones) for s in ss], axis=1).astype(BF16)


def _attn_lat_kernel(q_ref, k_ref, v_ref, qn_ref, kn_ref, cos_ref, sin_ref, ck_ref, cv_ref, o_ref):
    cos = cos_ref[...]
    sin = sin_ref[...]
    k = _rope(_rms(k_ref[...], kn_ref[...]), cos, sin)
    kb = _bf(jnp.concatenate([ck_ref[...], k], axis=0))
    vb = _bf(jnp.concatenate([cv_ref[...], v_ref[...]], axis=0))
    v_ones = jnp.concatenate([vb, jnp.ones(vb.shape, BF16)], axis=1)
    q = _bf(_rope(_rms(q_ref[...], qn_ref[...]), cos, sin))
    rb = 256
    blocks = range(DEC_SEQ // rb)
    ss = [_dg(q[r * rb:(r + 1) * rb], kb, _NT) * _ATT_SCALE for r in blocks]
    o_ref[...] = jnp.concatenate([_softmax_pv(s, v_ones) for s in ss], axis=0).astype(BF16)


def _attention_ctx(proj, qn, kn):
    blk = lambda w, f: pl.BlockSpec((SEQ, w), f)
    vec = pl.BlockSpec((1, HEAD_DIM), lambda b, g: (0, 0))
    kv_out = pl.BlockSpec((None, None, SEQ, HEAD_DIM), lambda b, g: (b, g, 0, 0))
    qw = _Q_PER_KV * HEAD_DIM
    return pl.pallas_call(
        _attn_ctx_kernel,
        grid=(BATCH, ATT_KV_HEADS),
        in_specs=[
            blk(qw, lambda b, g: (b, g)),
            blk(HEAD_DIM, lambda b, g: (b, ATT_HEADS + g)),
            blk(HEAD_DIM, lambda b, g: (b, ATT_HEADS + ATT_KV_HEADS + g)),
            vec, vec,
        ],
        out_specs=[blk(qw, lambda b, g: (b, g)), kv_out, kv_out],
        out_shape=[
            jax.ShapeDtypeStruct((N_CTX, ATT_Q_W), BF16),
            jax.ShapeDtypeStruct((BATCH, ATT_KV_HEADS, SEQ, HEAD_DIM), F32),
            jax.ShapeDtypeStruct((BATCH, ATT_KV_HEADS, SEQ, HEAD_DIM), F32),
        ],
        compiler_params=_cparams(("arbitrary", "arbitrary")),
        name="attention_ctx",
    )(proj, proj, proj, qn, kn)


def _attention_lat(proj, qn, kn, cos, sin, cache_k, cache_v, layer):
    rb = N_CTX // DEC_SEQ
    blk = lambda f: pl.BlockSpec((DEC_SEQ, HEAD_DIM), f)
    vec = pl.BlockSpec((1, HEAD_DIM), lambda b, g, i: (0, 0))
    tab = pl.BlockSpec((DEC_SEQ, HEAD_DIM), lambda b, g, i: (0, 0))
    cache = pl.BlockSpec((None, None, None, PAST_LEN, HEAD_DIM), lambda b, g, i: (b, layer, g, 0, 0))
    return pl.pallas_call(
        _attn_lat_kernel,
        grid=(DEC_BATCH, ATT_KV_HEADS, _Q_PER_KV),
        in_specs=[
            blk(lambda b, g, i: (rb + b, g * _Q_PER_KV + i)),
            blk(lambda b, g, i: (rb + b, ATT_HEADS + g)),
            blk(lambda b, g, i: (rb + b, ATT_HEADS + ATT_KV_HEADS + g)),
            vec, vec, tab, tab, cache, cache,
        ],
        out_specs=blk(lambda b, g, i: (b, g * _Q_PER_KV + i)),
        out_shape=jax.ShapeDtypeStruct((N_LAT, ATT_Q_W), BF16),
        compiler_params=_cparams(("arbitrary", "arbitrary", "arbitrary")),
        name="attention_lat",
    )(proj, proj, proj, qn, kn, cos, sin, cache_k, cache_v)


def _tri(n, upper):
    r = lax.broadcasted_iota(jnp.int32, (n, n), 0)
    c = lax.broadcasted_iota(jnp.int32, (n, n), 1)
    return (r <= c) if upper else (r >= c)


def _gla_kernel(*refs, seq, has_state):
    if has_state:
        q_ref, k_ref, v_ref, gg_ref, gad_ref, up_ref, bias_ref, gn_ref, s0_ref, o_ref, st_ref = refs
    else:
        q_ref, k_ref, v_ref, gg_ref, gad_ref, up_ref, bias_ref, gn_ref, o_ref, st_ref = refs
        s0_ref = None
    nc = seq // CHUNK
    nh = q_ref.shape[1] // LANE
    c3 = (nc, CHUNK, LANE)
    stack = lambda x: jnp.concatenate([x[:, h * LANE:(h + 1) * LANE].reshape(c3) for h in range(nh)], axis=0)
    gad = _bf(gad_ref[...])
    q3 = stack(q_ref[...] * (GLA_DK ** -0.5))
    k3 = stack(k_ref[...])
    v3 = stack(_bf(v_ref[...]))
    bdot = lambda a, b, ca, cb: lax.dot_general(a, b, (((ca,), (cb,)), ((0,), (0,))), preferred_element_type=F32)

    qe, oi, kv, dec = [], [], [], []
    for d in range(2):
        incl = _tri(CHUNK, d == 1)
        tri = jnp.broadcast_to(jnp.where(incl, 1.0, 0.0).astype(BF16)[None], (nh * nc, CHUNK, CHUNK))
        la = jnp.concatenate(
            [(jax.nn.log_sigmoid(_dg(gad, _bf(up_ref[d, h])) + bias_ref[d, h]) / GLA_TAU).reshape(c3)
             for h in range(nh)], axis=0)
        hi, mid, lo = _split3(la)
        cum = bdot(tri, hi, 2, 1) + (bdot(tri, mid, 2, 1) + bdot(tri, lo, 2, 1))
        tot = cum[:, 0:1, :] if d else cum[:, CHUNK - 1:CHUNK, :]
        qe_d = _bf(q3 * jnp.exp(cum))
        ke = _bf(k3 * jnp.exp(-cum))
        kl = _bf(k3 * jnp.exp(tot - cum))
        att = jnp.where(incl[None], bdot(qe_d, ke, 2, 2), 0.0)
        qe.append(qe_d)
        oi.append(bdot(_bf(att), v3, 2, 1))
        kv.append(bdot(v3, kl, 1, 1))
        dec.append(jnp.exp(tot))

    zero = jnp.zeros((GLA_DK, GLA_DK), F32)
    st = [[s0_ref[d, h].T if has_state else zero for h in range(nh)] for d in range(2)]
    o_f = [[None] * nc for _ in range(nh)]
    o_b = [[None] * nc for _ in range(nh)]
    for t in range(nc):
        nf, nb = t, nc - 1 - t
        for h in range(nh):
            o_f[h][nf] = oi[0][h * nc + nf] + _dg(qe[0][h * nc + nf], _bf(st[0][h]), _NT)
            o_b[h][nb] = oi[1][h * nc + nb] + _dg(qe[1][h * nc + nb], _bf(st[1][h]), _NT)
        for h in range(nh):
            st[0][h] = st[0][h] * dec[0][h * nc + nf] + kv[0][h * nc + nf]
            st[1][h] = st[1][h] * dec[1][h * nc + nb] + kv[1][h * nc + nb]
    outs = []
    for h in range(nh):
        st_ref[0, h] = st[0][h].T
        st_ref[1, h] = st[1][h].T
        outs.append(_rms(jnp.concatenate([a + b for a, b in zip(o_f[h], o_b[h])], axis=0), gn_ref[...]))
    g = gg_ref[...]
    o_ref[...] = (jnp.concatenate(outs, axis=1) * (g * _sigmoid(g))).astype(o_ref.dtype)


def _gla(proj, up_pad, bias, gnorm, s0, layer, seq, nb, row0, nh):
    rb = row0 // seq
    w = nh * LANE
    c0 = COL_GLA // w
    per = GLA_HEADS // nh
    blk = lambda c: pl.BlockSpec((seq, w), lambda b, h: (rb + b, c0 + c * per + h))
    has_state = s0 is not None
    in_specs = [
        blk(0), blk(1), blk(2), blk(3),
        pl.BlockSpec((seq, LANE), lambda b, h: (rb + b, COL_GAD // LANE)),
        pl.BlockSpec((2, nh, LANE, LANE), lambda b, h: (0, h, 0, 0)),
        pl.BlockSpec((2, nh, 1, LANE), lambda b, h: (0, h, 0, 0)),
        pl.BlockSpec((1, LANE), lambda b, h: (0, 0)),
    ]
    args = [proj, proj, proj, proj, proj, up_pad, bias, gnorm]
    if has_state:
        in_specs.append(pl.BlockSpec((None, None, 2, nh, GLA_DK, GLA_DK),
                                     lambda b, h: (b, layer, 0, h, 0, 0)))
        args.append(s0)
    return pl.pallas_call(
        functools.partial(_gla_kernel, seq=seq, has_state=has_state),
        grid=(nb, per),
        in_specs=in_specs,
        out_specs=[
            pl.BlockSpec((seq, w), lambda b, h: (b, h)),
            pl.BlockSpec((None, 2, nh, GLA_DK, GLA_DK), lambda b, h: (b, 0, h, 0, 0)),
        ],
        out_shape=[
            jax.ShapeDtypeStruct((nb * seq, GLA_W), BF16),
            jax.ShapeDtypeStruct((nb, 2, GLA_HEADS, GLA_DK, GLA_DK), F32),
        ],
        compiler_params=_cparams(("arbitrary", "arbitrary")),
        name="gla_lat" if has_state else "gla_ctx",
    )(*args)


def _segsum(x, bd_ref):
    return _mm_exact_rhs(x, bd_ref[...])


def _cumsum(tri_bf, x):
    h, m, l = _split3(x)
    return _dg(tri_bf, h) + (_dg(tri_bf, m) + _dg(tri_bf, l))


def _rwkv_prep_kernel(x_ref, xp_ref, xn_ref, mu_ref, wuf_ref, wub_ref, au_ref,
                      gu_ref, w0_ref, a0_ref, xi_ref, al_ref, bd_ref,
                      r_ref, k_ref, v_ref, kap_ref, beta_ref, lw_ref, g_ref):
    tm = x_ref.shape[0]
    start, end = _seq_edges(pl.program_id(0) * tm, tm)
    x = x_ref[...]
    before = jnp.where(start, 0.0, xp_ref[7:8, :])
    after = jnp.where(end, 0.0, xn_ref[0:1, :])
    prev, nxt = _shift_rows(x, before, after)
    rw = x + (0.5 * (prev + nxt) - x) * mu_ref[...]
    w = RWKV_W
    rr = rw[:, 0:w]
    rk = rw[:, w:2 * w]
    rv = rw[:, 2 * w:3 * w]
    rwd = _bf(jnp.tanh(rw[:, 3 * w:3 * w + LANE]))
    rad = _bf(rw[:, 3 * w + LANE:3 * w + 2 * LANE])
    rgd = _bf(_sigmoid(rw[:, 3 * w + LANE:3 * w + 3 * LANE]))
    lw_ref[0] = -RWKV_DECAY_SCALE * _sigmoid(w0_ref[0:1, :] + _dg(rwd, wuf_ref[...]))
    lw_ref[1] = -RWKV_DECAY_SCALE * _sigmoid(w0_ref[1:2, :] + _dg(rwd, wub_ref[...]))
    a = _sigmoid(a0_ref[...] + _dg(rad, au_ref[...]))
    g_ref[...] = _dg(rgd, gu_ref[...])
    kap = rk * xi_ref[...]
    kap = kap * lax.rsqrt(_segsum(kap * kap, bd_ref) + EPS)
    r_ref[...] = rr
    k_ref[...] = rk * (1.0 + (a - 1.0) * al_ref[...])
    v_ref[...] = rv
    kap_ref[...] = kap
    beta_ref[...] = kap * a


def _rwkv_prep(proj, mu, wuf, wub, au, gu, w0, a0, xi, al, bd):
    tm = 256
    nt = N_TOK // tm
    hb = tm // 8
    full = lambda shape: pl.BlockSpec(shape, lambda i: (0,) * len(shape))
    tok = pl.BlockSpec((tm, RWKV_W), lambda i: (i, 0))
    tshape = jax.ShapeDtypeStruct((N_TOK, RWKV_W), F32)
    return pl.pallas_call(
        _rwkv_prep_kernel,
        grid=(nt,),
        in_specs=[
            pl.BlockSpec((tm, RW_W), lambda i: (i, 0)),
            pl.BlockSpec((8, RW_W), lambda i: (jnp.maximum(i * hb - 1, 0), 0)),
            pl.BlockSpec((8, RW_W), lambda i: (jnp.minimum((i + 1) * hb, N_TOK // 8 - 1), 0)),
            full((1, RW_W)),
            full((LANE, RWKV_W)), full((LANE, RWKV_W)), full((LANE, RWKV_W)), full((2 * LANE, RWKV_W)),
            full((2, RWKV_W)), full((1, RWKV_W)), full((1, RWKV_W)), full((1, RWKV_W)),
            full((RWKV_W, RWKV_W)),
        ],
        out_specs=[tok, tok, tok, tok, tok,
                   pl.BlockSpec((2, tm, RWKV_W), lambda i: (0, i, 0)), tok],
        out_shape=[tshape, tshape, tshape, tshape, tshape,
                   jax.ShapeDtypeStruct((2, N_TOK, RWKV_W), F32), tshape],
        compiler_params=_cparams(("arbitrary",)),
        name="rwkv_prep",
    )(proj, proj, proj, mu, wuf, wub, au, gu, w0, a0, xi, al, bd)


def _unit_tri_inverse_corr(lows):
    c = [-low for low in lows]
    pb = [_bf(low) for low in lows]
    p = [_dg(b, b) for b in pb]
    span = 2
    while span < CHUNK:
        pb = [_bf(x) for x in p]
        c = [a + x + _dg(_bf(a), xb) for a, x, xb in zip(c, p, pb)]
        span *= 2
        if span < CHUNK:
            p = [_dg(xb, xb) for xb in pb]
    return c


def _rwkv_scan_kernel(*refs, seq, nseq, has_state):
    if has_state:
        r_ref, k_ref, v_ref, kap_ref, beta_ref, lw_ref, s0_ref, y_ref, st_ref, s_scr = refs
    else:
        r_ref, k_ref, v_ref, kap_ref, beta_ref, lw_ref, y_ref, st_ref, s_scr = refs
    n_chunks = seq // CHUNK
    c64 = CHUNK
    rr = lax.broadcasted_iota(jnp.int32, (c64, c64), 0)
    cc = lax.broadcasted_iota(jnp.int32, (c64, c64), 1)
    incl = (rr >= cc, rr <= cc)
    strict = (rr > cc, rr < cc)
    tri = tuple(jnp.where(m, 1.0, 0.0).astype(BF16) for m in incl)
    if has_state:
        s_scr[...] = s0_ref[...]
    else:
        s_scr[...] = jnp.zeros_like(s_scr)
    heads = range(RWKV_HEADS)
    cut = lambda x: [x[:, h * RWKV_HD:(h + 1) * RWKV_HD] for h in heads]
    groups = [(q, d) for q in range(nseq) for d in range(2)]

    def body(i, carry):
        lhs, rhs, upd, v_h, et_h, s, ms, mi, rows_d = [], [], [], [], [], [], [], [], []
        for q, d in groups:
            n = (n_chunks - 1 - i) if d else i
            rows = pl.ds(pl.multiple_of(q * seq + n * c64, c64), c64)
            lw = lw_ref[d, rows, :]
            cum = _cumsum(tri[d], lw)
            tot = cum[0:1, :] if d else cum[c64 - 1:c64, :]
            e_neg = jnp.exp(-cum)
            e_tot = jnp.exp(tot)
            kt = kap_ref[rows, :] * jnp.exp(cum - lw)
            bt = beta_ref[rows, :] * e_neg
            kk = k_ref[rows, :] * e_neg
            rt = r_ref[rows, :] * jnp.exp(cum)
            lhs += cut(_bf(jnp.concatenate([kt, rt], axis=0)))
            rhs += cut(_bf(jnp.concatenate([bt, kk], axis=0)))
            upd += cut(_bf(jnp.concatenate([kk * e_tot, -(bt * e_tot)], axis=0)))
            v_h += cut(v_ref[rows, :])
            et_h += cut(e_tot)
            s += [s_scr[q, d, h] for h in heads]
            ms += [strict[d]] * RWKV_HEADS
            mi += [incl[d]] * RWKV_HEADS
            rows_d.append(rows)
        chains = range(len(groups) * RWKV_HEADS)
        a1 = [_dg(lhs[c], jnp.concatenate([rhs[c], _bf(s[c])], axis=0), _NT) for c in chains]
        low = [jnp.where(ms[c], a1[c][:c64, 0:c64], 0.0) for c in chains]
        g1 = [jnp.where(ms[c], a1[c][:c64, c64:2 * c64], 0.0) for c in chains]
        corr = _unit_tri_inverse_corr(low)
        z = [a1[c][:c64, 2 * c64:] + _mm(g1[c], v_h[c]) for c in chains]
        u = [z[c] + _mm(corr[c], z[c]) for c in chains]
        vu = [_bf(jnp.concatenate([v_h[c], u[c]], axis=0)) for c in chains]
        g23 = [_bf(jnp.concatenate([jnp.where(mi[c], a1[c][c64:, c64:2 * c64], 0.0),
                                    jnp.where(mi[c], -a1[c][c64:, 0:c64], 0.0)], axis=1)) for c in chains]
        y = [a1[c][c64:, 2 * c64:] + _dg(g23[c], vu[c]) for c in chains]
        for c in chains:
            q, d = groups[c // RWKV_HEADS]
            s_scr[q, d, c % RWKV_HEADS] = s[c] * et_h[c] + _dg(vu[c], upd[c], _TN)
        for gi, (q, d) in enumerate(groups):
            y_ref[d, rows_d[gi], :] = jnp.concatenate(y[gi * RWKV_HEADS:(gi + 1) * RWKV_HEADS], axis=1)
        return carry

    lax.fori_loop(0, n_chunks, body, 0)
    st_ref[...] = s_scr[...]


def _rwkv_scan(r, k, v, kap, beta, lw, s0, layer, seq, nb, row0, nseq):
    rows = nseq * seq
    rb = row0 // rows
    has_state = s0 is not None
    mode = dict(pipeline_mode=pl.Buffered(1)) if rows * RWKV_W * 4 > (2 << 20) else {}
    tok = pl.BlockSpec((rows, RWKV_W), lambda b: (rb + b, 0), **mode)
    in_specs = [tok, tok, tok, tok, tok,
                pl.BlockSpec((2, rows, RWKV_W), lambda b: (0, rb + b, 0), **mode)]
    args = [r, k, v, kap, beta, lw]
    st_shape = (2, RWKV_HEADS, RWKV_HD, RWKV_HD)
    if has_state:
        in_specs.append(pl.BlockSpec((nseq, None) + st_shape, lambda b: (b, layer, 0, 0, 0, 0)))
        args.append(s0)
    return pl.pallas_call(
        functools.partial(_rwkv_scan_kernel, seq=seq, nseq=nseq, has_state=has_state),
        grid=(nb // nseq,),
        in_specs=in_specs,
        out_specs=[
            pl.BlockSpec((2, rows, RWKV_W), lambda b: (0, b, 0), **mode),
            pl.BlockSpec((nseq,) + st_shape, lambda b: (b, 0, 0, 0, 0)),
        ],
        out_shape=[
            jax.ShapeDtypeStruct((2, nb * seq, RWKV_W), F32),
            jax.ShapeDtypeStruct((nb,) + st_shape, F32),
        ],
        scratch_shapes=[pltpu.VMEM((nseq,) + st_shape, F32)],
        compiler_params=_cparams(("arbitrary",)),
        name="rwkv_scan_lat" if has_state else "rwkv_scan_ctx",
    )(*args)


def _rwkv_out(ys, r, k, v, g, rho, lnw, lnb, bd_ref):
    bonus = _segsum(r * k * rho, bd_ref) * v
    y = ys[0] + ys[1] + bonus
    inv_n = 1.0 / RWKV_HD
    yc = y - _segsum(y, bd_ref) * inv_n
    var = _segsum(yc * yc, bd_ref) * inv_n
    yn = yc * lax.rsqrt(var + RWKV_LN_EPS)
    return _bf((yn * lnw + lnb) * g)


def _merge_kernel(oaa_ref, oab_ref, oga_ref, ogb_ref, yra_ref, yrb_ref, r_ref, k_ref, v_ref, g_ref,
                  rho_ref, lnw_ref, lnb_ref, bd_ref, gate_ref, xa_ref, xb_ref, mod_ref, nf_ref,
                  wa_ref, wg_ref, wr_ref, wo_ref, ya_ref, yb_ref, h_ref):
    d = D_MODEL
    tm = h_ref.shape[0]
    ctx = _is_ctx(pl.program_id(0), tm)
    pick = lambda a, b: jnp.where(ctx, a[...], b[...])
    o_rw = _rwkv_out(pick(yra_ref, yrb_ref), r_ref[...], k_ref[...], v_ref[...], g_ref[...],
                     rho_ref[...], lnw_ref[...], lnb_ref[...], bd_ref)
    gate = lambda k: _sigmoid(gate_ref[:, k * d:(k + 1) * d].astype(F32))
    merged = gate(0) * _dg(pick(oaa_ref, oab_ref), wa_ref[...])
    merged += gate(1) * _dg(pick(oga_ref, ogb_ref), wg_ref[...])
    merged += gate(2) * _dg(o_rw, wr_ref[...])
    y = pick(xa_ref, xb_ref) + mod_ref[2:3, :] * _dg(_bf(merged), wo_ref[...])
    h_ref[...] = (_rms(y, nf_ref[...]) * (1.0 + mod_ref[4:5, :]) + mod_ref[3:4, :]).astype(BF16)

    @pl.when(ctx)
    def _():
        ya_ref[...] = y

    @pl.when(jnp.logical_not(ctx))
    def _():
        yb_ref[...] = y


def _merge(o_att, o_gla, y_rw, rwkv_parts, rwkv_vecs, bd, gates, x, mod, norm_ffn, wa, wg, wr, wo, layer):
    tm = 256
    na = N_CTX // tm
    const = lambda r, c: pl.BlockSpec((None, r, c), lambda i: (layer, 0, 0), pipeline_mode=pl.Buffered(1))
    xa, xb = _ctx_lat_specs(tm, D_MODEL)
    tok = pl.BlockSpec((tm, RWKV_W), lambda i: (i, 0))
    vec = pl.BlockSpec((1, RWKV_W), lambda i: (0, 0))
    return pl.pallas_call(
        _merge_kernel,
        grid=(N_TOK // tm,),
        in_specs=[
            *_ctx_lat_specs(tm, ATT_Q_W),
            *_ctx_lat_specs(tm, GLA_W),
            pl.BlockSpec((2, tm, RWKV_W), lambda i: (0, jnp.minimum(i, na - 1), 0)),
            pl.BlockSpec((2, tm, RWKV_W), lambda i: (0, jnp.maximum(i - na, 0), 0)),
            tok, tok, tok, tok, vec, vec, vec,
            pl.BlockSpec((RWKV_W, RWKV_W), lambda i: (0, 0)),
            pl.BlockSpec((tm, GATE_W), lambda i: (i, 0)),
            xa, xb,
            pl.BlockSpec((None, 6, D_MODEL), lambda i: (_mod_row(i, tm), 0, 0)),
            pl.BlockSpec((1, D_MODEL), lambda i: (0, 0)),
            const(ATT_Q_W, D_MODEL), const(GLA_W, D_MODEL), const(RWKV_W, D_MODEL), const(D_MODEL, D_MODEL),
        ],
        out_specs=[xa, xb, pl.BlockSpec((tm, D_MODEL), lambda i: (i, 0))],
        out_shape=[jax.ShapeDtypeStruct((N_CTX, D_MODEL), F32), jax.ShapeDtypeStruct((N_LAT, D_MODEL), F32),
                   jax.ShapeDtypeStruct((N_TOK, D_MODEL), BF16)],
        compiler_params=_cparams(("arbitrary",)),
        name="merge_out",
    )(o_att[0], o_att[1], o_gla[0], o_gla[1], y_rw[0], y_rw[1], *rwkv_parts, *rwkv_vecs, bd, gates,
      x[0], x[1], mod, norm_ffn.reshape(1, D_MODEL), wa, wg, wr, wo)


_FFN_TN = 512
_FFN_SUB = 256
_FFN_RB = 256


def _ffn_up_kernel(h_ref, hp_ref, hn_ref, wv_ref, wg0_ref, wg1_ref, wg2_ref, wg3_ref,
                   cwv_ref, cwg_ref, cbv_ref, cbg_ref, o_ref, wv_scr, wg_scr):
    tm = h_ref.shape[0]
    i = pl.program_id(1)

    @pl.when(i == 0)
    def _():
        wv_scr[...] = wv_ref[...].astype(BF16)
        wg_scr[...] = jnp.concatenate([r[...] for r in (wg0_ref, wg1_ref, wg2_ref, wg3_ref)], axis=1).astype(BF16)

    h = h_ref[...]
    hp = hp_ref[...]
    hn = hn_ref[...]
    n_sub = _FFN_TN // _FFN_SUB
    n_rb = tm // _FFN_RB
    edges = [_seq_edges(i * tm + r * _FFN_RB, _FFN_RB) for r in range(n_rb)]

    def up(w_scr, s):
        w = w_scr[:, s * _FFN_SUB:(s + 1) * _FFN_SUB]
        return _dg(h, w), _dg(hp, w)[7:8, :], _dg(hn, w)[0:1, :]

    def conv(us, cw_ref, cb_ref, s, r):
        u_all, u_before, u_after = us
        cols = slice(s * _FFN_SUB, (s + 1) * _FFN_SUB)
        lo, hi = r * _FFN_RB, (r + 1) * _FFN_RB
        u = u_all[lo:hi]
        start, end = edges[r]
        before = jnp.where(start, 0.0, u_before if r == 0 else u_all[lo - 1:lo])
        after = jnp.where(end, 0.0, u_after if r == n_rb - 1 else u_all[hi:hi + 1])
        prev, nxt = _shift_rows(u, before, after)
        return cw_ref[0:1, cols] * prev + cw_ref[1:2, cols] * u + cw_ref[2:3, cols] * nxt + cb_ref[:, cols]

    ups = [(up(wv_scr, s), up(wg_scr, s)) for s in range(n_sub)]
    for s in range(n_sub):
        for r in range(n_rb):
            val = conv(ups[s][0], cwv_ref, cbv_ref, s, r)
            gate = conv(ups[s][1], cwg_ref, cbg_ref, s, r)
            o_ref[r * _FFN_RB:(r + 1) * _FFN_RB, s * _FFN_SUB:(s + 1) * _FFN_SUB] = _bf(gate * _sigmoid(gate) * val)


def _ffn_down_kernel(a_ref, w_ref, xa_ref, xb_ref, mod_ref, ya_ref, yb_ref):
    tm = a_ref.shape[0]
    ctx = _is_ctx(pl.program_id(1), tm)
    y = jnp.where(ctx, xa_ref[...], xb_ref[...]) + mod_ref[5:6, :] * _dg(a_ref[...], w_ref[...])

    @pl.when(ctx)
    def _():
        ya_ref[...] = y

    @pl.when(jnp.logical_not(ctx))
    def _():
        yb_ref[...] = y


def _ffn(h, x, mod, ffn_up, conv_wv, conv_wg, conv_bv, conv_bg, w_down, layer):
    tm, tn = 1024, _FFN_TN
    nj = pl.cdiv(FFN_HIDDEN, tn)
    hb = tm // 8
    gate0 = FFN_HIDDEN // LANE
    last_blk = 2 * FFN_HIDDEN // LANE - 1
    gate_blk = lambda q: pl.BlockSpec(
        (None, D_MODEL, LANE),
        lambda j, i: (layer, 0, jnp.minimum(gate0 + (tn // LANE) * j + q, last_blk)))
    act = pl.pallas_call(
        _ffn_up_kernel,
        grid=(nj, N_TOK // tm),
        in_specs=[
            pl.BlockSpec((tm, D_MODEL), lambda j, i: (i, 0)),
            pl.BlockSpec((8, D_MODEL), lambda j, i: (jnp.maximum(i * hb - 1, 0), 0)),
            pl.BlockSpec((8, D_MODEL), lambda j, i: (jnp.minimum((i + 1) * hb, N_TOK // 8 - 1), 0)),
            pl.BlockSpec((None, D_MODEL, tn), lambda j, i: (layer, 0, j)),
            gate_blk(0), gate_blk(1), gate_blk(2), gate_blk(3),
            pl.BlockSpec((3, tn), lambda j, i: (0, j)),
            pl.BlockSpec((3, tn), lambda j, i: (0, j)),
            pl.BlockSpec((1, tn), lambda j, i: (0, j)),
            pl.BlockSpec((1, tn), lambda j, i: (0, j)),
        ],
        out_specs=pl.BlockSpec((tm, tn), lambda j, i: (i, j)),
        out_shape=jax.ShapeDtypeStruct((N_TOK, FFN_HIDDEN), BF16),
        scratch_shapes=[pltpu.VMEM((D_MODEL, tn), BF16), pltpu.VMEM((D_MODEL, tn), BF16)],
        compiler_params=_cparams(("arbitrary", "arbitrary")),
        name="ffn_up",
    )(h, h, h, ffn_up, ffn_up, ffn_up, ffn_up, ffn_up, conv_wv, conv_wg, conv_bv, conv_bg)
    to = 512
    xa, xb = _ctx_lat_specs(tm, to, grid_rank=2)
    return pl.pallas_call(
        _ffn_down_kernel,
        grid=(D_MODEL // to, N_TOK // tm),
        in_specs=[
            pl.BlockSpec((tm, FFN_HIDDEN), lambda n, i: (i, 0)),
            pl.BlockSpec((None, FFN_HIDDEN, to), lambda n, i: (layer, 0, n)),
            xa, xb,
            pl.BlockSpec((None, 6, to), lambda n, i: (_mod_row(i, tm), 0, n)),
        ],
        out_specs=[xa, xb],
        out_shape=[jax.ShapeDtypeStruct((N_CTX, D_MODEL), F32), jax.ShapeDtypeStruct((N_LAT, D_MODEL), F32)],
        compiler_params=_cparams(("arbitrary", "arbitrary")),
        name="ffn_down",
    )(act, w_down, x[0], x[1], mod)


def _pad_cols(x, n):
    return jnp.pad(x, ((0, 0), (0, n - x.shape[1])))


def _pad_rows(x, n, at=0):
    return jnp.pad(x, ((at, n - x.shape[0] - at), (0, 0)))


def _rope_tables(t):
    rows = t // GRID_W
    row = jnp.repeat(jnp.arange(rows, dtype=F32), GRID_W)
    col = jnp.tile(jnp.arange(GRID_W, dtype=F32), rows)
    half = HEAD_DIM // 2
    inv = ROPE_THETA ** (-jnp.arange(0, half, 2, dtype=F32) / half)

    def cos_sin(pos):
        ang = pos[:, None] * inv[None, :]
        ang = jnp.concatenate([ang, ang], axis=-1)
        return jnp.cos(ang), jnp.sin(ang)

    cos_r, sin_r = cos_sin(row)
    cos_c, sin_c = cos_sin(col)
    cos = jnp.concatenate([cos_r, cos_c], axis=-1)
    sin = jnp.concatenate([sin_r, sin_c], axis=-1)
    sign = jnp.where((jnp.arange(HEAD_DIM) % half) < half // 2, -1.0, 1.0).astype(F32)
    return cos, sin * sign[None, :]


def kernel(x_prompt, x_sample, cache_k, cache_v, state_gla, state_rwkv, c, c_ctx, w_mod, b_mod, norm_mix, w_in, q_norm, k_norm, gla_a_up, gla_a_bias, gla_norm, rwkv_mu, rwkv_w0, rwkv_w_up, rwkv_a0, rwkv_a_up, rwkv_g_up, rwkv_k_xi, rwkv_k_alpha, rwkv_bonus, rwkv_ln_w, rwkv_ln_b, w_br_att, w_br_gla, w_br_rwkv, w_out, norm_ffn, ffn_up, ffn_conv_w, ffn_conv_b, ffn_down):
    x = (x_prompt.reshape(N_CTX, D_MODEL), x_sample.reshape(N_LAT, D_MODEL))
    cvec = jnp.concatenate([c_ctx[None, :], c, jnp.zeros((N_MODROWS - 1 - DEC_BATCH, D_MODEL), F32)], axis=0)
    mod_all = _modulation(cvec, w_mod, b_mod).reshape(DEPTH, N_MODROWS, 6, D_MODEL)
    cos, sin = _rope_tables(DEC_SEQ)
    hr = lax.broadcasted_iota(jnp.int32, (RWKV_W, RWKV_W), 0) // RWKV_HD
    hc = lax.broadcasted_iota(jnp.int32, (RWKV_W, RWKV_W), 1) // RWKV_HD
    bd = (hr == hc).astype(BF16)
    wa_all, wg_all, wr_all, wo_all, wd_all = (_bf(w) for w in (w_br_att, w_br_gla, w_br_rwkv, w_out, ffn_down))
    ffn_pad = pl.cdiv(FFN_HIDDEN, _FFN_TN) * _FFN_TN
    w_in_t = jnp.swapaxes(w_in, 1, 2)

    ks, vs, gla_states, rwkv_states = [], [], [], []
    for l in range(DEPTH):
        mod = mod_all[l]
        h = _prenorm(x, norm_mix[l], mod, 0)
        proj = _proj(h, w_in_t, l, 0, MAIN_W, 512, "proj_main")
        p_rw = _proj(h, w_in_t, l, COL_RW, RW_W, 512, "proj_rwkv")
        gates = _proj(h, w_in_t, l, COL_GATE, GATE_W, GATE_TN, "proj_gates", BF16)

        qn = q_norm[l].reshape(1, HEAD_DIM)
        kn = k_norm[l].reshape(1, HEAD_DIM)
        oa_c, k_c, v_c = _attention_ctx(proj, qn, kn)
        oa_l = _attention_lat(proj, qn, kn, cos, sin, cache_k, cache_v, l)
        ks.append(k_c)
        vs.append(v_c)

        up = gla_a_up[l].reshape(2, GLA_RANK, GLA_HEADS, GLA_DK).transpose(0, 2, 1, 3)
        up_pad = jnp.stack([
            jnp.pad(up[0], ((0, 0), (0, LANE - GLA_RANK), (0, 0))),
            jnp.pad(up[1], ((0, 0), (GLA_RANK, LANE - 2 * GLA_RANK), (0, 0))),
        ])
        gbias = gla_a_bias[l].reshape(2, GLA_HEADS, 1, GLA_DK)
        gnorm = gla_norm[l].reshape(1, GLA_DK)
        og_c, gs_c = _gla(proj, up_pad, gbias, gnorm, None, l, SEQ, BATCH, 0, GLA_HEADS)
        og_l, _ = _gla(proj, up_pad, gbias, gnorm, state_gla, l, DEC_SEQ, DEC_BATCH, N_CTX, 1)
        gla_states.append(gs_c)

        row = lambda a: a.reshape(1, -1)
        mu = _pad_cols(row(rwkv_mu[l]), RW_W)
        wuf = _bf(_pad_rows(rwkv_w_up[l, 0], LANE))
        wub = _bf(_pad_rows(rwkv_w_up[l, 1], LANE, at=RWKV_W_RANK))
        au = _bf(_pad_rows(rwkv_a_up[l], LANE))
        gu = _bf(_pad_rows(rwkv_g_up[l], 2 * LANE, at=RWKV_A_RANK))
        r_, k_, v_, kap_, beta_, lw_, g_ = _rwkv_prep(
            p_rw, mu, wuf, wub, au, gu, rwkv_w0[l], row(rwkv_a0[l]), row(rwkv_k_xi[l]),
            row(rwkv_k_alpha[l]), bd)
        y_c, rs_c = _rwkv_scan(r_, k_, v_, kap_, beta_, lw_, None, l, SEQ, BATCH, 0, 2)
        y_l, _ = _rwkv_scan(r_, k_, v_, kap_, beta_, lw_, state_rwkv, l, DEC_SEQ, DEC_BATCH, N_CTX, 2)
        rwkv_states.append(rs_c)

        xa, xb, h2 = _merge((oa_c, oa_l), (og_c, og_l), (y_c, y_l), (r_, k_, v_, g_),
                            (row(rwkv_bonus[l]), row(rwkv_ln_w[l]), row(rwkv_ln_b[l])), bd, gates, x, mod,
                            norm_ffn[l], wa_all, wg_all, wr_all, wo_all, l)
        x = (xa, xb)
        cw = ffn_conv_w[l]
        cb = row(ffn_conv_b[l])
        x = _ffn(h2, x, mod, ffn_up,
                 _pad_cols(cw[:, :FFN_HIDDEN], ffn_pad), _pad_cols(cw[:, FFN_HIDDEN:], ffn_pad),
                 _pad_cols(cb[:, :FFN_HIDDEN], ffn_pad), _pad_cols(cb[:, FFN_HIDDEN:], ffn_pad),
                 wd_all, l)

    y_prompt = x[0].reshape(BATCH, SEQ, D_MODEL)
    y_sample = x[1].reshape(DEC_BATCH, DEC_SEQ, D_MODEL)
    new_cache_k = jnp.stack(ks, axis=1)
    new_cache_v = jnp.stack(vs, axis=1)
    new_state_gla = jnp.stack(gla_states, axis=1)
    new_state_rwkv = jnp.stack(rwkv_states, axis=1)
    return (y_prompt, y_sample, new_cache_k, new_cache_v, new_state_gla, new_state_rwkv)
```

```python
import functools

import jax
import jax.numpy as jnp
from jax import lax
from jax.experimental import pallas as pl
from jax.experimental.pallas import tpu as pltpu

F32 = jnp.float32
BF16 = jnp.bfloat16

D_MODEL = 2048
BATCH = 16
SEQ = 256
DEPTH = 2
DEC_BATCH = 4
DEC_SEQ = 1024
PAST_LEN = 256
GRID_W = 64
HEAD_DIM = 128
ATT_HEADS = 8
ATT_KV_HEADS = 2
ROPE_THETA = 10000.0
GLA_HEADS = 4
GLA_DK = 128
GLA_RANK = 16
GLA_TAU = 16.0
RWKV_HEADS = 8
RWKV_HD = 64
RWKV_W_RANK = 64
RWKV_A_RANK = 64
RWKV_G_RANK = 128
RWKV_DECAY_SCALE = 0.606531
RWKV_LN_EPS = 64e-5
FFN_HIDDEN = 5504
EPS = 1e-6

ATT_Q_W = ATT_HEADS * HEAD_DIM
ATT_KV_W = ATT_KV_HEADS * HEAD_DIM
GLA_W = GLA_HEADS * GLA_DK
RWKV_W = RWKV_HEADS * RWKV_HD
RWKV_COLS = 3 * RWKV_W + 2 * RWKV_W_RANK + RWKV_A_RANK + RWKV_G_RANK
IN_COLS = ATT_Q_W + 2 * ATT_KV_W + 4 * GLA_W + 2 * GLA_RANK + RWKV_COLS + 3 * D_MODEL

LANE = 128
CHUNK = 64
N_CTX = BATCH * SEQ
N_LAT = DEC_BATCH * DEC_SEQ
N_TOK = N_CTX + N_LAT
N_MODROWS = 8
VMEM_LIMIT = 56 * 1024 * 1024
_HALO = 16

COL_GLA = ATT_Q_W + 2 * ATT_KV_W
COL_GAD = COL_GLA + 4 * GLA_W
COL_RW = COL_GAD + 2 * GLA_RANK
COL_GATE = COL_RW + RWKV_COLS
MAIN_W = 4096
RW_W = 2048
GATE_TN = 768
GATE_W = 3 * D_MODEL

_NT = (((1,), (1,)), ((), ()))
_TN = (((0,), (0,)), ((), ()))
_NN = (((1,), (0,)), ((), ()))


def _bf(x):
    return x.astype(BF16)


def _dg(a, b, dims=_NN):
    return lax.dot_general(a, b, dims, preferred_element_type=F32)


def _mm(a, b, dims=_NN):
    return _dg(_bf(a), _bf(b), dims)


def _split3(x):
    h = x.astype(BF16)
    r = x - h.astype(F32)
    m = r.astype(BF16)
    return h, m, (r - m.astype(F32)).astype(BF16)


def _mm_exact_rhs(a, b_bf, dims=_NN):
    h, m, l = _split3(a)
    return _dg(h, b_bf, dims) + (_dg(m, b_bf, dims) + _dg(l, b_bf, dims))


def _sigmoid(x):
    return 0.5 * jnp.tanh(0.5 * x) + 0.5


def _rms(x, g):
    return x * lax.rsqrt(jnp.mean(x * x, axis=-1, keepdims=True) + EPS) * g


def _cparams(sem):
    return pltpu.CompilerParams(dimension_semantics=sem, vmem_limit_bytes=VMEM_LIMIT)


def _mod_row(i, tm):
    n_ctx = N_CTX // tm
    return jnp.where(i < n_ctx, 0, 1 + (i - n_ctx) // (DEC_SEQ // tm))


def _ctx_lat_specs(tm, width, grid_rank=1):
    na = N_CTX // tm
    if grid_rank == 1:
        return (pl.BlockSpec((tm, width), lambda i: (jnp.minimum(i, na - 1), 0)),
                pl.BlockSpec((tm, width), lambda i: (jnp.maximum(i - na, 0), 0)))
    return (pl.BlockSpec((tm, width), lambda n, i: (jnp.minimum(i, na - 1), n)),
            pl.BlockSpec((tm, width), lambda n, i: (jnp.maximum(i - na, 0), n)))


def _is_ctx(i, tm):
    return i < N_CTX // tm


def _seq_edges(g, rows):
    lat = g - N_CTX
    start = (g < N_CTX) | ((lat & (DEC_SEQ - 1)) == 0)
    end = (g < N_CTX) | (((lat + rows) & (DEC_SEQ - 1)) == 0)
    if rows < SEQ:
        start = jnp.where(g < N_CTX, (g & (SEQ - 1)) == 0, start)
        end = jnp.where(g < N_CTX, ((g + rows) & (SEQ - 1)) == 0, end)
    return start, end


def _shift_rows(x, before, after):
    n = x.shape[0]
    row = lax.broadcasted_iota(jnp.int32, (n, 1), 0)
    prev = jnp.where(row == 0, before, pltpu.roll(x, 1, 0))
    nxt = jnp.where(row == n - 1, after, pltpu.roll(x, n - 1, 0))
    return prev, nxt


def _mod_kernel(c_ref, w_ref, b_ref, o_ref):
    c = c_ref[...]
    o_ref[...] = _mm(c * _sigmoid(c), w_ref[...]) + b_ref[...]


def _modulation(cvec, w_mod, b_mod):
    tn = 1024
    n = 6 * D_MODEL
    return pl.pallas_call(
        _mod_kernel,
        grid=(DEPTH, n // tn),
        in_specs=[
            pl.BlockSpec((N_MODROWS, D_MODEL), lambda l, j: (0, 0)),
            pl.BlockSpec((None, D_MODEL, tn), lambda l, j: (l, 0, j)),
            pl.BlockSpec((None, 1, tn), lambda l, j: (l, 0, j)),
        ],
        out_specs=pl.BlockSpec((None, N_MODROWS, tn), lambda l, j: (l, 0, j)),
        out_shape=jax.ShapeDtypeStruct((DEPTH, N_MODROWS, n), F32),
        compiler_params=_cparams(("arbitrary", "arbitrary")),
        name="modulation",
    )(cvec, w_mod, b_mod.reshape(DEPTH, 1, n))


def _prenorm_kernel(xa_ref, xb_ref, g_ref, mod_ref, o_ref, *, shift_idx):
    tm = o_ref.shape[0]
    x = jnp.where(_is_ctx(pl.program_id(0), tm), xa_ref[...], xb_ref[...])
    y = _rms(x, g_ref[...])
    sh = mod_ref[shift_idx:shift_idx + 1, :]
    sc = mod_ref[shift_idx + 1:shift_idx + 2, :]
    o_ref[...] = (y * (1.0 + sc) + sh).astype(BF16)


def _prenorm(x, g, mod, shift_idx):
    tm = 512
    xa, xb = _ctx_lat_specs(tm, D_MODEL)
    return pl.pallas_call(
        functools.partial(_prenorm_kernel, shift_idx=shift_idx),
        grid=(N_TOK // tm,),
        in_specs=[
            xa, xb,
            pl.BlockSpec((1, D_MODEL), lambda i: (0, 0)),
            pl.BlockSpec((None, 6, D_MODEL), lambda i: (_mod_row(i, tm), 0, 0)),
        ],
        out_specs=pl.BlockSpec((tm, D_MODEL), lambda i: (i, 0)),
        out_shape=jax.ShapeDtypeStruct((N_TOK, D_MODEL), BF16),
        compiler_params=_cparams(("arbitrary",)),
        name="prenorm",
    )(x[0], x[1], g.reshape(1, D_MODEL), mod)


def _proj_kernel(*refs, n_w, row_shift):
    h_ref = refs[0]
    w_refs = refs[1:1 + n_w]
    o_ref, w_scr = refs[1 + n_w:]

    @pl.when(pl.program_id(1) == 0)
    def _():
        w = w_refs[0][...] if n_w == 1 else jnp.concatenate([r[...] for r in w_refs], axis=0)
        w_scr[...] = w[row_shift:row_shift + w_scr.shape[0], :].astype(BF16)

    o_ref[...] = _dg(h_ref[...], w_scr[...], _NT).astype(o_ref.dtype)


def _proj(h, w_in_t, layer, col0, width, tn, name, dtype=F32):
    tm = 1024
    off = col0 % tn
    base = col0 - off
    assert width % tn == 0 and off % 8 == 0 and (off == 0 or (tn % off == 0 and IN_COLS % off == 0))
    in_specs = [
        pl.BlockSpec((tm, D_MODEL), lambda j, i: (i, 0)),
        pl.BlockSpec((None, tn, D_MODEL), lambda j, i: (layer, base // tn + j, 0)),
    ]
    if off:
        in_specs.append(pl.BlockSpec((None, off, D_MODEL), lambda j, i: (layer, (base + tn * (j + 1)) // off, 0)))
    return pl.pallas_call(
        functools.partial(_proj_kernel, n_w=len(in_specs) - 1, row_shift=off),
        grid=(width // tn, N_TOK // tm),
        in_specs=in_specs,
        out_specs=pl.BlockSpec((tm, tn), lambda j, i: (i, j)),
        out_shape=jax.ShapeDtypeStruct((N_TOK, width), dtype),
        scratch_shapes=[pltpu.VMEM((tn, D_MODEL), BF16)],
        compiler_params=_cparams(("arbitrary", "arbitrary")),
        name=name,
    )(h, *([w_in_t] * (len(in_specs) - 1)))


_ATT_SCALE = HEAD_DIM ** -0.5
_Q_PER_KV = ATT_HEADS // ATT_KV_HEADS


def _rope(x, cos, sin_signed):
    lane = lax.broadcasted_iota(jnp.int32, x.shape, 1)
    partner = jnp.where((lane & 63) < 32, pltpu.roll(x, 96, 1), pltpu.roll(x, 32, 1))
    return x * cos + partner * sin_signed


def _softmax_pv(s, v_ones):
    p = _bf(jnp.exp(s - jnp.max(s, axis=-1, keepdims=True)))
    oa = _dg(p, v_ones)
    return oa[:, :HEAD_DIM] / oa[:, HEAD_DIM:]


def _attn_ctx_kernel(q_ref, k_ref, v_ref, qn_ref, kn_ref, o_ref, ko_ref, vo_ref):
    k = _rms(k_ref[...].astype(F32), kn_ref[...])
    v = v_ref[...]
    ko_ref[...] = k
    vo_ref[...] = v.astype(F32)
    kb = _bf(k)
    v_ones = jnp.concatenate([v, jnp.ones(v.shape, BF16)], axis=1)
    heads = range(_Q_PER_KV)
    qs = [_bf(_rms(q_ref[:, h * HEAD_DIM:(h + 1) * HEAD_DIM].astype(F32), qn_ref[...])) for h in heads]
    ss = [_dg(qs[h], kb, _NT) * _ATT_SCALE for h in heads]
    o_ref[...] = jnp.concatenate([_softmax_pv(s, v_ones) for s in ss], axis=1).astype(BF16)


def _attn_lat_kernel(q_ref, k_ref, v_ref, qn_ref, kn_ref, cos_ref, sin_ref, ck_ref, cv_ref, o_ref):
    cos = cos_ref[...]
    sin = sin_ref[...]
    k = _rope(_rms(k_ref[...].astype(F32), kn_ref[...]), cos, sin)
    kb = jnp.concatenate([_bf(ck_ref[...]), _bf(k)], axis=0)
    vb = jnp.concatenate([_bf(cv_ref[...]), v_ref[...]], axis=0)
    v_ones = jnp.concatenate([vb, jnp.ones(vb.shape, BF16)], axis=1)
    rb = 256
    blocks = range(DEC_SEQ // rb)

    def scores(h):
        q = q_ref[:, h * HEAD_DIM:(h + 1) * HEAD_DIM].astype(F32)
        q = _bf(_rope(_rms(q, qn_ref[...]), cos, sin))
        return [_dg(q[r * rb:(r + 1) * rb], kb, _NT) * _ATT_SCALE for r in blocks]

    ss_next = scores(0)
    for h in range(_Q_PER_KV):
        ss = ss_next
        if h + 1 < _Q_PER_KV:
            ss_next = scores(h + 1)
        o = jnp.concatenate([_softmax_pv(s, v_ones) for s in ss], axis=0)
        o_ref[:, h * HEAD_DIM:(h + 1) * HEAD_DIM] = o.astype(BF16)


def _attention_ctx(proj, qn, kn):
    blk = lambda w, f: pl.BlockSpec((SEQ, w), f)
    vec = pl.BlockSpec((1, HEAD_DIM), lambda b, g: (0, 0))
    kv_out = pl.BlockSpec((None, None, SEQ, HEAD_DIM), lambda b, g: (b, g, 0, 0))
    qw = _Q_PER_KV * HEAD_DIM
    return pl.pallas_call(
        _attn_ctx_kernel,
        grid=(BATCH, ATT_KV_HEADS),
        in_specs=[
            blk(qw, lambda b, g: (b, g)),
            blk(HEAD_DIM, lambda b, g: (b, ATT_HEADS + g)),
            blk(HEAD_DIM, lambda b, g: (b, ATT_HEADS + ATT_KV_HEADS + g)),
            vec, vec,
        ],
        out_specs=[blk(qw, lambda b, g: (b, g)), kv_out, kv_out],
        out_shape=[
            jax.ShapeDtypeStruct((N_CTX, ATT_Q_W), BF16),
            jax.ShapeDtypeStruct((BATCH, ATT_KV_HEADS, SEQ, HEAD_DIM), F32),
            jax.ShapeDtypeStruct((BATCH, ATT_KV_HEADS, SEQ, HEAD_DIM), F32),
        ],
        compiler_params=_cparams(("arbitrary", "arbitrary")),
        name="attention_ctx",
    )(proj, proj, proj, qn, kn)


def _attention_lat(proj, qn, kn, cos, sin, cache_k, cache_v, layer):
    rb = N_CTX // DEC_SEQ
    qw = _Q_PER_KV * HEAD_DIM
    blk = lambda w, f: pl.BlockSpec((DEC_SEQ, w), f)
    vec = pl.BlockSpec((1, HEAD_DIM), lambda b, g: (0, 0))
    tab = pl.BlockSpec((DEC_SEQ, HEAD_DIM), lambda b, g: (0, 0))
    cache = pl.BlockSpec((None, None, None, PAST_LEN, HEAD_DIM), lambda b, g: (b, layer, g, 0, 0))
    return pl.pallas_call(
        _attn_lat_kernel,
        grid=(DEC_BATCH, ATT_KV_HEADS),
        in_specs=[
            blk(qw, lambda b, g: (rb + b, g)),
            blk(HEAD_DIM, lambda b, g: (rb + b, ATT_HEADS + g)),
            blk(HEAD_DIM, lambda b, g: (rb + b, ATT_HEADS + ATT_KV_HEADS + g)),
            vec, vec, tab, tab, cache, cache,
        ],
        out_specs=blk(qw, lambda b, g: (b, g)),
        out_shape=jax.ShapeDtypeStruct((N_LAT, ATT_Q_W), BF16),
        compiler_params=_cparams(("arbitrary", "arbitrary")),
        name="attention_lat",
    )(proj, proj, proj, qn, kn, cos, sin, cache_k, cache_v)


def _tri(n, upper):
    r = lax.broadcasted_iota(jnp.int32, (n, n), 0)
    c = lax.broadcasted_iota(jnp.int32, (n, n), 1)
    return (r <= c) if upper else (r >= c)


def _gla_kernel(*refs, seq, has_state):
    if has_state:
        q_ref, k_ref, v_ref, gg_ref, gad_ref, up_ref, bias_ref, gn_ref, s0_ref, o_ref, st_ref = refs
    else:
        q_ref, k_ref, v_ref, gg_ref, gad_ref, up_ref, bias_ref, gn_ref, o_ref, st_ref = refs
        s0_ref = None
    nc = seq // CHUNK
    nh = q_ref.shape[1] // LANE
    c3 = (nc, CHUNK, LANE)
    stack = lambda x: jnp.concatenate([x[:, h * LANE:(h + 1) * LANE].reshape(c3) for h in range(nh)], axis=0)
    gad = gad_ref[...]
    q3 = stack(q_ref[...].astype(F32) * (GLA_DK ** -0.5))
    k3 = stack(k_ref[...].astype(F32))
    v3 = stack(v_ref[...])
    bdot = lambda a, b, ca, cb: lax.dot_general(a, b, (((ca,), (cb,)), ((0,), (0,))), preferred_element_type=F32)

    qe, oi, kv, dec = [], [], [], []
    for d in range(2):
        incl = _tri(CHUNK, d == 1)
        tri = jnp.broadcast_to(jnp.where(incl, 1.0, 0.0).astype(BF16)[None], (nh * nc, CHUNK, CHUNK))
        la = jnp.concatenate(
            [(jax.nn.log_sigmoid(_dg(gad, _bf(up_ref[d, h])) + bias_ref[d, h]) / GLA_TAU).reshape(c3)
             for h in range(nh)], axis=0)
        hi, mid, lo = _split3(la)
        cum = bdot(tri, hi, 2, 1) + (bdot(tri, mid, 2, 1) + bdot(tri, lo, 2, 1))
        tot = cum[:, 0:1, :] if d else cum[:, CHUNK - 1:CHUNK, :]
        qe_d = _bf(q3 * jnp.exp(cum))
        ke = _bf(k3 * jnp.exp(-cum))
        kl = _bf(k3 * jnp.exp(tot - cum))
        att = jnp.where(incl[None], bdot(qe_d, ke, 2, 2), 0.0)
        qe.append(qe_d)
        oi.append(bdot(_bf(att), v3, 2, 1))
        kv.append(bdot(v3, kl, 1, 1))
        dec.append(jnp.exp(tot))

    zero = jnp.zeros((GLA_DK, GLA_DK), F32)
    st = [[s0_ref[d, h].T if has_state else zero for h in range(nh)] for d in range(2)]
    o_f = [[None] * nc for _ in range(nh)]
    o_b = [[None] * nc for _ in range(nh)]
    for t in range(nc):
        nf, nb = t, nc - 1 - t
        for h in range(nh):
            o_f[h][nf] = oi[0][h * nc + nf] + _dg(qe[0][h * nc + nf], _bf(st[0][h]), _NT)
            o_b[h][nb] = oi[1][h * nc + nb] + _dg(qe[1][h * nc + nb], _bf(st[1][h]), _NT)
        for h in range(nh):
            st[0][h] = st[0][h] * dec[0][h * nc + nf] + kv[0][h * nc + nf]
            st[1][h] = st[1][h] * dec[1][h * nc + nb] + kv[1][h * nc + nb]
    outs = []
    for h in range(nh):
        st_ref[0, h] = st[0][h].T
        st_ref[1, h] = st[1][h].T
        outs.append(_rms(jnp.concatenate([a + b for a, b in zip(o_f[h], o_b[h])], axis=0), gn_ref[...]))
    g = gg_ref[...].astype(F32)
    o_ref[...] = (jnp.concatenate(outs, axis=1) * (g * _sigmoid(g))).astype(o_ref.dtype)


def _gla(proj, up_pad, bias, gnorm, s0, layer, seq, nb, row0, nh):
    rb = row0 // seq
    w = nh * LANE
    c0 = COL_GLA // w
    per = GLA_HEADS // nh
    blk = lambda c: pl.BlockSpec((seq, w), lambda b, h: (rb + b, c0 + c * per + h))
    has_state = s0 is not None
    in_specs = [
        blk(0), blk(1), blk(2), blk(3),
        pl.BlockSpec((seq, LANE), lambda b, h: (rb + b, COL_GAD // LANE)),
        pl.BlockSpec((2, nh, LANE, LANE), lambda b, h: (0, h, 0, 0)),
        pl.BlockSpec((2, nh, 1, LANE), lambda b, h: (0, h, 0, 0)),
        pl.BlockSpec((1, LANE), lambda b, h: (0, 0)),
    ]
    args = [proj, proj, proj, proj, proj, up_pad, bias, gnorm]
    if has_state:
        in_specs.append(pl.BlockSpec((None, None, 2, nh, GLA_DK, GLA_DK),
                                     lambda b, h: (b, layer, 0, h, 0, 0)))
        args.append(s0)
    return pl.pallas_call(
        functools.partial(_gla_kernel, seq=seq, has_state=has_state),
        grid=(nb, per),
        in_specs=in_specs,
        out_specs=[
            pl.BlockSpec((seq, w), lambda b, h: (b, h)),
            pl.BlockSpec((None, 2, nh, GLA_DK, GLA_DK), lambda b, h: (b, 0, h, 0, 0)),
        ],
        out_shape=[
            jax.ShapeDtypeStruct((nb * seq, GLA_W), BF16),
            jax.ShapeDtypeStruct((nb, 2, GLA_HEADS, GLA_DK, GLA_DK), F32),
        ],
        compiler_params=_cparams(("arbitrary", "arbitrary")),
        name="gla_lat" if has_state else "gla_ctx",
    )(*args)


def _segsum(x, bd_ref):
    return _mm_exact_rhs(x, bd_ref[...])


def _cumsum(tri_bf, x):
    h, m, l = _split3(x)
    return _dg(tri_bf, h) + (_dg(tri_bf, m) + _dg(tri_bf, l))


def _rwkv_prep_kernel(x_ref, xp_ref, xn_ref, mu_ref, wuf_ref, wub_ref, au_ref,
                      gu_ref, w0_ref, a0_ref, xi_ref, al_ref, bd_ref,
                      r_ref, k_ref, v_ref, kap_ref, beta_ref, lw_ref, g_ref):
    tm = x_ref.shape[0]
    start, end = _seq_edges(pl.program_id(0) * tm, tm)
    x = x_ref[...].astype(F32)
    before = jnp.where(start, 0.0, xp_ref[...].astype(F32)[_HALO - 1:_HALO, :])
    after = jnp.where(end, 0.0, xn_ref[...].astype(F32)[0:1, :])
    prev, nxt = _shift_rows(x, before, after)
    rw = x + (0.5 * (prev + nxt) - x) * mu_ref[...]
    w = RWKV_W
    rr = rw[:, 0:w]
    rk = rw[:, w:2 * w]
    rv = rw[:, 2 * w:3 * w]
    rwd = _bf(jnp.tanh(rw[:, 3 * w:3 * w + LANE]))
    rad = _bf(rw[:, 3 * w + LANE:3 * w + 2 * LANE])
    rgd = _bf(_sigmoid(rw[:, 3 * w + LANE:3 * w + 3 * LANE]))
    lw_ref[0] = -RWKV_DECAY_SCALE * _sigmoid(w0_ref[0:1, :] + _dg(rwd, wuf_ref[...]))
    lw_ref[1] = -RWKV_DECAY_SCALE * _sigmoid(w0_ref[1:2, :] + _dg(rwd, wub_ref[...]))
    a = _sigmoid(a0_ref[...] + _dg(rad, au_ref[...]))
    g_ref[...] = _dg(rgd, gu_ref[...])
    kap = rk * xi_ref[...]
    kap = kap * lax.rsqrt(_segsum(kap * kap, bd_ref) + EPS)
    r_ref[...] = rr
    k_ref[...] = rk * (1.0 + (a - 1.0) * al_ref[...])
    v_ref[...] = rv
    kap_ref[...] = kap
    beta_ref[...] = kap * a


def _rwkv_prep(proj, mu, wuf, wub, au, gu, w0, a0, xi, al, bd):
    tm = 256
    nt = N_TOK // tm
    hb = tm // _HALO
    full = lambda shape: pl.BlockSpec(shape, lambda i: (0,) * len(shape))
    tok = pl.BlockSpec((tm, RWKV_W), lambda i: (i, 0))
    tshape = jax.ShapeDtypeStruct((N_TOK, RWKV_W), F32)
    return pl.pallas_call(
        _rwkv_prep_kernel,
        grid=(nt,),
        in_specs=[
            pl.BlockSpec((tm, RW_W), lambda i: (i, 0)),
            pl.BlockSpec((_HALO, RW_W), lambda i: (jnp.maximum(i * hb - 1, 0), 0)),
            pl.BlockSpec((_HALO, RW_W), lambda i: (jnp.minimum((i + 1) * hb, N_TOK // _HALO - 1), 0)),
            full((1, RW_W)),
            full((LANE, RWKV_W)), full((LANE, RWKV_W)), full((LANE, RWKV_W)), full((2 * LANE, RWKV_W)),
            full((2, RWKV_W)), full((1, RWKV_W)), full((1, RWKV_W)), full((1, RWKV_W)),
            full((RWKV_W, RWKV_W)),
        ],
        out_specs=[tok, tok, tok, tok, tok,
                   pl.BlockSpec((2, tm, RWKV_W), lambda i: (0, i, 0)), tok],
        out_shape=[tshape, tshape, tshape, tshape, tshape,
                   jax.ShapeDtypeStruct((2, N_TOK, RWKV_W), F32), tshape],
        compiler_params=_cparams(("arbitrary",)),
        name="rwkv_prep",
    )(proj, proj, proj, mu, wuf, wub, au, gu, w0, a0, xi, al, bd)


def _unit_tri_inverse_corr(lows):
    c = [-low for low in lows]
    pb = [_bf(low) for low in lows]
    p = [_dg(b, b) for b in pb]
    span = 2
    while span < CHUNK:
        pb = [_bf(x) for x in p]
        c = [a + x + _dg(_bf(a), xb) for a, x, xb in zip(c, p, pb)]
        span *= 2
        if span < CHUNK:
            p = [_dg(xb, xb) for xb in pb]
    return c


def _rwkv_scan_kernel(*refs, seq, nseq, has_state):
    if has_state:
        r_ref, k_ref, v_ref, kap_ref, beta_ref, lw_ref, s0_ref, y_ref, st_ref, s_scr = refs
    else:
        r_ref, k_ref, v_ref, kap_ref, beta_ref, lw_ref, y_ref, st_ref, s_scr = refs
    n_chunks = seq // CHUNK
    c64 = CHUNK
    rr = lax.broadcasted_iota(jnp.int32, (c64, c64), 0)
    cc = lax.broadcasted_iota(jnp.int32, (c64, c64), 1)
    incl = (rr >= cc, rr <= cc)
    strict = (rr > cc, rr < cc)
    tri = tuple(jnp.where(m, 1.0, 0.0).astype(BF16) for m in incl)
    if has_state:
        s_scr[...] = s0_ref[...]
    else:
        s_scr[...] = jnp.zeros_like(s_scr)
    heads = range(RWKV_HEADS)
    cut = lambda x: [x[:, h * RWKV_HD:(h + 1) * RWKV_HD] for h in heads]
    groups = [(q, d) for q in range(nseq) for d in range(2)]

    def body(i, carry):
        lhs, rhs, upd, v_h, et_h, s, ms, mi, rows_d = [], [], [], [], [], [], [], [], []
        for q, d in groups:
            n = (n_chunks - 1 - i) if d else i
            rows = pl.ds(pl.multiple_of(q * seq + n * c64, c64), c64)
            lw = lw_ref[d, rows, :]
            cum = _cumsum(tri[d], lw)
            tot = cum[0:1, :] if d else cum[c64 - 1:c64, :]
            e_neg = jnp.exp(-cum)
            e_tot = jnp.exp(tot)
            kt = kap_ref[rows, :] * jnp.exp(cum - lw)
            bt = beta_ref[rows, :] * e_neg
            kk = k_ref[rows, :] * e_neg
            rt = r_ref[rows, :] * jnp.exp(cum)
            lhs += cut(_bf(jnp.concatenate([kt, rt], axis=0)))
            rhs += cut(_bf(jnp.concatenate([bt, kk], axis=0)))
            upd += cut(_bf(jnp.concatenate([kk * e_tot, -(bt * e_tot)], axis=0)))
            v_h += cut(v_ref[rows, :])
            et_h += cut(e_tot)
            s += [s_scr[q, d, h] for h in heads]
            ms += [strict[d]] * RWKV_HEADS
            mi += [incl[d]] * RWKV_HEADS
            rows_d.append(rows)
        chains = range(len(groups) * RWKV_HEADS)
        a1 = [_dg(lhs[c], jnp.concatenate([rhs[c], _bf(s[c])], axis=0), _NT) for c in chains]
        low = [jnp.where(ms[c], a1[c][:c64, 0:c64], 0.0) for c in chains]
        g1 = [jnp.where(ms[c], a1[c][:c64, c64:2 * c64], 0.0) for c in chains]
        corr = _unit_tri_inverse_corr(low)
        z = [a1[c][:c64, 2 * c64:] + _mm(g1[c], v_h[c]) for c in chains]
        u = [z[c] + _mm(corr[c], z[c]) for c in chains]
        vu = [_bf(jnp.concatenate([v_h[c], u[c]], axis=0)) for c in chains]
        g23 = [_bf(jnp.concatenate([jnp.where(mi[c], a1[c][c64:, c64:2 * c64], 0.0),
                                    jnp.where(mi[c], -a1[c][c64:, 0:c64], 0.0)], axis=1)) for c in chains]
        y = [a1[c][c64:, 2 * c64:] + _dg(g23[c], vu[c]) for c in chains]
        for c in chains:
            q, d = groups[c // RWKV_HEADS]
            s_scr[q, d, c % RWKV_HEADS] = s[c] * et_h[c] + _dg(vu[c], upd[c], _TN)
        for gi, (q, d) in enumerate(groups):
            y_ref[d, rows_d[gi], :] = jnp.concatenate(y[gi * RWKV_HEADS:(gi + 1) * RWKV_HEADS], axis=1)
        return carry

    lax.fori_loop(0, n_chunks, body, 0)
    st_ref[...] = s_scr[...]


def _rwkv_scan(r, k, v, kap, beta, lw, s0, layer, seq, nb, row0, nseq):
    rows = nseq * seq
    rb = row0 // rows
    has_state = s0 is not None
    mode = dict(pipeline_mode=pl.Buffered(1)) if rows * RWKV_W * 4 > (2 << 20) else {}
    tok = pl.BlockSpec((rows, RWKV_W), lambda b: (rb + b, 0), **mode)
    in_specs = [tok, tok, tok, tok, tok,
                pl.BlockSpec((2, rows, RWKV_W), lambda b: (0, rb + b, 0), **mode)]
    args = [r, k, v, kap, beta, lw]
    st_shape = (2, RWKV_HEADS, RWKV_HD, RWKV_HD)
    if has_state:
        in_specs.append(pl.BlockSpec((nseq, None) + st_shape, lambda b: (b, layer, 0, 0, 0, 0)))
        args.append(s0)
    return pl.pallas_call(
        functools.partial(_rwkv_scan_kernel, seq=seq, nseq=nseq, has_state=has_state),
        grid=(nb // nseq,),
        in_specs=in_specs,
        out_specs=[
            pl.BlockSpec((2, rows, RWKV_W), lambda b: (0, b, 0), **mode),
            pl.BlockSpec((nseq,) + st_shape, lambda b: (b, 0, 0, 0, 0)),
        ],
        out_shape=[
            jax.ShapeDtypeStruct((2, nb * seq, RWKV_W), F32),
            jax.ShapeDtypeStruct((nb,) + st_shape, F32),
        ],
        scratch_shapes=[pltpu.VMEM((nseq,) + st_shape, F32)],
        compiler_params=_cparams(("arbitrary",)),
        name="rwkv_scan_lat" if has_state else "rwkv_scan_ctx",
    )(*args)


def _rwkv_out(ys, r, k, v, g, rho, lnw, lnb, bd_ref):
    bonus = _segsum(r * k * rho, bd_ref) * v
    y = ys[0] + ys[1] + bonus
    inv_n = 1.0 / RWKV_HD
    yc = y - _segsum(y, bd_ref) * inv_n
    var = _segsum(yc * yc, bd_ref) * inv_n
    yn = yc * lax.rsqrt(var + RWKV_LN_EPS)
    return _bf((yn * lnw + lnb) * g)


def _merge_kernel(oaa_ref, oab_ref, oga_ref, ogb_ref, yra_ref, yrb_ref, r_ref, k_ref, v_ref, g_ref,
                  rho_ref, lnw_ref, lnb_ref, bd_ref, gate_ref, xa_ref, xb_ref, mod_ref, nf_ref,
                  wa_ref, wg_ref, wr_ref, wo_ref, ya_ref, yb_ref, h_ref):
    d = D_MODEL
    tm = h_ref.shape[0]
    ctx = _is_ctx(pl.program_id(0), tm)
    pick = lambda a, b: jnp.where(ctx, a[...], b[...])
    o_rw = _rwkv_out(pick(yra_ref, yrb_ref), r_ref[...], k_ref[...], v_ref[...], g_ref[...],
                     rho_ref[...], lnw_ref[...], lnb_ref[...], bd_ref)
    gate = lambda k: _sigmoid(gate_ref[:, k * d:(k + 1) * d].astype(F32))
    merged = gate(0) * _dg(pick(oaa_ref, oab_ref), wa_ref[...])
    merged += gate(1) * _dg(pick(oga_ref, ogb_ref), wg_ref[...])
    merged += gate(2) * _dg(o_rw, wr_ref[...])
    y = pick(xa_ref, xb_ref) + mod_ref[2:3, :] * _dg(_bf(merged), wo_ref[...])
    h_ref[...] = (_rms(y, nf_ref[...]) * (1.0 + mod_ref[4:5, :]) + mod_ref[3:4, :]).astype(BF16)

    @pl.when(ctx)
    def _():
        ya_ref[...] = y

    @pl.when(jnp.logical_not(ctx))
    def _():
        yb_ref[...] = y


def _merge(o_att, o_gla, y_rw, rwkv_parts, rwkv_vecs, bd, gates, x, mod, norm_ffn, wa, wg, wr, wo, layer):
    tm = 256
    na = N_CTX // tm
    const = lambda r, c: pl.BlockSpec((None, r, c), lambda i: (layer, 0, 0), pipeline_mode=pl.Buffered(1))
    xa, xb = _ctx_lat_specs(tm, D_MODEL)
    tok = pl.BlockSpec((tm, RWKV_W), lambda i: (i, 0))
    vec = pl.BlockSpec((1, RWKV_W), lambda i: (0, 0))
    return pl.pallas_call(
        _merge_kernel,
        grid=(N_TOK // tm,),
        in_specs=[
            *_ctx_lat_specs(tm, ATT_Q_W),
            *_ctx_lat_specs(tm, GLA_W),
            pl.BlockSpec((2, tm, RWKV_W), lambda i: (0, jnp.minimum(i, na - 1), 0)),
            pl.BlockSpec((2, tm, RWKV_W), lambda i: (0, jnp.maximum(i - na, 0), 0)),
            tok, tok, tok, tok, vec, vec, vec,
            pl.BlockSpec((RWKV_W, RWKV_W), lambda i: (0, 0)),
            pl.BlockSpec((tm, GATE_W), lambda i: (i, 0)),
            xa, xb,
            pl.BlockSpec((None, 6, D_MODEL), lambda i: (_mod_row(i, tm), 0, 0)),
            pl.BlockSpec((1, D_MODEL), lambda i: (0, 0)),
            const(ATT_Q_W, D_MODEL), const(GLA_W, D_MODEL), const(RWKV_W, D_MODEL), const(D_MODEL, D_MODEL),
        ],
        out_specs=[xa, xb, pl.BlockSpec((tm, D_MODEL), lambda i: (i, 0))],
        out_shape=[jax.ShapeDtypeStruct((N_CTX, D_MODEL), F32), jax.ShapeDtypeStruct((N_LAT, D_MODEL), F32),
                   jax.ShapeDtypeStruct((N_TOK, D_MODEL), BF16)],
        compiler_params=_cparams(("arbitrary",)),
        name="merge_out",
    )(o_att[0], o_att[1], o_gla[0], o_gla[1], y_rw[0], y_rw[1], *rwkv_parts, *rwkv_vecs, bd, gates,
      x[0], x[1], mod, norm_ffn.reshape(1, D_MODEL), wa, wg, wr, wo)


_FFN_TN = 512
_FFN_SUB = 256
_FFN_RB = 256


def _ffn_up_kernel(*refs, n_q):
    h_ref, hp_ref, hn_ref = refs[:3]
    groups = [refs[3 + k * (1 + n_q):3 + (k + 1) * (1 + n_q)] for k in range(3)]
    (wv_ref, *wg_refs), (cwv_ref, *cwg_refs), (cbv_ref, *cbg_refs) = groups
    o_ref, wv_scr, wg_scr = refs[3 + 3 * (1 + n_q):]
    tm = h_ref.shape[0]
    i = pl.program_id(1)
    lanes = lambda parts: jnp.concatenate([r[...] for r in parts], axis=1)

    @pl.when(i == 0)
    def _():
        wv_scr[...] = wv_ref[...].astype(BF16)
        wg_scr[...] = lanes(wg_refs).astype(BF16)

    cwv = cwv_ref[...]
    cwg = lanes(cwg_refs)
    cbv = cbv_ref[...]
    cbg = lanes(cbg_refs)

    h = h_ref[...]
    hp = hp_ref[...]
    hn = hn_ref[...]
    n_sub = _FFN_TN // _FFN_SUB
    n_rb = tm // _FFN_RB
    edges = [_seq_edges(i * tm + r * _FFN_RB, _FFN_RB) for r in range(n_rb)]

    def up(w_scr, s):
        w = w_scr[:, s * _FFN_SUB:(s + 1) * _FFN_SUB]
        return _dg(h, w), _dg(hp, w)[7:8, :], _dg(hn, w)[0:1, :]

    def conv(us, cw, cb, s, r):
        u_all, u_before, u_after = us
        cols = slice(s * _FFN_SUB, (s + 1) * _FFN_SUB)
        lo, hi = r * _FFN_RB, (r + 1) * _FFN_RB
        u = u_all[lo:hi]
        start, end = edges[r]
        before = jnp.where(start, 0.0, u_before if r == 0 else u_all[lo - 1:lo])
        after = jnp.where(end, 0.0, u_after if r == n_rb - 1 else u_all[hi:hi + 1])
        prev, nxt = _shift_rows(u, before, after)
        return cw[0:1, cols] * prev + cw[1:2, cols] * u + cw[2:3, cols] * nxt + cb[:, cols]

    ups = [(up(wv_scr, s), up(wg_scr, s)) for s in range(n_sub)]
    for s in range(n_sub):
        for r in range(n_rb):
            val = conv(ups[s][0], cwv, cbv, s, r)
            gate = conv(ups[s][1], cwg, cbg, s, r)
            o_ref[r * _FFN_RB:(r + 1) * _FFN_RB, s * _FFN_SUB:(s + 1) * _FFN_SUB] = _bf(gate * _sigmoid(gate) * val)


def _ffn_down_kernel(a_ref, w_ref, xa_ref, xb_ref, mod_ref, ya_ref, yb_ref):
    tm = a_ref.shape[0]
    ctx = _is_ctx(pl.program_id(1), tm)
    y = jnp.where(ctx, xa_ref[...], xb_ref[...]) + mod_ref[5:6, :] * _dg(a_ref[...], w_ref[...])

    @pl.when(ctx)
    def _():
        ya_ref[...] = y

    @pl.when(jnp.logical_not(ctx))
    def _():
        yb_ref[...] = y


def _ffn(h, x, mod, ffn_up, conv_w, conv_b, w_down, layer):
    tm, tn = 1024, _FFN_TN
    nj = pl.cdiv(FFN_HIDDEN, tn)
    hb = tm // 8
    gate0 = FFN_HIDDEN // LANE
    last_blk = 2 * FFN_HIDDEN // LANE - 1
    n_q = tn // LANE
    gate_blk = lambda rows, q: pl.BlockSpec(
        (None, rows, LANE), lambda j, i: (layer, 0, jnp.minimum(gate0 + n_q * j + q, last_blk)))
    val_blk = lambda rows: pl.BlockSpec((None, rows, tn), lambda j, i: (layer, 0, j))
    act = pl.pallas_call(
        functools.partial(_ffn_up_kernel, n_q=n_q),
        grid=(nj, N_TOK // tm),
        in_specs=[
            pl.BlockSpec((tm, D_MODEL), lambda j, i: (i, 0)),
            pl.BlockSpec((8, D_MODEL), lambda j, i: (jnp.maximum(i * hb - 1, 0), 0)),
            pl.BlockSpec((8, D_MODEL), lambda j, i: (jnp.minimum((i + 1) * hb, N_TOK // 8 - 1), 0)),
            val_blk(D_MODEL), *[gate_blk(D_MODEL, q) for q in range(n_q)],
            val_blk(3), *[gate_blk(3, q) for q in range(n_q)],
            val_blk(1), *[gate_blk(1, q) for q in range(n_q)],
        ],
        out_specs=pl.BlockSpec((tm, tn), lambda j, i: (i, j)),
        out_shape=jax.ShapeDtypeStruct((N_TOK, FFN_HIDDEN), BF16),
        scratch_shapes=[pltpu.VMEM((D_MODEL, tn), BF16), pltpu.VMEM((D_MODEL, tn), BF16)],
        compiler_params=_cparams(("arbitrary", "arbitrary")),
        name="ffn_up",
    )(h, h, h, *([ffn_up] * (1 + n_q)), *([conv_w] * (1 + n_q)), *([conv_b] * (1 + n_q)))
    to = 512
    xa, xb = _ctx_lat_specs(tm, to, grid_rank=2)
    return pl.pallas_call(
        _ffn_down_kernel,
        grid=(D_MODEL // to, N_TOK // tm),
        in_specs=[
            pl.BlockSpec((tm, FFN_HIDDEN), lambda n, i: (i, 0)),
            pl.BlockSpec((None, FFN_HIDDEN, to), lambda n, i: (layer, 0, n)),
            xa, xb,
            pl.BlockSpec((None, 6, to), lambda n, i: (_mod_row(i, tm), 0, n)),
        ],
        out_specs=[xa, xb],
        out_shape=[jax.ShapeDtypeStruct((N_CTX, D_MODEL), F32), jax.ShapeDtypeStruct((N_LAT, D_MODEL), F32)],
        compiler_params=_cparams(("arbitrary", "arbitrary")),
        name="ffn_down",
    )(act, w_down, x[0], x[1], mod)


def _pad_cols(x, n):
    return jnp.pad(x, ((0, 0), (0, n - x.shape[1])))


def _pad_rows(x, n, at=0):
    return jnp.pad(x, ((at, n - x.shape[0] - at), (0, 0)))


def _rope_tables(t):
    rows = t // GRID_W
    row = jnp.repeat(jnp.arange(rows, dtype=F32), GRID_W)
    col = jnp.tile(jnp.arange(GRID_W, dtype=F32), rows)
    half = HEAD_DIM // 2
    inv = ROPE_THETA ** (-jnp.arange(0, half, 2, dtype=F32) / half)

    def cos_sin(pos):
        ang = pos[:, None] * inv[None, :]
        ang = jnp.concatenate([ang, ang], axis=-1)
        return jnp.cos(ang), jnp.sin(ang)

    cos_r, sin_r = cos_sin(row)
    cos_c, sin_c = cos_sin(col)
    cos = jnp.concatenate([cos_r, cos_c], axis=-1)
    sin = jnp.concatenate([sin_r, sin_c], axis=-1)
    sign = jnp.where((jnp.arange(HEAD_DIM) % half) < half // 2, -1.0, 1.0).astype(F32)
    return cos, sin * sign[None, :]


def kernel(x_prompt, x_sample, cache_k, cache_v, state_gla, state_rwkv, c, c_ctx, w_mod, b_mod, norm_mix, w_in, q_norm, k_norm, gla_a_up, gla_a_bias, gla_norm, rwkv_mu, rwkv_w0, rwkv_w_up, rwkv_a0, rwkv_a_up, rwkv_g_up, rwkv_k_xi, rwkv_k_alpha, rwkv_bonus, rwkv_ln_w, rwkv_ln_b, w_br_att, w_br_gla, w_br_rwkv, w_out, norm_ffn, ffn_up, ffn_conv_w, ffn_conv_b, ffn_down):
    x = (x_prompt.reshape(N_CTX, D_MODEL), x_sample.reshape(N_LAT, D_MODEL))
    cvec = jnp.concatenate([c_ctx[None, :], c, jnp.zeros((N_MODROWS - 1 - DEC_BATCH, D_MODEL), F32)], axis=0)
    mod_all = _modulation(cvec, w_mod, b_mod).reshape(DEPTH, N_MODROWS, 6, D_MODEL)
    cos, sin = _rope_tables(DEC_SEQ)
    hr = lax.broadcasted_iota(jnp.int32, (RWKV_W, RWKV_W), 0) // RWKV_HD
    hc = lax.broadcasted_iota(jnp.int32, (RWKV_W, RWKV_W), 1) // RWKV_HD
    bd = (hr == hc).astype(BF16)
    wa_all, wg_all, wr_all, wo_all, wd_all = (_bf(w) for w in (w_br_att, w_br_gla, w_br_rwkv, w_out, ffn_down))
    w_in_t = jnp.swapaxes(w_in, 1, 2)

    ks, vs, gla_states, rwkv_states = [], [], [], []
    for l in range(DEPTH):
        mod = mod_all[l]
        h = _prenorm(x, norm_mix[l], mod, 0)
        proj = _proj(h, w_in_t, l, 0, MAIN_W, 1024, "proj_main", BF16)
        p_rw = _proj(h, w_in_t, l, COL_RW, RW_W, 512, "proj_rwkv", BF16)
        gates = _proj(h, w_in_t, l, COL_GATE, GATE_W, GATE_TN, "proj_gates", BF16)

        qn = q_norm[l].reshape(1, HEAD_DIM)
        kn = k_norm[l].reshape(1, HEAD_DIM)
        oa_c, k_c, v_c = _attention_ctx(proj, qn, kn)
        oa_l = _attention_lat(proj, qn, kn, cos, sin, cache_k, cache_v, l)
        ks.append(k_c)
        vs.append(v_c)

        up = gla_a_up[l].reshape(2, GLA_RANK, GLA_HEADS, GLA_DK).transpose(0, 2, 1, 3)
        up_pad = jnp.stack([
            jnp.pad(up[0], ((0, 0), (0, LANE - GLA_RANK), (0, 0))),
            jnp.pad(up[1], ((0, 0), (GLA_RANK, LANE - 2 * GLA_RANK), (0, 0))),
        ])
        gbias = gla_a_bias[l].reshape(2, GLA_HEADS, 1, GLA_DK)
        gnorm = gla_norm[l].reshape(1, GLA_DK)
        og_c, gs_c = _gla(proj, up_pad, gbias, gnorm, None, l, SEQ, BATCH, 0, GLA_HEADS)
        og_l, _ = _gla(proj, up_pad, gbias, gnorm, state_gla, l, DEC_SEQ, DEC_BATCH, N_CTX, 1)
        gla_states.append(gs_c)

        row = lambda a: a.reshape(1, -1)
        mu = _pad_cols(row(rwkv_mu[l]), RW_W)
        wuf = _bf(_pad_rows(rwkv_w_up[l, 0], LANE))
        wub = _bf(_pad_rows(rwkv_w_up[l, 1], LANE, at=RWKV_W_RANK))
        au = _bf(_pad_rows(rwkv_a_up[l], LANE))
        gu = _bf(_pad_rows(rwkv_g_up[l], 2 * LANE, at=RWKV_A_RANK))
        r_, k_, v_, kap_, beta_, lw_, g_ = _rwkv_prep(
            p_rw, mu, wuf, wub, au, gu, rwkv_w0[l], row(rwkv_a0[l]), row(rwkv_k_xi[l]),
            row(rwkv_k_alpha[l]), bd)
        y_c, rs_c = _rwkv_scan(r_, k_, v_, kap_, beta_, lw_, None, l, SEQ, BATCH, 0, 2)
        y_l, _ = _rwkv_scan(r_, k_, v_, kap_, beta_, lw_, state_rwkv, l, DEC_SEQ, DEC_BATCH, N_CTX, 2)
        rwkv_states.append(rs_c)

        xa, xb, h2 = _merge((oa_c, oa_l), (og_c, og_l), (y_c, y_l), (r_, k_, v_, g_),
                            (row(rwkv_bonus[l]), row(rwkv_ln_w[l]), row(rwkv_ln_b[l])), bd, gates, x, mod,
                            norm_ffn[l], wa_all, wg_all, wr_all, wo_all, l)
        x = (xa, xb)
        x = _ffn(h2, x, mod, ffn_up, ffn_conv_w, ffn_conv_b.reshape(DEPTH, 1, 2 * FFN_HIDDEN), wd_all, l)

    y_prompt = x[0].reshape(BATCH, SEQ, D_MODEL)
    y_sample = x[1].reshape(DEC_BATCH, DEC_SEQ, D_MODEL)
    new_cache_k = jnp.stack(ks, axis=1)
    new_cache_v = jnp.stack(vs, axis=1)
    new_state_gla = jnp.stack(gla_states, axis=1)
    new_state_rwkv = jnp.stack(rwkv_states, axis=1)
    return (y_prompt, y_sample, new_cache_k, new_cache_v, new_state_gla, new_state_rwkv)
```

```python
import functools

import jax
import jax.numpy as jnp
from jax import lax
from jax.experimental import pallas as pl
from jax.experimental.pallas import tpu as pltpu

F32 = jnp.float32
BF16 = jnp.bfloat16

D_MODEL = 2048
BATCH = 16
SEQ = 256
DEPTH = 2
DEC_BATCH = 4
DEC_SEQ = 1024
PAST_LEN = 256
GRID_W = 64
HEAD_DIM = 128
ATT_HEADS = 8
ATT_KV_HEADS = 2
ROPE_THETA = 10000.0
GLA_HEADS = 4
GLA_DK = 128
GLA_RANK = 16
GLA_TAU = 16.0
RWKV_HEADS = 8
RWKV_HD = 64
RWKV_W_RANK = 64
RWKV_A_RANK = 64
RWKV_G_RANK = 128
RWKV_DECAY_SCALE = 0.606531
RWKV_LN_EPS = 64e-5
FFN_HIDDEN = 5504
EPS = 1e-6

ATT_Q_W = ATT_HEADS * HEAD_DIM
ATT_KV_W = ATT_KV_HEADS * HEAD_DIM
GLA_W = GLA_HEADS * GLA_DK
RWKV_W = RWKV_HEADS * RWKV_HD
RWKV_COLS = 3 * RWKV_W + 2 * RWKV_W_RANK + RWKV_A_RANK + RWKV_G_RANK
IN_COLS = ATT_Q_W + 2 * ATT_KV_W + 4 * GLA_W + 2 * GLA_RANK + RWKV_COLS + 3 * D_MODEL

LANE = 128
CHUNK = 64
N_CTX = BATCH * SEQ
N_LAT = DEC_BATCH * DEC_SEQ
N_TOK = N_CTX + N_LAT
N_MODROWS = 8
VMEM_LIMIT = 56 * 1024 * 1024
_HALO = 16

COL_GLA = ATT_Q_W + 2 * ATT_KV_W
COL_GAD = COL_GLA + 4 * GLA_W
COL_RW = COL_GAD + 2 * GLA_RANK
COL_GATE = COL_RW + RWKV_COLS
MAIN_W = 4096
RW_W = 2048
GATE_TN = 768
GATE_W = 3 * D_MODEL

_NT = (((1,), (1,)), ((), ()))
_TN = (((0,), (0,)), ((), ()))
_NN = (((1,), (0,)), ((), ()))


def _bf(x):
    return x.astype(BF16)


def _dg(a, b, dims=_NN):
    return lax.dot_general(a, b, dims, preferred_element_type=F32)


def _mm(a, b, dims=_NN):
    return _dg(_bf(a), _bf(b), dims)


def _split3(x):
    h = x.astype(BF16)
    r = x - h.astype(F32)
    m = r.astype(BF16)
    return h, m, (r - m.astype(F32)).astype(BF16)


def _mm_exact_rhs(a, b_bf, dims=_NN):
    h, m, l = _split3(a)
    return _dg(h, b_bf, dims) + (_dg(m, b_bf, dims) + _dg(l, b_bf, dims))


def _sigmoid(x):
    return 0.5 * jnp.tanh(0.5 * x) + 0.5


def _rms(x, g):
    return x * lax.rsqrt(jnp.mean(x * x, axis=-1, keepdims=True) + EPS) * g


def _cparams(sem):
    return pltpu.CompilerParams(dimension_semantics=sem, vmem_limit_bytes=VMEM_LIMIT)


def _mod_row(i, tm):
    n_ctx = N_CTX // tm
    return jnp.where(i < n_ctx, 0, 1 + (i - n_ctx) // (DEC_SEQ // tm))


def _ctx_lat_specs(tm, width, grid_rank=1):
    na = N_CTX // tm
    if grid_rank == 1:
        return (pl.BlockSpec((tm, width), lambda i: (jnp.minimum(i, na - 1), 0)),
                pl.BlockSpec((tm, width), lambda i: (jnp.maximum(i - na, 0), 0)))
    return (pl.BlockSpec((tm, width), lambda n, i: (jnp.minimum(i, na - 1), n)),
            pl.BlockSpec((tm, width), lambda n, i: (jnp.maximum(i - na, 0), n)))


def _is_ctx(i, tm):
    return i < N_CTX // tm


def _seq_edges(g, rows):
    lat = g - N_CTX
    start = (g < N_CTX) | ((lat & (DEC_SEQ - 1)) == 0)
    end = (g < N_CTX) | (((lat + rows) & (DEC_SEQ - 1)) == 0)
    if rows < SEQ:
        start = jnp.where(g < N_CTX, (g & (SEQ - 1)) == 0, start)
        end = jnp.where(g < N_CTX, ((g + rows) & (SEQ - 1)) == 0, end)
    return start, end


def _shift_rows(x, before, after):
    n = x.shape[0]
    row = lax.broadcasted_iota(jnp.int32, (n, 1), 0)
    prev = jnp.where(row == 0, before, pltpu.roll(x, 1, 0))
    nxt = jnp.where(row == n - 1, after, pltpu.roll(x, n - 1, 0))
    return prev, nxt


def _mod_kernel(c_ref, w_ref, b_ref, o_ref):
    c = c_ref[...]
    o_ref[...] = _mm(c * _sigmoid(c), w_ref[...]) + b_ref[...]


def _modulation(cvec, w_mod, b_mod):
    tn = 1024
    n = 6 * D_MODEL
    return pl.pallas_call(
        _mod_kernel,
        grid=(DEPTH, n // tn),
        in_specs=[
            pl.BlockSpec((N_MODROWS, D_MODEL), lambda l, j: (0, 0)),
            pl.BlockSpec((None, D_MODEL, tn), lambda l, j: (l, 0, j)),
            pl.BlockSpec((None, 1, tn), lambda l, j: (l, 0, j)),
        ],
        out_specs=pl.BlockSpec((None, N_MODROWS, tn), lambda l, j: (l, 0, j)),
        out_shape=jax.ShapeDtypeStruct((DEPTH, N_MODROWS, n), F32),
        compiler_params=_cparams(("arbitrary", "arbitrary")),
        name="modulation",
    )(cvec, w_mod, b_mod.reshape(DEPTH, 1, n))


def _prenorm_kernel(xa_ref, xb_ref, g_ref, mod_ref, o_ref, *, shift_idx):
    tm = o_ref.shape[0]
    x = jnp.where(_is_ctx(pl.program_id(0), tm), xa_ref[...], xb_ref[...])
    y = _rms(x, g_ref[...])
    sh = mod_ref[shift_idx:shift_idx + 1, :]
    sc = mod_ref[shift_idx + 1:shift_idx + 2, :]
    o_ref[...] = (y * (1.0 + sc) + sh).astype(BF16)


def _layer_vec(width, layer, grid_rank=1):
    if grid_rank == 1:
        return pl.BlockSpec((None, 1, width), lambda i: (layer, 0, 0))
    return pl.BlockSpec((None, 1, width), lambda a, b: (layer, 0, 0))


def _mod_spec(tm, layer, width=D_MODEL, grid_rank=1):
    if grid_rank == 1:
        return pl.BlockSpec((None, None, 6, width), lambda i: (layer, _mod_row(i, tm), 0, 0))
    return pl.BlockSpec((None, None, 6, width), lambda n, i: (layer, _mod_row(i, tm), 0, n))


def _prenorm(x, g_all, mod_all, layer, shift_idx):
    tm = 512
    xa, xb = _ctx_lat_specs(tm, D_MODEL)
    return pl.pallas_call(
        functools.partial(_prenorm_kernel, shift_idx=shift_idx),
        grid=(N_TOK // tm,),
        in_specs=[xa, xb, _layer_vec(D_MODEL, layer), _mod_spec(tm, layer)],
        out_specs=pl.BlockSpec((tm, D_MODEL), lambda i: (i, 0)),
        out_shape=jax.ShapeDtypeStruct((N_TOK, D_MODEL), BF16),
        compiler_params=_cparams(("arbitrary",)),
        name="prenorm",
    )(x[0], x[1], g_all, mod_all)


def _proj_kernel(*refs, n_w, row_shift):
    h_ref = refs[0]
    w_refs = refs[1:1 + n_w]
    o_ref, w_scr = refs[1 + n_w:]

    @pl.when(pl.program_id(1) == 0)
    def _():
        w = w_refs[0][...] if n_w == 1 else jnp.concatenate([r[...] for r in w_refs], axis=0)
        w_scr[...] = w[row_shift:row_shift + w_scr.shape[0], :].astype(BF16)

    o_ref[...] = _dg(h_ref[...], w_scr[...], _NT).astype(o_ref.dtype)


def _proj(h, w_in_t, layer, col0, width, tn, name, dtype=F32):
    tm = 1024
    off = col0 % tn
    base = col0 - off
    assert width % tn == 0 and off % 8 == 0 and (off == 0 or (tn % off == 0 and IN_COLS % off == 0))
    in_specs = [
        pl.BlockSpec((tm, D_MODEL), lambda j, i: (i, 0)),
        pl.BlockSpec((None, tn, D_MODEL), lambda j, i: (layer, base // tn + j, 0)),
    ]
    if off:
        in_specs.append(pl.BlockSpec((None, off, D_MODEL), lambda j, i: (layer, (base + tn * (j + 1)) // off, 0)))
    return pl.pallas_call(
        functools.partial(_proj_kernel, n_w=len(in_specs) - 1, row_shift=off),
        grid=(width // tn, N_TOK // tm),
        in_specs=in_specs,
        out_specs=pl.BlockSpec((tm, tn), lambda j, i: (i, j)),
        out_shape=jax.ShapeDtypeStruct((N_TOK, width), dtype),
        scratch_shapes=[pltpu.VMEM((tn, D_MODEL), BF16)],
        compiler_params=_cparams(("arbitrary", "arbitrary")),
        name=name,
    )(h, *([w_in_t] * (len(in_specs) - 1)))


_ATT_SCALE = HEAD_DIM ** -0.5
_Q_PER_KV = ATT_HEADS // ATT_KV_HEADS


def _rope(x, cos, sin_signed):
    lane = lax.broadcasted_iota(jnp.int32, x.shape, 1)
    partner = jnp.where((lane & 63) < 32, pltpu.roll(x, 96, 1), pltpu.roll(x, 32, 1))
    return x * cos + partner * sin_signed


def _softmax_pv(s, v_ones):
    p = _bf(jnp.exp(s - jnp.max(s, axis=-1, keepdims=True)))
    oa = _dg(p, v_ones)
    return oa[:, :HEAD_DIM] / oa[:, HEAD_DIM:]


def _attn_ctx_kernel(q_ref, k_ref, v_ref, qn_ref, kn_ref, o_ref, ko_ref, vo_ref):
    k = _rms(k_ref[...].astype(F32), kn_ref[...])
    v = v_ref[...]
    ko_ref[...] = k
    vo_ref[...] = v.astype(F32)
    kb = _bf(k)
    v_ones = jnp.concatenate([v, jnp.ones(v.shape, BF16)], axis=1)
    heads = range(_Q_PER_KV)
    qs = [_bf(_rms(q_ref[:, h * HEAD_DIM:(h + 1) * HEAD_DIM].astype(F32), qn_ref[...])) for h in heads]
    ss = [_dg(qs[h], kb, _NT) * _ATT_SCALE for h in heads]
    o_ref[...] = jnp.concatenate([_softmax_pv(s, v_ones) for s in ss], axis=1).astype(BF16)


def _attn_lat_kernel(q_ref, k_ref, v_ref, qn_ref, kn_ref, cos_ref, sin_ref, ck_ref, cv_ref, o_ref):
    cos = cos_ref[...]
    sin = sin_ref[...]
    k = _rope(_rms(k_ref[...].astype(F32), kn_ref[...]), cos, sin)
    kb = jnp.concatenate([_bf(ck_ref[...]), _bf(k)], axis=0)
    vb = jnp.concatenate([_bf(cv_ref[...]), v_ref[...]], axis=0)
    v_ones = jnp.concatenate([vb, jnp.ones(vb.shape, BF16)], axis=1)
    rb = 256
    blocks = range(DEC_SEQ // rb)

    def scores(h):
        q = q_ref[:, h * HEAD_DIM:(h + 1) * HEAD_DIM].astype(F32)
        q = _bf(_rope(_rms(q, qn_ref[...]), cos, sin))
        return [_dg(q[r * rb:(r + 1) * rb], kb, _NT) * _ATT_SCALE for r in blocks]

    ss_next = scores(0)
    for h in range(_Q_PER_KV):
        ss = ss_next
        if h + 1 < _Q_PER_KV:
            ss_next = scores(h + 1)
        o = jnp.concatenate([_softmax_pv(s, v_ones) for s in ss], axis=0)
        o_ref[:, h * HEAD_DIM:(h + 1) * HEAD_DIM] = o.astype(BF16)


def _attention_ctx(proj, qn, kn, layer):
    blk = lambda w, f: pl.BlockSpec((SEQ, w), f)
    vec = _layer_vec(HEAD_DIM, layer, grid_rank=2)
    kv_out = pl.BlockSpec((None, None, SEQ, HEAD_DIM), lambda b, g: (b, g, 0, 0))
    qw = _Q_PER_KV * HEAD_DIM
    return pl.pallas_call(
        _attn_ctx_kernel,
        grid=(BATCH, ATT_KV_HEADS),
        in_specs=[
            blk(qw, lambda b, g: (b, g)),
            blk(HEAD_DIM, lambda b, g: (b, ATT_HEADS + g)),
            blk(HEAD_DIM, lambda b, g: (b, ATT_HEADS + ATT_KV_HEADS + g)),
            vec, vec,
        ],
        out_specs=[blk(qw, lambda b, g: (b, g)), kv_out, kv_out],
        out_shape=[
            jax.ShapeDtypeStruct((N_CTX, ATT_Q_W), BF16),
            jax.ShapeDtypeStruct((BATCH, ATT_KV_HEADS, SEQ, HEAD_DIM), F32),
            jax.ShapeDtypeStruct((BATCH, ATT_KV_HEADS, SEQ, HEAD_DIM), F32),
        ],
        compiler_params=_cparams(("arbitrary", "arbitrary")),
        name="attention_ctx",
    )(proj, proj, proj, qn, kn)


def _attention_lat(proj, qn, kn, cos, sin, cache_k, cache_v, layer):
    rb = N_CTX // DEC_SEQ
    qw = _Q_PER_KV * HEAD_DIM
    blk = lambda w, f: pl.BlockSpec((DEC_SEQ, w), f)
    vec = _layer_vec(HEAD_DIM, layer, grid_rank=2)
    tab = pl.BlockSpec((DEC_SEQ, HEAD_DIM), lambda b, g: (0, 0))
    cache = pl.BlockSpec((None, None, None, PAST_LEN, HEAD_DIM), lambda b, g: (b, layer, g, 0, 0))
    return pl.pallas_call(
        _attn_lat_kernel,
        grid=(DEC_BATCH, ATT_KV_HEADS),
        in_specs=[
            blk(qw, lambda b, g: (rb + b, g)),
            blk(HEAD_DIM, lambda b, g: (rb + b, ATT_HEADS + g)),
            blk(HEAD_DIM, lambda b, g: (rb + b, ATT_HEADS + ATT_KV_HEADS + g)),
            vec, vec, tab, tab, cache, cache,
        ],
        out_specs=blk(qw, lambda b, g: (b, g)),
        out_shape=jax.ShapeDtypeStruct((N_LAT, ATT_Q_W), BF16),
        compiler_params=_cparams(("arbitrary", "arbitrary")),
        name="attention_lat",
    )(proj, proj, proj, qn, kn, cos, sin, cache_k, cache_v)


def _tri(n, upper):
    r = lax.broadcasted_iota(jnp.int32, (n, n), 0)
    c = lax.broadcasted_iota(jnp.int32, (n, n), 1)
    return (r <= c) if upper else (r >= c)


def _gla_kernel(*refs, seq, has_state):
    if has_state:
        q_ref, k_ref, v_ref, gg_ref, gad_ref, up_ref, bias_ref, gn_ref, s0_ref, o_ref, st_ref = refs
    else:
        q_ref, k_ref, v_ref, gg_ref, gad_ref, up_ref, bias_ref, gn_ref, o_ref, st_ref = refs
        s0_ref = None
    nc = seq // CHUNK
    nh = q_ref.shape[1] // LANE
    c3 = (nc, CHUNK, LANE)
    stack = lambda x: jnp.concatenate([x[:, h * LANE:(h + 1) * LANE].reshape(c3) for h in range(nh)], axis=0)
    gad = gad_ref[...]
    q3 = stack(q_ref[...].astype(F32) * (GLA_DK ** -0.5))
    k3 = stack(k_ref[...].astype(F32))
    v3 = stack(v_ref[...])
    bdot = lambda a, b, ca, cb: lax.dot_general(a, b, (((ca,), (cb,)), ((0,), (0,))), preferred_element_type=F32)

    qe, oi, kv, dec = [], [], [], []
    for d in range(2):
        incl = _tri(CHUNK, d == 1)
        tri = jnp.broadcast_to(jnp.where(incl, 1.0, 0.0).astype(BF16)[None], (nh * nc, CHUNK, CHUNK))
        la = jnp.concatenate(
            [(jax.nn.log_sigmoid(_dg(gad, _bf(up_ref[d, h])) + bias_ref[d, h]) / GLA_TAU).reshape(c3)
             for h in range(nh)], axis=0)
        hi, mid, lo = _split3(la)
        cum = bdot(tri, hi, 2, 1) + (bdot(tri, mid, 2, 1) + bdot(tri, lo, 2, 1))
        tot = cum[:, 0:1, :] if d else cum[:, CHUNK - 1:CHUNK, :]
        qe_d = _bf(q3 * jnp.exp(cum))
        ke = _bf(k3 * jnp.exp(-cum))
        kl = _bf(k3 * jnp.exp(tot - cum))
        att = jnp.where(incl[None], bdot(qe_d, ke, 2, 2), 0.0)
        qe.append(qe_d)
        oi.append(bdot(_bf(att), v3, 2, 1))
        kv.append(bdot(v3, kl, 1, 1))
        dec.append(jnp.exp(tot))

    zero = jnp.zeros((GLA_DK, GLA_DK), F32)
    st = [[s0_ref[d, h].T if has_state else zero for h in range(nh)] for d in range(2)]
    o_f = [[None] * nc for _ in range(nh)]
    o_b = [[None] * nc for _ in range(nh)]
    for t in range(nc):
        nf, nb = t, nc - 1 - t
        for h in range(nh):
            o_f[h][nf] = oi[0][h * nc + nf] + _dg(qe[0][h * nc + nf], _bf(st[0][h]), _NT)
            o_b[h][nb] = oi[1][h * nc + nb] + _dg(qe[1][h * nc + nb], _bf(st[1][h]), _NT)
        for h in range(nh):
            st[0][h] = st[0][h] * dec[0][h * nc + nf] + kv[0][h * nc + nf]
            st[1][h] = st[1][h] * dec[1][h * nc + nb] + kv[1][h * nc + nb]
    outs = []
    for h in range(nh):
        st_ref[0, h] = st[0][h].T
        st_ref[1, h] = st[1][h].T
        outs.append(_rms(jnp.concatenate([a + b for a, b in zip(o_f[h], o_b[h])], axis=0), gn_ref[...]))
    g = gg_ref[...].astype(F32)
    o_ref[...] = (jnp.concatenate(outs, axis=1) * (g * _sigmoid(g))).astype(o_ref.dtype)


def _gla(proj, up_pad, bias, gnorm, s0, layer, seq, nb, row0, nh):
    rb = row0 // seq
    w = nh * LANE
    c0 = COL_GLA // w
    per = GLA_HEADS // nh
    blk = lambda c: pl.BlockSpec((seq, w), lambda b, h: (rb + b, c0 + c * per + h))
    has_state = s0 is not None
    in_specs = [
        blk(0), blk(1), blk(2), blk(3),
        pl.BlockSpec((seq, LANE), lambda b, h: (rb + b, COL_GAD // LANE)),
        pl.BlockSpec((None, 2, nh, LANE, LANE), lambda b, h: (layer, 0, h, 0, 0)),
        pl.BlockSpec((None, 2, nh, 1, LANE), lambda b, h: (layer, 0, h, 0, 0)),
        _layer_vec(LANE, layer, grid_rank=2),
    ]
    args = [proj, proj, proj, proj, proj, up_pad, bias, gnorm]
    if has_state:
        in_specs.append(pl.BlockSpec((None, None, 2, nh, GLA_DK, GLA_DK),
                                     lambda b, h: (b, layer, 0, h, 0, 0)))
        args.append(s0)
    return pl.pallas_call(
        functools.partial(_gla_kernel, seq=seq, has_state=has_state),
        grid=(nb, per),
        in_specs=in_specs,
        out_specs=[
            pl.BlockSpec((seq, w), lambda b, h: (b, h)),
            pl.BlockSpec((None, 2, nh, GLA_DK, GLA_DK), lambda b, h: (b, 0, h, 0, 0)),
        ],
        out_shape=[
            jax.ShapeDtypeStruct((nb * seq, GLA_W), BF16),
            jax.ShapeDtypeStruct((nb, 2, GLA_HEADS, GLA_DK, GLA_DK), F32),
        ],
        compiler_params=_cparams(("arbitrary", "arbitrary")),
        name="gla_lat" if has_state else "gla_ctx",
    )(*args)


def _segsum(x, bd_ref):
    return _mm_exact_rhs(x, bd_ref[...])


def _cumsum(tri_bf, x):
    h, m, l = _split3(x)
    return _dg(tri_bf, h) + (_dg(tri_bf, m) + _dg(tri_bf, l))


def _rwkv_prep_kernel(x_ref, xp_ref, xn_ref, mu_ref, wuf_ref, wub_ref, au_ref,
                      gu_ref, w0_ref, a0_ref, xi_ref, al_ref, bd_ref,
                      r_ref, k_ref, v_ref, kap_ref, beta_ref, lw_ref, g_ref):
    tm = x_ref.shape[0]
    start, end = _seq_edges(pl.program_id(0) * tm, tm)
    x = x_ref[...].astype(F32)
    before = jnp.where(start, 0.0, xp_ref[...].astype(F32)[_HALO - 1:_HALO, :])
    after = jnp.where(end, 0.0, xn_ref[...].astype(F32)[0:1, :])
    prev, nxt = _shift_rows(x, before, after)
    rw = x + (0.5 * (prev + nxt) - x) * mu_ref[...]
    w = RWKV_W
    rr = rw[:, 0:w]
    rk = rw[:, w:2 * w]
    rv = rw[:, 2 * w:3 * w]
    rwd = _bf(jnp.tanh(rw[:, 3 * w:3 * w + LANE]))
    rad = _bf(rw[:, 3 * w + LANE:3 * w + 2 * LANE])
    rgd = _bf(_sigmoid(rw[:, 3 * w + LANE:3 * w + 3 * LANE]))
    lw_ref[0] = -RWKV_DECAY_SCALE * _sigmoid(w0_ref[0:1, :] + _dg(rwd, wuf_ref[...]))
    lw_ref[1] = -RWKV_DECAY_SCALE * _sigmoid(w0_ref[1:2, :] + _dg(rwd, wub_ref[...]))
    a = _sigmoid(a0_ref[...] + _dg(rad, au_ref[...]))
    g_ref[...] = _dg(rgd, gu_ref[...])
    kap = rk * xi_ref[...]
    kap = kap * lax.rsqrt(_segsum(kap * kap, bd_ref) + EPS)
    r_ref[...] = rr
    k_ref[...] = rk * (1.0 + (a - 1.0) * al_ref[...])
    v_ref[...] = rv
    kap_ref[...] = kap
    beta_ref[...] = kap * a


def _rwkv_prep(proj, mu, wuf, wub, au, gu, w0, a0, xi, al, bd, layer):
    tm = 256
    nt = N_TOK // tm
    hb = tm // _HALO
    full = lambda shape: (pl.BlockSpec((None,) + shape, lambda i: (layer,) + (0,) * len(shape)))
    tok = pl.BlockSpec((tm, RWKV_W), lambda i: (i, 0))
    tshape = jax.ShapeDtypeStruct((N_TOK, RWKV_W), F32)
    return pl.pallas_call(
        _rwkv_prep_kernel,
        grid=(nt,),
        in_specs=[
            pl.BlockSpec((tm, RW_W), lambda i: (i, 0)),
            pl.BlockSpec((_HALO, RW_W), lambda i: (jnp.maximum(i * hb - 1, 0), 0)),
            pl.BlockSpec((_HALO, RW_W), lambda i: (jnp.minimum((i + 1) * hb, N_TOK // _HALO - 1), 0)),
            full((1, RW_W)),
            full((LANE, RWKV_W)), full((LANE, RWKV_W)), full((LANE, RWKV_W)), full((2 * LANE, RWKV_W)),
            full((2, RWKV_W)), full((1, RWKV_W)), full((1, RWKV_W)), full((1, RWKV_W)),
            pl.BlockSpec((RWKV_W, RWKV_W), lambda i: (0, 0)),
        ],
        out_specs=[tok, tok, tok, tok, tok,
                   pl.BlockSpec((2, tm, RWKV_W), lambda i: (0, i, 0)), tok],
        out_shape=[tshape, tshape, tshape, tshape, tshape,
                   jax.ShapeDtypeStruct((2, N_TOK, RWKV_W), F32), tshape],
        compiler_params=_cparams(("arbitrary",)),
        name="rwkv_prep",
    )(proj, proj, proj, mu, wuf, wub, au, gu, w0, a0, xi, al, bd)


def _unit_tri_inverse_corr(lows):
    c = [-low for low in lows]
    pb = [_bf(low) for low in lows]
    p = [_dg(b, b) for b in pb]
    span = 2
    while span < CHUNK:
        pb = [_bf(x) for x in p]
        c = [a + x + _dg(_bf(a), xb) for a, x, xb in zip(c, p, pb)]
        span *= 2
        if span < CHUNK:
            p = [_dg(xb, xb) for xb in pb]
    return c


def _rwkv_scan_kernel(*refs, seq, nseq, has_state):
    if has_state:
        r_ref, k_ref, v_ref, kap_ref, beta_ref, lw_ref, s0_ref, y_ref, st_ref, s_scr = refs
    else:
        r_ref, k_ref, v_ref, kap_ref, beta_ref, lw_ref, y_ref, st_ref, s_scr = refs
    n_chunks = seq // CHUNK
    c64 = CHUNK
    rr = lax.broadcasted_iota(jnp.int32, (c64, c64), 0)
    cc = lax.broadcasted_iota(jnp.int32, (c64, c64), 1)
    incl = (rr >= cc, rr <= cc)
    strict = (rr > cc, rr < cc)
    tri = tuple(jnp.where(m, 1.0, 0.0).astype(BF16) for m in incl)
    if has_state:
        s_scr[...] = s0_ref[...]
    else:
        s_scr[...] = jnp.zeros_like(s_scr)
    heads = range(RWKV_HEADS)
    cut = lambda x: [x[:, h * RWKV_HD:(h + 1) * RWKV_HD] for h in heads]
    groups = [(q, d) for q in range(nseq) for d in range(2)]

    def body(i, carry):
        lhs, rhs, upd, v_h, et_h, s, ms, mi, rows_d = [], [], [], [], [], [], [], [], []
        for q, d in groups:
            n = (n_chunks - 1 - i) if d else i
            rows = pl.ds(pl.multiple_of(q * seq + n * c64, c64), c64)
            lw = lw_ref[d, rows, :]
            cum = _cumsum(tri[d], lw)
            tot = cum[0:1, :] if d else cum[c64 - 1:c64, :]
            e_neg = jnp.exp(-cum)
            e_tot = jnp.exp(tot)
            kt = kap_ref[rows, :] * jnp.exp(cum - lw)
            bt = beta_ref[rows, :] * e_neg
            kk = k_ref[rows, :] * e_neg
            rt = r_ref[rows, :] * jnp.exp(cum)
            lhs += cut(_bf(jnp.concatenate([kt, rt], axis=0)))
            rhs += cut(_bf(jnp.concatenate([bt, kk], axis=0)))
            upd += cut(_bf(jnp.concatenate([kk * e_tot, -(bt * e_tot)], axis=0)))
            v_h += cut(v_ref[rows, :])
            et_h += cut(e_tot)
            s += [s_scr[q, d, h] for h in heads]
            ms += [strict[d]] * RWKV_HEADS
            mi += [incl[d]] * RWKV_HEADS
            rows_d.append(rows)
        chains = range(len(groups) * RWKV_HEADS)
        a1 = [_dg(lhs[c], jnp.concatenate([rhs[c], _bf(s[c])], axis=0), _NT) for c in chains]
        low = [jnp.where(ms[c], a1[c][:c64, 0:c64], 0.0) for c in chains]
        g1 = [jnp.where(ms[c], a1[c][:c64, c64:2 * c64], 0.0) for c in chains]
        corr = _unit_tri_inverse_corr(low)
        z = [a1[c][:c64, 2 * c64:] + _mm(g1[c], v_h[c]) for c in chains]
        u = [z[c] + _mm(corr[c], z[c]) for c in chains]
        vu = [_bf(jnp.concatenate([v_h[c], u[c]], axis=0)) for c in chains]
        g23 = [_bf(jnp.concatenate([jnp.where(mi[c], a1[c][c64:, c64:2 * c64], 0.0),
                                    jnp.where(mi[c], -a1[c][c64:, 0:c64], 0.0)], axis=1)) for c in chains]
        y = [a1[c][c64:, 2 * c64:] + _dg(g23[c], vu[c]) for c in chains]
        for c in chains:
            q, d = groups[c // RWKV_HEADS]
            s_scr[q, d, c % RWKV_HEADS] = s[c] * et_h[c] + _dg(vu[c], upd[c], _TN)
        for gi, (q, d) in enumerate(groups):
            y_ref[d, rows_d[gi], :] = jnp.concatenate(y[gi * RWKV_HEADS:(gi + 1) * RWKV_HEADS], axis=1)
        return carry

    lax.fori_loop(0, n_chunks, body, 0)
    st_ref[...] = s_scr[...]


def _rwkv_scan(r, k, v, kap, beta, lw, s0, layer, seq, nb, row0, nseq):
    rows = nseq * seq
    rb = row0 // rows
    has_state = s0 is not None
    mode = dict(pipeline_mode=pl.Buffered(1)) if rows * RWKV_W * 4 > (2 << 20) else {}
    tok = pl.BlockSpec((rows, RWKV_W), lambda b: (rb + b, 0), **mode)
    in_specs = [tok, tok, tok, tok, tok,
                pl.BlockSpec((2, rows, RWKV_W), lambda b: (0, rb + b, 0), **mode)]
    args = [r, k, v, kap, beta, lw]
    st_shape = (2, RWKV_HEADS, RWKV_HD, RWKV_HD)
    if has_state:
        in_specs.append(pl.BlockSpec((nseq, None) + st_shape, lambda b: (b, layer, 0, 0, 0, 0)))
        args.append(s0)
    return pl.pallas_call(
        functools.partial(_rwkv_scan_kernel, seq=seq, nseq=nseq, has_state=has_state),
        grid=(nb // nseq,),
        in_specs=in_specs,
        out_specs=[
            pl.BlockSpec((2, rows, RWKV_W), lambda b: (0, b, 0), **mode),
            pl.BlockSpec((nseq,) + st_shape, lambda b: (b, 0, 0, 0, 0)),
        ],
        out_shape=[
            jax.ShapeDtypeStruct((2, nb * seq, RWKV_W), F32),
            jax.ShapeDtypeStruct((nb,) + st_shape, F32),
        ],
        scratch_shapes=[pltpu.VMEM((nseq,) + st_shape, F32)],
        compiler_params=_cparams(("arbitrary",)),
        name="rwkv_scan_lat" if has_state else "rwkv_scan_ctx",
    )(*args)


def _rwkv_out(ys, r, k, v, g, rho, lnw, lnb, bd_ref):
    bonus = _segsum(r * k * rho, bd_ref) * v
    y = ys[0] + ys[1] + bonus
    inv_n = 1.0 / RWKV_HD
    yc = y - _segsum(y, bd_ref) * inv_n
    var = _segsum(yc * yc, bd_ref) * inv_n
    yn = yc * lax.rsqrt(var + RWKV_LN_EPS)
    return _bf((yn * lnw + lnb) * g)


def _merge_kernel(oaa_ref, oab_ref, oga_ref, ogb_ref, yra_ref, yrb_ref, r_ref, k_ref, v_ref, g_ref,
                  rho_ref, lnw_ref, lnb_ref, bd_ref, gate_ref, xa_ref, xb_ref, mod_ref, nf_ref,
                  wa_ref, wg_ref, wr_ref, wo_ref, ya_ref, yb_ref, h_ref):
    d = D_MODEL
    tm = h_ref.shape[0]
    ctx = _is_ctx(pl.program_id(0), tm)
    pick = lambda a, b: jnp.where(ctx, a[...], b[...])
    o_rw = _rwkv_out(pick(yra_ref, yrb_ref), r_ref[...], k_ref[...], v_ref[...], g_ref[...],
                     rho_ref[...], lnw_ref[...], lnb_ref[...], bd_ref)
    gate = lambda k: _sigmoid(gate_ref[:, k * d:(k + 1) * d].astype(F32))
    merged = gate(0) * _dg(pick(oaa_ref, oab_ref), wa_ref[...])
    merged += gate(1) * _dg(pick(oga_ref, ogb_ref), wg_ref[...])
    merged += gate(2) * _dg(o_rw, wr_ref[...])
    y = pick(xa_ref, xb_ref) + mod_ref[2:3, :] * _dg(_bf(merged), wo_ref[...])
    h_ref[...] = (_rms(y, nf_ref[...]) * (1.0 + mod_ref[4:5, :]) + mod_ref[3:4, :]).astype(BF16)

    @pl.when(ctx)
    def _():
        ya_ref[...] = y

    @pl.when(jnp.logical_not(ctx))
    def _():
        yb_ref[...] = y


def _merge(o_att, o_gla, y_rw, rwkv_parts, rwkv_vecs, bd, gates, x, mod, norm_ffn, wa, wg, wr, wo, layer):
    tm = 256
    na = N_CTX // tm
    const = lambda r, c: pl.BlockSpec((None, r, c), lambda i: (layer, 0, 0), pipeline_mode=pl.Buffered(1))
    xa, xb = _ctx_lat_specs(tm, D_MODEL)
    tok = pl.BlockSpec((tm, RWKV_W), lambda i: (i, 0))
    vec = _layer_vec(RWKV_W, layer)
    return pl.pallas_call(
        _merge_kernel,
        grid=(N_TOK // tm,),
        in_specs=[
            *_ctx_lat_specs(tm, ATT_Q_W),
            *_ctx_lat_specs(tm, GLA_W),
            pl.BlockSpec((2, tm, RWKV_W), lambda i: (0, jnp.minimum(i, na - 1), 0)),
            pl.BlockSpec((2, tm, RWKV_W), lambda i: (0, jnp.maximum(i - na, 0), 0)),
            tok, tok, tok, tok, vec, vec, vec,
            pl.BlockSpec((RWKV_W, RWKV_W), lambda i: (0, 0)),
            pl.BlockSpec((tm, GATE_W), lambda i: (i, 0)),
            xa, xb,
            _mod_spec(tm, layer),
            _layer_vec(D_MODEL, layer),
            const(ATT_Q_W, D_MODEL), const(GLA_W, D_MODEL), const(RWKV_W, D_MODEL), const(D_MODEL, D_MODEL),
        ],
        out_specs=[xa, xb, pl.BlockSpec((tm, D_MODEL), lambda i: (i, 0))],
        out_shape=[jax.ShapeDtypeStruct((N_CTX, D_MODEL), F32), jax.ShapeDtypeStruct((N_LAT, D_MODEL), F32),
                   jax.ShapeDtypeStruct((N_TOK, D_MODEL), BF16)],
        compiler_params=_cparams(("arbitrary",)),
        name="merge_out",
    )(o_att[0], o_att[1], o_gla[0], o_gla[1], y_rw[0], y_rw[1], *rwkv_parts, *rwkv_vecs, bd, gates,
      x[0], x[1], mod, norm_ffn, wa, wg, wr, wo)


_FFN_TN = 512
_FFN_SUB = 256
_FFN_RB = 256


def _ffn_up_kernel(*refs, n_q):
    h_ref, hp_ref, hn_ref = refs[:3]
    groups = [refs[3 + k * (1 + n_q):3 + (k + 1) * (1 + n_q)] for k in range(3)]
    (wv_ref, *wg_refs), (cwv_ref, *cwg_refs), (cbv_ref, *cbg_refs) = groups
    o_ref, wv_scr, wg_scr = refs[3 + 3 * (1 + n_q):]
    tm = h_ref.shape[0]
    i = pl.program_id(1)
    lanes = lambda parts: jnp.concatenate([r[...] for r in parts], axis=1)

    @pl.when(i == 0)
    def _():
        wv_scr[...] = wv_ref[...].astype(BF16)
        wg_scr[...] = lanes(wg_refs).astype(BF16)

    cwv = cwv_ref[...]
    cwg = lanes(cwg_refs)
    cbv = cbv_ref[...]
    cbg = lanes(cbg_refs)

    h = h_ref[...]
    hp = hp_ref[...]
    hn = hn_ref[...]
    n_sub = _FFN_TN // _FFN_SUB
    n_rb = tm // _FFN_RB
    edges = [_seq_edges(i * tm + r * _FFN_RB, _FFN_RB) for r in range(n_rb)]

    def up(w_scr, s):
        w = w_scr[:, s * _FFN_SUB:(s + 1) * _FFN_SUB]
        return _dg(h, w), _dg(hp, w)[7:8, :], _dg(hn, w)[0:1, :]

    def conv(us, cw, cb, s, r):
        u_all, u_before, u_after = us
        cols = slice(s * _FFN_SUB, (s + 1) * _FFN_SUB)
        lo, hi = r * _FFN_RB, (r + 1) * _FFN_RB
        u = u_all[lo:hi]
        start, end = edges[r]
        before = jnp.where(start, 0.0, u_before if r == 0 else u_all[lo - 1:lo])
        after = jnp.where(end, 0.0, u_after if r == n_rb - 1 else u_all[hi:hi + 1])
        prev, nxt = _shift_rows(u, before, after)
        return cw[0:1, cols] * prev + cw[1:2, cols] * u + cw[2:3, cols] * nxt + cb[:, cols]

    ups = [(up(wv_scr, s), up(wg_scr, s)) for s in range(n_sub)]
    for s in range(n_sub):
        for r in range(n_rb):
            val = conv(ups[s][0], cwv, cbv, s, r)
            gate = conv(ups[s][1], cwg, cbg, s, r)
            o_ref[r * _FFN_RB:(r + 1) * _FFN_RB, s * _FFN_SUB:(s + 1) * _FFN_SUB] = _bf(gate * _sigmoid(gate) * val)


def _ffn_down_kernel(a_ref, w_ref, xa_ref, xb_ref, mod_ref, ya_ref, yb_ref):
    tm = a_ref.shape[0]
    ctx = _is_ctx(pl.program_id(1), tm)
    y = jnp.where(ctx, xa_ref[...], xb_ref[...]) + mod_ref[5:6, :] * _dg(a_ref[...], w_ref[...])

    @pl.when(ctx)
    def _():
        ya_ref[...] = y

    @pl.when(jnp.logical_not(ctx))
    def _():
        yb_ref[...] = y


def _ffn(h, x, mod, ffn_up, conv_w, conv_b, w_down, layer):
    tm, tn = 1024, _FFN_TN
    tu = tm
    nj = pl.cdiv(FFN_HIDDEN, tn)
    hb = tu // 8
    gate0 = FFN_HIDDEN // LANE
    last_blk = 2 * FFN_HIDDEN // LANE - 1
    n_q = tn // LANE
    gate_blk = lambda rows, q: pl.BlockSpec(
        (None, rows, LANE), lambda j, i: (layer, 0, jnp.minimum(gate0 + n_q * j + q, last_blk)))
    val_blk = lambda rows: pl.BlockSpec((None, rows, tn), lambda j, i: (layer, 0, j))
    act = pl.pallas_call(
        functools.partial(_ffn_up_kernel, n_q=n_q),
        grid=(nj, N_TOK // tu),
        in_specs=[
            pl.BlockSpec((tu, D_MODEL), lambda j, i: (i, 0)),
            pl.BlockSpec((8, D_MODEL), lambda j, i: (jnp.maximum(i * hb - 1, 0), 0)),
            pl.BlockSpec((8, D_MODEL), lambda j, i: (jnp.minimum((i + 1) * hb, N_TOK // 8 - 1), 0)),
            val_blk(D_MODEL), *[gate_blk(D_MODEL, q) for q in range(n_q)],
            val_blk(3), *[gate_blk(3, q) for q in range(n_q)],
            val_blk(1), *[gate_blk(1, q) for q in range(n_q)],
        ],
        out_specs=pl.BlockSpec((tu, tn), lambda j, i: (i, j)),
        out_shape=jax.ShapeDtypeStruct((N_TOK, FFN_HIDDEN), BF16),
        scratch_shapes=[pltpu.VMEM((D_MODEL, tn), BF16), pltpu.VMEM((D_MODEL, tn), BF16)],
        compiler_params=_cparams(("arbitrary", "arbitrary")),
        name="ffn_up",
    )(h, h, h, *([ffn_up] * (1 + n_q)), *([conv_w] * (1 + n_q)), *([conv_b] * (1 + n_q)))
    to = 512
    xa, xb = _ctx_lat_specs(tm, to, grid_rank=2)
    return pl.pallas_call(
        _ffn_down_kernel,
        grid=(D_MODEL // to, N_TOK // tm),
        in_specs=[
            pl.BlockSpec((tm, FFN_HIDDEN), lambda n, i: (i, 0)),
            pl.BlockSpec((None, FFN_HIDDEN, to), lambda n, i: (layer, 0, n)),
            xa, xb,
            _mod_spec(tm, layer, width=to, grid_rank=2),
        ],
        out_specs=[xa, xb],
        out_shape=[jax.ShapeDtypeStruct((N_CTX, D_MODEL), F32), jax.ShapeDtypeStruct((N_LAT, D_MODEL), F32)],
        compiler_params=_cparams(("arbitrary", "arbitrary")),
        name="ffn_down",
    )(act, w_down, x[0], x[1], mod)


def _rope_tables(t):
    rows = t // GRID_W
    row = jnp.repeat(jnp.arange(rows, dtype=F32), GRID_W)
    col = jnp.tile(jnp.arange(GRID_W, dtype=F32), rows)
    half = HEAD_DIM // 2
    inv = ROPE_THETA ** (-jnp.arange(0, half, 2, dtype=F32) / half)

    def cos_sin(pos):
        ang = pos[:, None] * inv[None, :]
        ang = jnp.concatenate([ang, ang], axis=-1)
        return jnp.cos(ang), jnp.sin(ang)

    cos_r, sin_r = cos_sin(row)
    cos_c, sin_c = cos_sin(col)
    cos = jnp.concatenate([cos_r, cos_c], axis=-1)
    sin = jnp.concatenate([sin_r, sin_c], axis=-1)
    sign = jnp.where((jnp.arange(HEAD_DIM) % half) < half // 2, -1.0, 1.0).astype(F32)
    return cos, sin * sign[None, :]


def kernel(x_prompt, x_sample, cache_k, cache_v, state_gla, state_rwkv, c, c_ctx, w_mod, b_mod, norm_mix, w_in, q_norm, k_norm, gla_a_up, gla_a_bias, gla_norm, rwkv_mu, rwkv_w0, rwkv_w_up, rwkv_a0, rwkv_a_up, rwkv_g_up, rwkv_k_xi, rwkv_k_alpha, rwkv_bonus, rwkv_ln_w, rwkv_ln_b, w_br_att, w_br_gla, w_br_rwkv, w_out, norm_ffn, ffn_up, ffn_conv_w, ffn_conv_b, ffn_down):
    x = (x_prompt.reshape(N_CTX, D_MODEL), x_sample.reshape(N_LAT, D_MODEL))
    cvec = jnp.concatenate([c_ctx[None, :], c, jnp.zeros((N_MODROWS - 1 - DEC_BATCH, D_MODEL), F32)], axis=0)
    mod_all = _modulation(cvec, w_mod, b_mod).reshape(DEPTH, N_MODROWS, 6, D_MODEL)
    cos, sin = _rope_tables(DEC_SEQ)
    hr = lax.broadcasted_iota(jnp.int32, (RWKV_W, RWKV_W), 0) // RWKV_HD
    hc = lax.broadcasted_iota(jnp.int32, (RWKV_W, RWKV_W), 1) // RWKV_HD
    bd = (hr == hc).astype(BF16)
    wa_all, wg_all, wr_all, wo_all, wd_all = (_bf(w) for w in (w_br_att, w_br_gla, w_br_rwkv, w_out, ffn_down))
    w_in_t = jnp.swapaxes(w_in, 1, 2)

    vecs = lambda a: a.reshape(DEPTH, 1, -1)
    pad_rows = lambda a, n, at=0: jnp.pad(a, ((0, 0), (at, n - a.shape[1] - at), (0, 0)))
    qn, kn, gnorm, norm_mix3, norm_ffn3 = (vecs(a) for a in (q_norm, k_norm, gla_norm, norm_mix, norm_ffn))
    up = gla_a_up.reshape(DEPTH, 2, GLA_RANK, GLA_HEADS, GLA_DK).transpose(0, 1, 3, 2, 4)
    up_pad = jnp.stack([
        jnp.pad(up[:, 0], ((0, 0), (0, 0), (0, LANE - GLA_RANK), (0, 0))),
        jnp.pad(up[:, 1], ((0, 0), (0, 0), (GLA_RANK, LANE - 2 * GLA_RANK), (0, 0))),
    ], axis=1)
    gbias = gla_a_bias.reshape(DEPTH, 2, GLA_HEADS, 1, GLA_DK)
    mu = vecs(jnp.pad(rwkv_mu, ((0, 0), (0, RW_W - RWKV_COLS))))
    wuf = _bf(pad_rows(rwkv_w_up[:, 0], LANE))
    wub = _bf(pad_rows(rwkv_w_up[:, 1], LANE, at=RWKV_W_RANK))
    au = _bf(pad_rows(rwkv_a_up, LANE))
    gu = _bf(pad_rows(rwkv_g_up, 2 * LANE, at=RWKV_A_RANK))
    a0, xi, al, rho, lnw, lnb = (vecs(a) for a in (rwkv_a0, rwkv_k_xi, rwkv_k_alpha, rwkv_bonus, rwkv_ln_w, rwkv_ln_b))
    conv_b = vecs(ffn_conv_b)

    ks, vs, gla_states, rwkv_states = [], [], [], []
    for l in range(DEPTH):
        h = _prenorm(x, norm_mix3, mod_all, l, 0)
        proj = _proj(h, w_in_t, l, 0, MAIN_W, 1024, "proj_main", BF16)
        p_rw = _proj(h, w_in_t, l, COL_RW, RW_W, 512, "proj_rwkv", BF16)
        gates = _proj(h, w_in_t, l, COL_GATE, GATE_W, GATE_TN, "proj_gates", BF16)

        oa_c, k_c, v_c = _attention_ctx(proj, qn, kn, l)
        oa_l = _attention_lat(proj, qn, kn, cos, sin, cache_k, cache_v, l)
        ks.append(k_c)
        vs.append(v_c)

        og_c, gs_c = _gla(proj, up_pad, gbias, gnorm, None, l, SEQ, BATCH, 0, GLA_HEADS)
        og_l, _ = _gla(proj, up_pad, gbias, gnorm, state_gla, l, DEC_SEQ, DEC_BATCH, N_CTX, 2)
        gla_states.append(gs_c)

        r_, k_, v_, kap_, beta_, lw_, g_ = _rwkv_prep(p_rw, mu, wuf, wub, au, gu, rwkv_w0, a0, xi, al, bd, l)
        y_c, rs_c = _rwkv_scan(r_, k_, v_, kap_, beta_, lw_, None, l, SEQ, BATCH, 0, 2)
        y_l, _ = _rwkv_scan(r_, k_, v_, kap_, beta_, lw_, state_rwkv, l, DEC_SEQ, DEC_BATCH, N_CTX, 2)
        rwkv_states.append(rs_c)

        xa, xb, h2 = _merge((oa_c, oa_l), (og_c, og_l), (y_c, y_l), (r_, k_, v_, g_), (rho, lnw, lnb), bd,
                            gates, x, mod_all, norm_ffn3, wa_all, wg_all, wr_all, wo_all, l)
        x = _ffn(h2, (xa, xb), mod_all, ffn_up, ffn_conv_w, conv_b, wd_all, l)

    y_prompt = x[0].reshape(BATCH, SEQ, D_MODEL)
    y_sample = x[1].reshape(DEC_BATCH, DEC_SEQ, D_MODEL)
    new_cache_k = jnp.stack(ks, axis=1)
    new_cache_v = jnp.stack(vs, axis=1)
    new_state_gla = jnp.stack(gla_states, axis=1)
    new_state_rwkv = jnp.stack(rwkv_states, axis=1)
    return (y_prompt, y_sample, new_cache_k, new_cache_v, new_state_gla, new_state_rwkv)
```

```python
import functools

import jax
import jax.numpy as jnp
from jax import lax
from jax.experimental import pallas as pl
from jax.experimental.pallas import tpu as pltpu

F32 = jnp.float32
BF16 = jnp.bfloat16

D_MODEL = 2048
BATCH = 16
SEQ = 256
DEPTH = 2
DEC_BATCH = 4
DEC_SEQ = 1024
PAST_LEN = 256
GRID_W = 64
HEAD_DIM = 128
ATT_HEADS = 8
ATT_KV_HEADS = 2
ROPE_THETA = 10000.0
GLA_HEADS = 4
GLA_DK = 128
GLA_RANK = 16
GLA_TAU = 16.0
RWKV_HEADS = 8
RWKV_HD = 64
RWKV_W_RANK = 64
RWKV_A_RANK = 64
RWKV_G_RANK = 128
RWKV_DECAY_SCALE = 0.606531
RWKV_LN_EPS = 64e-5
FFN_HIDDEN = 5504
EPS = 1e-6

ATT_Q_W = ATT_HEADS * HEAD_DIM
ATT_KV_W = ATT_KV_HEADS * HEAD_DIM
GLA_W = GLA_HEADS * GLA_DK
RWKV_W = RWKV_HEADS * RWKV_HD
RWKV_COLS = 3 * RWKV_W + 2 * RWKV_W_RANK + RWKV_A_RANK + RWKV_G_RANK
IN_COLS = ATT_Q_W + 2 * ATT_KV_W + 4 * GLA_W + 2 * GLA_RANK + RWKV_COLS + 3 * D_MODEL

LANE = 128
CHUNK = 64
N_CTX = BATCH * SEQ
N_LAT = DEC_BATCH * DEC_SEQ
N_TOK = N_CTX + N_LAT
N_MODROWS = 8
VMEM_LIMIT = 56 * 1024 * 1024
_HALO = 16

COL_GLA = ATT_Q_W + 2 * ATT_KV_W
COL_GAD = COL_GLA + 4 * GLA_W
COL_RW = COL_GAD + 2 * GLA_RANK
COL_GATE = COL_RW + RWKV_COLS
MAIN_W = 4096
RW_W = 2048
GATE_TN = 768
GATE_W = 3 * D_MODEL

_NT = (((1,), (1,)), ((), ()))
_TN = (((0,), (0,)), ((), ()))
_NN = (((1,), (0,)), ((), ()))


def _bf(x):
    return x.astype(BF16)


def _dg(a, b, dims=_NN):
    return lax.dot_general(a, b, dims, preferred_element_type=F32)


def _mm(a, b, dims=_NN):
    return _dg(_bf(a), _bf(b), dims)


def _split3(x):
    h = x.astype(BF16)
    r = x - h.astype(F32)
    m = r.astype(BF16)
    return h, m, (r - m.astype(F32)).astype(BF16)


def _mm_exact_rhs(a, b_bf, dims=_NN):
    h, m, l = _split3(a)
    return _dg(h, b_bf, dims) + (_dg(m, b_bf, dims) + _dg(l, b_bf, dims))


def _sigmoid(x):
    return 0.5 * jnp.tanh(0.5 * x) + 0.5


def _rms(x, g):
    return x * lax.rsqrt(jnp.mean(x * x, axis=-1, keepdims=True) + EPS) * g


def _cparams(sem):
    return pltpu.CompilerParams(dimension_semantics=sem, vmem_limit_bytes=VMEM_LIMIT)


def _mod_row(i, tm):
    n_ctx = N_CTX // tm
    return jnp.where(i < n_ctx, 0, 1 + (i - n_ctx) // (DEC_SEQ // tm))


def _ctx_lat_specs(tm, width, grid_rank=1):
    na = N_CTX // tm
    if grid_rank == 1:
        return (pl.BlockSpec((tm, width), lambda i: (jnp.minimum(i, na - 1), 0)),
                pl.BlockSpec((tm, width), lambda i: (jnp.maximum(i - na, 0), 0)))
    return (pl.BlockSpec((tm, width), lambda n, i: (jnp.minimum(i, na - 1), n)),
            pl.BlockSpec((tm, width), lambda n, i: (jnp.maximum(i - na, 0), n)))


def _is_ctx(i, tm):
    return i < N_CTX // tm


def _seq_edges(g, rows):
    lat = g - N_CTX
    start = (g < N_CTX) | ((lat & (DEC_SEQ - 1)) == 0)
    end = (g < N_CTX) | (((lat + rows) & (DEC_SEQ - 1)) == 0)
    if rows < SEQ:
        start = jnp.where(g < N_CTX, (g & (SEQ - 1)) == 0, start)
        end = jnp.where(g < N_CTX, ((g + rows) & (SEQ - 1)) == 0, end)
    return start, end


def _shift_rows(x, before, after):
    n = x.shape[0]
    row = lax.broadcasted_iota(jnp.int32, (n, 1), 0)
    prev = jnp.where(row == 0, before, pltpu.roll(x, 1, 0))
    nxt = jnp.where(row == n - 1, after, pltpu.roll(x, n - 1, 0))
    return prev, nxt


def _mod_kernel(c_ref, w_ref, b_ref, o_ref):
    c = c_ref[...]
    o_ref[...] = _mm(c * _sigmoid(c), w_ref[...]) + b_ref[...]


def _modulation(cvec, w_mod, b_mod):
    tn = 1024
    n = 6 * D_MODEL
    return pl.pallas_call(
        _mod_kernel,
        grid=(DEPTH, n // tn),
        in_specs=[
            pl.BlockSpec((N_MODROWS, D_MODEL), lambda l, j: (0, 0)),
            pl.BlockSpec((None, D_MODEL, tn), lambda l, j: (l, 0, j)),
            pl.BlockSpec((None, 1, tn), lambda l, j: (l, 0, j)),
        ],
        out_specs=pl.BlockSpec((None, N_MODROWS, tn), lambda l, j: (l, 0, j)),
        out_shape=jax.ShapeDtypeStruct((DEPTH, N_MODROWS, n), F32),
        compiler_params=_cparams(("arbitrary", "arbitrary")),
        name="modulation",
    )(cvec, w_mod, b_mod.reshape(DEPTH, 1, n))


def _prenorm_kernel(xa_ref, xb_ref, g_ref, mod_ref, o_ref, *, shift_idx):
    tm = o_ref.shape[0]
    x = jnp.where(_is_ctx(pl.program_id(0), tm), xa_ref[...], xb_ref[...])
    y = _rms(x, g_ref[...])
    sh = mod_ref[shift_idx:shift_idx + 1, :]
    sc = mod_ref[shift_idx + 1:shift_idx + 2, :]
    o_ref[...] = (y * (1.0 + sc) + sh).astype(BF16)


def _layer_vec(width, layer, grid_rank=1):
    if grid_rank == 1:
        return pl.BlockSpec((None, 1, width), lambda i: (layer, 0, 0))
    return pl.BlockSpec((None, 1, width), lambda a, b: (layer, 0, 0))


def _mod_spec(tm, layer, width=D_MODEL, grid_rank=1):
    if grid_rank == 1:
        return pl.BlockSpec((None, None, 6, width), lambda i: (layer, _mod_row(i, tm), 0, 0))
    return pl.BlockSpec((None, None, 6, width), lambda n, i: (layer, _mod_row(i, tm), 0, n))


def _prenorm(x, g_all, mod_all, layer, shift_idx):
    tm = 512
    xa, xb = _ctx_lat_specs(tm, D_MODEL)
    return pl.pallas_call(
        functools.partial(_prenorm_kernel, shift_idx=shift_idx),
        grid=(N_TOK // tm,),
        in_specs=[xa, xb, _layer_vec(D_MODEL, layer), _mod_spec(tm, layer)],
        out_specs=pl.BlockSpec((tm, D_MODEL), lambda i: (i, 0)),
        out_shape=jax.ShapeDtypeStruct((N_TOK, D_MODEL), BF16),
        compiler_params=_cparams(("arbitrary",)),
        name="prenorm",
    )(x[0], x[1], g_all, mod_all)


_CAST_ROWS = 128


def _proj_kernel(*refs, n_w, n_cast, row_shift):
    h_ref = refs[0]
    w_refs = refs[1:1 + n_w]
    cast_in = refs[1 + n_w:1 + n_w + n_cast]
    o_ref = refs[1 + n_w + n_cast]
    cast_out = refs[2 + n_w + n_cast:2 + n_w + 2 * n_cast]
    w_scr = refs[-1]

    @pl.when(pl.program_id(1) == 0)
    def _():
        w = w_refs[0][...] if n_w == 1 else jnp.concatenate([r[...] for r in w_refs], axis=0)
        w_scr[...] = w[row_shift:row_shift + w_scr.shape[0], :].astype(BF16)

    o_ref[...] = _dg(h_ref[...], w_scr[...], _NT).astype(o_ref.dtype)
    for src, dst in zip(cast_in, cast_out):
        dst[...] = src[...].astype(BF16)


def _proj(h, w_in_t, layer, col0, width, tn, name, dtype=F32, casts=()):
    tm = 1024
    off = col0 % tn
    base = col0 - off
    assert width % tn == 0 and off % 8 == 0 and (off == 0 or (tn % off == 0 and IN_COLS % off == 0))
    grid = (width // tn, N_TOK // tm)
    in_specs = [
        pl.BlockSpec((tm, D_MODEL), lambda j, i: (i, 0)),
        pl.BlockSpec((None, tn, D_MODEL), lambda j, i: (layer, base // tn + j, 0)),
    ]
    if off:
        in_specs.append(pl.BlockSpec((None, off, D_MODEL), lambda j, i: (layer, (base + tn * (j + 1)) // off, 0)))
    n_w = len(in_specs) - 1
    out_specs = [pl.BlockSpec((tm, tn), lambda j, i: (i, j))]
    out_shape = [jax.ShapeDtypeStruct((N_TOK, width), dtype)]
    for w in casts:
        rows, cols = w.shape[1:]
        last = rows // _CAST_ROWS - 1
        assert rows % _CAST_ROWS == 0 and last < grid[0] * grid[1]
        step = lambda j, i, last=last: jnp.minimum(j * grid[1] + i, last)
        in_specs.append(pl.BlockSpec((None, _CAST_ROWS, cols), lambda j, i, step=step: (layer, step(j, i), 0)))
        out_specs.append(pl.BlockSpec((_CAST_ROWS, cols), lambda j, i, step=step: (step(j, i), 0)))
        out_shape.append(jax.ShapeDtypeStruct((rows, cols), BF16))
    outs = pl.pallas_call(
        functools.partial(_proj_kernel, n_w=n_w, n_cast=len(casts), row_shift=off),
        grid=grid,
        in_specs=in_specs,
        out_specs=out_specs,
        out_shape=out_shape,
        scratch_shapes=[pltpu.VMEM((tn, D_MODEL), BF16)],
        compiler_params=_cparams(("arbitrary", "arbitrary")),
        name=name,
    )(h, *([w_in_t] * n_w), *casts)
    return outs[0] if not casts else outs


_ATT_SCALE = HEAD_DIM ** -0.5
_Q_PER_KV = ATT_HEADS // ATT_KV_HEADS


def _rope(x, cos, sin_signed):
    lane = lax.broadcasted_iota(jnp.int32, x.shape, 1)
    partner = jnp.where((lane & 63) < 32, pltpu.roll(x, 96, 1), pltpu.roll(x, 32, 1))
    return x * cos + partner * sin_signed


def _softmax_pv(s, v_ones):
    p = _bf(jnp.exp(s - jnp.max(s, axis=-1, keepdims=True)))
    oa = _dg(p, v_ones)
    return oa[:, :HEAD_DIM] / oa[:, HEAD_DIM:]


def _attn_ctx_kernel(q_ref, k_ref, v_ref, qn_ref, kn_ref, o_ref, ko_ref, vo_ref):
    k = _rms(k_ref[...].astype(F32), kn_ref[...])
    v = v_ref[...]
    ko_ref[...] = k
    vo_ref[...] = v.astype(F32)
    kb = _bf(k)
    v_ones = jnp.concatenate([v, jnp.ones(v.shape, BF16)], axis=1)
    heads = range(_Q_PER_KV)
    qs = [_bf(_rms(q_ref[:, h * HEAD_DIM:(h + 1) * HEAD_DIM].astype(F32), qn_ref[...])) for h in heads]
    ss = [_dg(qs[h], kb, _NT) * _ATT_SCALE for h in heads]
    o_ref[...] = jnp.concatenate([_softmax_pv(s, v_ones) for s in ss], axis=1).astype(BF16)


def _attn_lat_kernel(q_ref, k_ref, v_ref, qn_ref, kn_ref, cos_ref, sin_ref, ck_ref, cv_ref, o_ref):
    cos = cos_ref[...]
    sin = sin_ref[...]
    k = _rope(_rms(k_ref[...].astype(F32), kn_ref[...]), cos, sin)
    kb = jnp.concatenate([_bf(ck_ref[...]), _bf(k)], axis=0)
    vb = jnp.concatenate([_bf(cv_ref[...]), v_ref[...]], axis=0)
    v_ones = jnp.concatenate([vb, jnp.ones(vb.shape, BF16)], axis=1)
    rb = 256
    blocks = range(DEC_SEQ // rb)

    def scores(h):
        q = q_ref[:, h * HEAD_DIM:(h + 1) * HEAD_DIM].astype(F32)
        q = _bf(_rope(_rms(q, qn_ref[...]), cos, sin))
        return [_dg(q[r * rb:(r + 1) * rb], kb, _NT) * _ATT_SCALE for r in blocks]

    ss_next = scores(0)
    for h in range(_Q_PER_KV):
        ss = ss_next
        if h + 1 < _Q_PER_KV:
            ss_next = scores(h + 1)
        o = jnp.concatenate([_softmax_pv(s, v_ones) for s in ss], axis=0)
        o_ref[:, h * HEAD_DIM:(h + 1) * HEAD_DIM] = o.astype(BF16)


def _attention_ctx(proj, qn, kn, layer):
    blk = lambda w, f: pl.BlockSpec((SEQ, w), f)
    vec = _layer_vec(HEAD_DIM, layer, grid_rank=2)
    kv_out = pl.BlockSpec((None, None, SEQ, HEAD_DIM), lambda b, g: (b, g, 0, 0))
    qw = _Q_PER_KV * HEAD_DIM
    return pl.pallas_call(
        _attn_ctx_kernel,
        grid=(BATCH, ATT_KV_HEADS),
        in_specs=[
            blk(qw, lambda b, g: (b, g)),
            blk(HEAD_DIM, lambda b, g: (b, ATT_HEADS + g)),
            blk(HEAD_DIM, lambda b, g: (b, ATT_HEADS + ATT_KV_HEADS + g)),
            vec, vec,
        ],
        out_specs=[blk(qw, lambda b, g: (b, g)), kv_out, kv_out],
        out_shape=[
            jax.ShapeDtypeStruct((N_CTX, ATT_Q_W), BF16),
            jax.ShapeDtypeStruct((BATCH, ATT_KV_HEADS, SEQ, HEAD_DIM), F32),
            jax.ShapeDtypeStruct((BATCH, ATT_KV_HEADS, SEQ, HEAD_DIM), F32),
        ],
        compiler_params=_cparams(("arbitrary", "arbitrary")),
        name="attention_ctx",
    )(proj, proj, proj, qn, kn)


def _attention_lat(proj, qn, kn, cos, sin, cache_k, cache_v, layer):
    rb = N_CTX // DEC_SEQ
    qw = _Q_PER_KV * HEAD_DIM
    blk = lambda w, f: pl.BlockSpec((DEC_SEQ, w), f)
    vec = _layer_vec(HEAD_DIM, layer, grid_rank=2)
    tab = pl.BlockSpec((DEC_SEQ, HEAD_DIM), lambda b, g: (0, 0))
    cache = pl.BlockSpec((None, None, None, PAST_LEN, HEAD_DIM), lambda b, g: (b, layer, g, 0, 0))
    return pl.pallas_call(
        _attn_lat_kernel,
        grid=(DEC_BATCH, ATT_KV_HEADS),
        in_specs=[
            blk(qw, lambda b, g: (rb + b, g)),
            blk(HEAD_DIM, lambda b, g: (rb + b, ATT_HEADS + g)),
            blk(HEAD_DIM, lambda b, g: (rb + b, ATT_HEADS + ATT_KV_HEADS + g)),
            vec, vec, tab, tab, cache, cache,
        ],
        out_specs=blk(qw, lambda b, g: (b, g)),
        out_shape=jax.ShapeDtypeStruct((N_LAT, ATT_Q_W), BF16),
        compiler_params=_cparams(("arbitrary", "arbitrary")),
        name="attention_lat",
    )(proj, proj, proj, qn, kn, cos, sin, cache_k, cache_v)


def _tri(n, upper):
    r = lax.broadcasted_iota(jnp.int32, (n, n), 0)
    c = lax.broadcasted_iota(jnp.int32, (n, n), 1)
    return (r <= c) if upper else (r >= c)


def _gla_kernel(*refs, seq, has_state):
    if has_state:
        q_ref, k_ref, v_ref, gg_ref, gad_ref, up_ref, bias_ref, gn_ref, s0_ref, o_ref, st_ref = refs
    else:
        q_ref, k_ref, v_ref, gg_ref, gad_ref, up_ref, bias_ref, gn_ref, o_ref, st_ref = refs
        s0_ref = None
    nc = seq // CHUNK
    nh = q_ref.shape[1] // LANE
    c3 = (nc, CHUNK, LANE)
    stack = lambda x: jnp.concatenate([x[:, h * LANE:(h + 1) * LANE].reshape(c3) for h in range(nh)], axis=0)
    gad = gad_ref[...]
    q3 = stack(q_ref[...].astype(F32) * (GLA_DK ** -0.5))
    k3 = stack(k_ref[...].astype(F32))
    v3 = stack(v_ref[...])
    bdot = lambda a, b, ca, cb: lax.dot_general(a, b, (((ca,), (cb,)), ((0,), (0,))), preferred_element_type=F32)

    qe, oi, kv, dec = [], [], [], []
    for d in range(2):
        incl = _tri(CHUNK, d == 1)
        tri = jnp.broadcast_to(jnp.where(incl, 1.0, 0.0).astype(BF16)[None], (nh * nc, CHUNK, CHUNK))
        la = jnp.concatenate(
            [(jax.nn.log_sigmoid(_dg(gad, _bf(up_ref[d, h])) + bias_ref[d, h]) / GLA_TAU).reshape(c3)
             for h in range(nh)], axis=0)
        hi, mid, lo = _split3(la)
        cum = bdot(tri, hi, 2, 1) + (bdot(tri, mid, 2, 1) + bdot(tri, lo, 2, 1))
        tot = cum[:, 0:1, :] if d else cum[:, CHUNK - 1:CHUNK, :]
        qe_d = _bf(q3 * jnp.exp(cum))
        ke = _bf(k3 * jnp.exp(-cum))
        kl = _bf(k3 * jnp.exp(tot - cum))
        att = jnp.where(incl[None], bdot(qe_d, ke, 2, 2), 0.0)
        qe.append(qe_d)
        oi.append(bdot(_bf(att), v3, 2, 1))
        kv.append(bdot(v3, kl, 1, 1))
        dec.append(jnp.exp(tot))

    zero = jnp.zeros((GLA_DK, GLA_DK), F32)
    st = [[s0_ref[d, h].T if has_state else zero for h in range(nh)] for d in range(2)]
    o_f = [[None] * nc for _ in range(nh)]
    o_b = [[None] * nc for _ in range(nh)]
    for t in range(nc):
        nf, nb = t, nc - 1 - t
        for h in range(nh):
            o_f[h][nf] = oi[0][h * nc + nf] + _dg(qe[0][h * nc + nf], _bf(st[0][h]), _NT)
            o_b[h][nb] = oi[1][h * nc + nb] + _dg(qe[1][h * nc + nb], _bf(st[1][h]), _NT)
        for h in range(nh):
            st[0][h] = st[0][h] * dec[0][h * nc + nf] + kv[0][h * nc + nf]
            st[1][h] = st[1][h] * dec[1][h * nc + nb] + kv[1][h * nc + nb]
    outs = []
    for h in range(nh):
        st_ref[0, h] = st[0][h].T
        st_ref[1, h] = st[1][h].T
        outs.append(_rms(jnp.concatenate([a + b for a, b in zip(o_f[h], o_b[h])], axis=0), gn_ref[...]))
    g = gg_ref[...].astype(F32)
    o_ref[...] = (jnp.concatenate(outs, axis=1) * (g * _sigmoid(g))).astype(o_ref.dtype)


def _gla(proj, up_pad, bias, gnorm, s0, layer, seq, nb, row0, nh):
    rb = row0 // seq
    w = nh * LANE
    c0 = COL_GLA // w
    per = GLA_HEADS // nh
    blk = lambda c: pl.BlockSpec((seq, w), lambda b, h: (rb + b, c0 + c * per + h))
    has_state = s0 is not None
    in_specs = [
        blk(0), blk(1), blk(2), blk(3),
        pl.BlockSpec((seq, LANE), lambda b, h: (rb + b, COL_GAD // LANE)),
        pl.BlockSpec((None, 2, nh, LANE, LANE), lambda b, h: (layer, 0, h, 0, 0)),
        pl.BlockSpec((None, 2, nh, 1, LANE), lambda b, h: (layer, 0, h, 0, 0)),
        _layer_vec(LANE, layer, grid_rank=2),
    ]
    args = [proj, proj, proj, proj, proj, up_pad, bias, gnorm]
    if has_state:
        in_specs.append(pl.BlockSpec((None, None, 2, nh, GLA_DK, GLA_DK),
                                     lambda b, h: (b, layer, 0, h, 0, 0)))
        args.append(s0)
    return pl.pallas_call(
        functools.partial(_gla_kernel, seq=seq, has_state=has_state),
        grid=(nb, per),
        in_specs=in_specs,
        out_specs=[
            pl.BlockSpec((seq, w), lambda b, h: (b, h)),
            pl.BlockSpec((None, 2, nh, GLA_DK, GLA_DK), lambda b, h: (b, 0, h, 0, 0)),
        ],
        out_shape=[
            jax.ShapeDtypeStruct((nb * seq, GLA_W), BF16),
            jax.ShapeDtypeStruct((nb, 2, GLA_HEADS, GLA_DK, GLA_DK), F32),
        ],
        compiler_params=_cparams(("arbitrary", "arbitrary")),
        name="gla_lat" if has_state else "gla_ctx",
    )(*args)


def _segsum(x, bd_ref):
    return _mm_exact_rhs(x, bd_ref[...])


def _cumsum(tri_bf, x):
    h, m, l = _split3(x)
    return _dg(tri_bf, h) + (_dg(tri_bf, m) + _dg(tri_bf, l))


def _rwkv_prep_kernel(x_ref, xp_ref, xn_ref, mu_ref, wuf_ref, wub_ref, au_ref,
                      gu_ref, w0_ref, a0_ref, xi_ref, al_ref, bd_ref,
                      r_ref, k_ref, v_ref, kap_ref, beta_ref, lw_ref, g_ref):
    tm = x_ref.shape[0]
    start, end = _seq_edges(pl.program_id(0) * tm, tm)
    x = x_ref[...].astype(F32)
    before = jnp.where(start, 0.0, xp_ref[...].astype(F32)[_HALO - 1:_HALO, :])
    after = jnp.where(end, 0.0, xn_ref[...].astype(F32)[0:1, :])
    prev, nxt = _shift_rows(x, before, after)
    rw = x + (0.5 * (prev + nxt) - x) * mu_ref[...]
    w = RWKV_W
    rr = rw[:, 0:w]
    rk = rw[:, w:2 * w]
    rv = rw[:, 2 * w:3 * w]
    rwd = _bf(jnp.tanh(rw[:, 3 * w:3 * w + LANE]))
    rad = _bf(rw[:, 3 * w + LANE:3 * w + 2 * LANE])
    rgd = _bf(_sigmoid(rw[:, 3 * w + LANE:3 * w + 3 * LANE]))
    lw_ref[0] = -RWKV_DECAY_SCALE * _sigmoid(w0_ref[0:1, :] + _dg(rwd, wuf_ref[...]))
    lw_ref[1] = -RWKV_DECAY_SCALE * _sigmoid(w0_ref[1:2, :] + _dg(rwd, wub_ref[...]))
    a = _sigmoid(a0_ref[...] + _dg(rad, au_ref[...]))
    g_ref[...] = _dg(rgd, gu_ref[...])
    kap = rk * xi_ref[...]
    kap = kap * lax.rsqrt(_segsum(kap * kap, bd_ref) + EPS)
    r_ref[...] = rr
    k_ref[...] = rk * (1.0 + (a - 1.0) * al_ref[...])
    v_ref[...] = rv
    kap_ref[...] = kap
    beta_ref[...] = kap * a


def _rwkv_prep(proj, mu, wuf, wub, au, gu, w0, a0, xi, al, bd, layer):
    tm = 256
    nt = N_TOK // tm
    hb = tm // _HALO
    full = lambda shape: (pl.BlockSpec((None,) + shape, lambda i: (layer,) + (0,) * len(shape)))
    tok = pl.BlockSpec((tm, RWKV_W), lambda i: (i, 0))
    tshape = jax.ShapeDtypeStruct((N_TOK, RWKV_W), F32)
    return pl.pallas_call(
        _rwkv_prep_kernel,
        grid=(nt,),
        in_specs=[
            pl.BlockSpec((tm, RW_W), lambda i: (i, 0)),
            pl.BlockSpec((_HALO, RW_W), lambda i: (jnp.maximum(i * hb - 1, 0), 0)),
            pl.BlockSpec((_HALO, RW_W), lambda i: (jnp.minimum((i + 1) * hb, N_TOK // _HALO - 1), 0)),
            full((1, RW_W)),
            full((LANE, RWKV_W)), full((LANE, RWKV_W)), full((LANE, RWKV_W)), full((2 * LANE, RWKV_W)),
            full((2, RWKV_W)), full((1, RWKV_W)), full((1, RWKV_W)), full((1, RWKV_W)),
            pl.BlockSpec((RWKV_W, RWKV_W), lambda i: (0, 0)),
        ],
        out_specs=[tok, tok, tok, tok, tok,
                   pl.BlockSpec((2, tm, RWKV_W), lambda i: (0, i, 0)), tok],
        out_shape=[tshape, tshape, tshape, tshape, tshape,
                   jax.ShapeDtypeStruct((2, N_TOK, RWKV_W), F32), tshape],
        compiler_params=_cparams(("arbitrary",)),
        name="rwkv_prep",
    )(proj, proj, proj, mu, wuf, wub, au, gu, w0, a0, xi, al, bd)


def _unit_tri_inverse_corr(lows):
    c = [-low for low in lows]
    pb = [_bf(low) for low in lows]
    p = [_dg(b, b) for b in pb]
    span = 2
    while span < CHUNK:
        pb = [_bf(x) for x in p]
        c = [a + x + _dg(_bf(a), xb) for a, x, xb in zip(c, p, pb)]
        span *= 2
        if span < CHUNK:
            p = [_dg(xb, xb) for xb in pb]
    return c


def _rwkv_scan_kernel(*refs, seq, nseq, has_state):
    if has_state:
        r_ref, k_ref, v_ref, kap_ref, beta_ref, lw_ref, s0_ref, y_ref, st_ref, s_scr = refs
    else:
        r_ref, k_ref, v_ref, kap_ref, beta_ref, lw_ref, y_ref, st_ref, s_scr = refs
    n_chunks = seq // CHUNK
    c64 = CHUNK
    rr = lax.broadcasted_iota(jnp.int32, (c64, c64), 0)
    cc = lax.broadcasted_iota(jnp.int32, (c64, c64), 1)
    incl = (rr >= cc, rr <= cc)
    strict = (rr > cc, rr < cc)
    tri = tuple(jnp.where(m, 1.0, 0.0).astype(BF16) for m in incl)
    if has_state:
        s_scr[...] = s0_ref[...]
    else:
        s_scr[...] = jnp.zeros_like(s_scr)
    heads = range(RWKV_HEADS)
    cut = lambda x: [x[:, h * RWKV_HD:(h + 1) * RWKV_HD] for h in heads]
    groups = [(q, d) for q in range(nseq) for d in range(2)]

    def body(i, carry):
        lhs, rhs, upd, v_h, et_h, s, ms, mi, rows_d = [], [], [], [], [], [], [], [], []
        for q, d in groups:
            n = (n_chunks - 1 - i) if d else i
            rows = pl.ds(pl.multiple_of(q * seq + n * c64, c64), c64)
            lw = lw_ref[d, rows, :]
            cum = _cumsum(tri[d], lw)
            tot = cum[0:1, :] if d else cum[c64 - 1:c64, :]
            e_neg = jnp.exp(-cum)
            e_tot = jnp.exp(tot)
            kt = kap_ref[rows, :] * jnp.exp(cum - lw)
            bt = beta_ref[rows, :] * e_neg
            kk = k_ref[rows, :] * e_neg
            rt = r_ref[rows, :] * jnp.exp(cum)
            lhs += cut(_bf(jnp.concatenate([kt, rt], axis=0)))
            rhs += cut(_bf(jnp.concatenate([bt, kk], axis=0)))
            upd += cut(_bf(jnp.concatenate([kk * e_tot, -(bt * e_tot)], axis=0)))
            v_h += cut(v_ref[rows, :])
            et_h += cut(e_tot)
            s += [s_scr[q, d, h] for h in heads]
            ms += [strict[d]] * RWKV_HEADS
            mi += [incl[d]] * RWKV_HEADS
            rows_d.append(rows)
        chains = range(len(groups) * RWKV_HEADS)
        a1 = [_dg(lhs[c], jnp.concatenate([rhs[c], _bf(s[c])], axis=0), _NT) for c in chains]
        low = [jnp.where(ms[c], a1[c][:c64, 0:c64], 0.0) for c in chains]
        g1 = [jnp.where(ms[c], a1[c][:c64, c64:2 * c64], 0.0) for c in chains]
        corr = _unit_tri_inverse_corr(low)
        z = [a1[c][:c64, 2 * c64:] + _mm(g1[c], v_h[c]) for c in chains]
        u = [z[c] + _mm(corr[c], z[c]) for c in chains]
        vu = [_bf(jnp.concatenate([v_h[c], u[c]], axis=0)) for c in chains]
        g23 = [_bf(jnp.concatenate([jnp.where(mi[c], a1[c][c64:, c64:2 * c64], 0.0),
                                    jnp.where(mi[c], -a1[c][c64:, 0:c64], 0.0)], axis=1)) for c in chains]
        y = [a1[c][c64:, 2 * c64:] + _dg(g23[c], vu[c]) for c in chains]
        for c in chains:
            q, d = groups[c // RWKV_HEADS]
            s_scr[q, d, c % RWKV_HEADS] = s[c] * et_h[c] + _dg(vu[c], upd[c], _TN)
        for gi, (q, d) in enumerate(groups):
            y_ref[d, rows_d[gi], :] = jnp.concatenate(y[gi * RWKV_HEADS:(gi + 1) * RWKV_HEADS], axis=1)
        return carry

    lax.fori_loop(0, n_chunks, body, 0)
    st_ref[...] = s_scr[...]


def _rwkv_scan(r, k, v, kap, beta, lw, s0, layer, seq, nb, row0, nseq):
    rows = nseq * seq
    rb = row0 // rows
    has_state = s0 is not None
    mode = dict(pipeline_mode=pl.Buffered(1)) if rows * RWKV_W * 4 > (2 << 20) else {}
    tok = pl.BlockSpec((rows, RWKV_W), lambda b: (rb + b, 0), **mode)
    in_specs = [tok, tok, tok, tok, tok,
                pl.BlockSpec((2, rows, RWKV_W), lambda b: (0, rb + b, 0), **mode)]
    args = [r, k, v, kap, beta, lw]
    st_shape = (2, RWKV_HEADS, RWKV_HD, RWKV_HD)
    if has_state:
        in_specs.append(pl.BlockSpec((nseq, None) + st_shape, lambda b: (b, layer, 0, 0, 0, 0)))
        args.append(s0)
    return pl.pallas_call(
        functools.partial(_rwkv_scan_kernel, seq=seq, nseq=nseq, has_state=has_state),
        grid=(nb // nseq,),
        in_specs=in_specs,
        out_specs=[
            pl.BlockSpec((2, rows, RWKV_W), lambda b: (0, b, 0), **mode),
            pl.BlockSpec((nseq,) + st_shape, lambda b: (b, 0, 0, 0, 0)),
        ],
        out_shape=[
            jax.ShapeDtypeStruct((2, nb * seq, RWKV_W), F32),
            jax.ShapeDtypeStruct((nb,) + st_shape, F32),
        ],
        scratch_shapes=[pltpu.VMEM((nseq,) + st_shape, F32)],
        compiler_params=_cparams(("arbitrary",)),
        name="rwkv_scan_lat" if has_state else "rwkv_scan_ctx",
    )(*args)


def _rwkv_out(ys, r, k, v, g, rho, lnw, lnb, bd_ref):
    bonus = _segsum(r * k * rho, bd_ref) * v
    y = ys[0] + ys[1] + bonus
    inv_n = 1.0 / RWKV_HD
    yc = y - _segsum(y, bd_ref) * inv_n
    var = _segsum(yc * yc, bd_ref) * inv_n
    yn = yc * lax.rsqrt(var + RWKV_LN_EPS)
    return _bf((yn * lnw + lnb) * g)


def _merge_kernel(oaa_ref, oab_ref, oga_ref, ogb_ref, yra_ref, yrb_ref, r_ref, k_ref, v_ref, g_ref,
                  rho_ref, lnw_ref, lnb_ref, bd_ref, gate_ref, xa_ref, xb_ref, mod_ref, nf_ref,
                  wa_ref, wg_ref, wr_ref, wo_ref, ya_ref, yb_ref, h_ref):
    d = D_MODEL
    tm = h_ref.shape[0]
    ctx = _is_ctx(pl.program_id(0), tm)
    pick = lambda a, b: jnp.where(ctx, a[...], b[...])
    o_rw = _rwkv_out(pick(yra_ref, yrb_ref), r_ref[...], k_ref[...], v_ref[...], g_ref[...],
                     rho_ref[...], lnw_ref[...], lnb_ref[...], bd_ref)
    gate = lambda k: _sigmoid(gate_ref[:, k * d:(k + 1) * d].astype(F32))
    merged = gate(0) * _dg(pick(oaa_ref, oab_ref), wa_ref[...])
    merged += gate(1) * _dg(pick(oga_ref, ogb_ref), wg_ref[...])
    merged += gate(2) * _dg(o_rw, wr_ref[...])
    y = pick(xa_ref, xb_ref) + mod_ref[2:3, :] * _dg(_bf(merged), wo_ref[...])
    h_ref[...] = (_rms(y, nf_ref[...]) * (1.0 + mod_ref[4:5, :]) + mod_ref[3:4, :]).astype(BF16)

    @pl.when(ctx)
    def _():
        ya_ref[...] = y

    @pl.when(jnp.logical_not(ctx))
    def _():
        yb_ref[...] = y


def _merge(o_att, o_gla, y_rw, rwkv_parts, rwkv_vecs, bd, gates, x, mod, norm_ffn, wa, wg, wr, wo, layer):
    tm = 256
    na = N_CTX // tm
    const = lambda r, c: pl.BlockSpec((r, c), lambda i: (0, 0), pipeline_mode=pl.Buffered(1))
    xa, xb = _ctx_lat_specs(tm, D_MODEL)
    tok = pl.BlockSpec((tm, RWKV_W), lambda i: (i, 0))
    vec = _layer_vec(RWKV_W, layer)
    return pl.pallas_call(
        _merge_kernel,
        grid=(N_TOK // tm,),
        in_specs=[
            *_ctx_lat_specs(tm, ATT_Q_W),
            *_ctx_lat_specs(tm, GLA_W),
            pl.BlockSpec((2, tm, RWKV_W), lambda i: (0, jnp.minimum(i, na - 1), 0)),
            pl.BlockSpec((2, tm, RWKV_W), lambda i: (0, jnp.maximum(i - na, 0), 0)),
            tok, tok, tok, tok, vec, vec, vec,
            pl.BlockSpec((RWKV_W, RWKV_W), lambda i: (0, 0)),
            pl.BlockSpec((tm, GATE_W), lambda i: (i, 0)),
            xa, xb,
            _mod_spec(tm, layer),
            _layer_vec(D_MODEL, layer),
            const(ATT_Q_W, D_MODEL), const(GLA_W, D_MODEL), const(RWKV_W, D_MODEL), const(D_MODEL, D_MODEL),
        ],
        out_specs=[xa, xb, pl.BlockSpec((tm, D_MODEL), lambda i: (i, 0))],
        out_shape=[jax.ShapeDtypeStruct((N_CTX, D_MODEL), F32), jax.ShapeDtypeStruct((N_LAT, D_MODEL), F32),
                   jax.ShapeDtypeStruct((N_TOK, D_MODEL), BF16)],
        compiler_params=_cparams(("arbitrary",)),
        name="merge_out",
    )(o_att[0], o_att[1], o_gla[0], o_gla[1], y_rw[0], y_rw[1], *rwkv_parts, *rwkv_vecs, bd, gates,
      x[0], x[1], mod, norm_ffn, wa, wg, wr, wo)


_FFN_TN = 512
_FFN_SUB = 256
_FFN_RB = 256


def _ffn_up_kernel(*refs, n_q):
    h_ref, hp_ref, hn_ref = refs[:3]
    groups = [refs[3 + k * (1 + n_q):3 + (k + 1) * (1 + n_q)] for k in range(3)]
    (wv_ref, *wg_refs), (cwv_ref, *cwg_refs), (cbv_ref, *cbg_refs) = groups
    o_ref, wv_scr, wg_scr = refs[3 + 3 * (1 + n_q):]
    tm = h_ref.shape[0]
    i = pl.program_id(1)
    lanes = lambda parts: jnp.concatenate([r[...] for r in parts], axis=1)

    @pl.when(i == 0)
    def _():
        wv_scr[...] = wv_ref[...].astype(BF16)
        wg_scr[...] = lanes(wg_refs).astype(BF16)

    cwv = cwv_ref[...]
    cwg = lanes(cwg_refs)
    cbv = cbv_ref[...]
    cbg = lanes(cbg_refs)

    h = h_ref[...]
    hp = hp_ref[...]
    hn = hn_ref[...]
    n_sub = _FFN_TN // _FFN_SUB
    n_rb = tm // _FFN_RB
    edges = [_seq_edges(i * tm + r * _FFN_RB, _FFN_RB) for r in range(n_rb)]

    def up(w_scr, s):
        w = w_scr[:, s * _FFN_SUB:(s + 1) * _FFN_SUB]
        return _dg(h, w), _dg(hp, w)[7:8, :], _dg(hn, w)[0:1, :]

    def conv(us, cw, cb, s, r):
        u_all, u_before, u_after = us
        cols = slice(s * _FFN_SUB, (s + 1) * _FFN_SUB)
        lo, hi = r * _FFN_RB, (r + 1) * _FFN_RB
        u = u_all[lo:hi]
        start, end = edges[r]
        before = jnp.where(start, 0.0, u_before if r == 0 else u_all[lo - 1:lo])
        after = jnp.where(end, 0.0, u_after if r == n_rb - 1 else u_all[hi:hi + 1])
        prev, nxt = _shift_rows(u, before, after)
        return cw[0:1, cols] * prev + cw[1:2, cols] * u + cw[2:3, cols] * nxt + cb[:, cols]

    ups = [(up(wv_scr, s), up(wg_scr, s)) for s in range(n_sub)]
    for s in range(n_sub):
        for r in range(n_rb):
            val = conv(ups[s][0], cwv, cbv, s, r)
            gate = conv(ups[s][1], cwg, cbg, s, r)
            o_ref[r * _FFN_RB:(r + 1) * _FFN_RB, s * _FFN_SUB:(s + 1) * _FFN_SUB] = _bf(gate * _sigmoid(gate) * val)


def _ffn_down_kernel(a_ref, w_ref, xa_ref, xb_ref, mod_ref, ya_ref, yb_ref):
    tm = a_ref.shape[0]
    ctx = _is_ctx(pl.program_id(1), tm)
    y = jnp.where(ctx, xa_ref[...], xb_ref[...]) + mod_ref[5:6, :] * _dg(a_ref[...], w_ref[...])

    @pl.when(ctx)
    def _():
        ya_ref[...] = y

    @pl.when(jnp.logical_not(ctx))
    def _():
        yb_ref[...] = y


def _ffn(h, x, mod, ffn_up, conv_w, conv_b, w_down, layer):
    tm, tn = 1024, _FFN_TN
    tu = tm
    nj = pl.cdiv(FFN_HIDDEN, tn)
    hb = tu // 8
    gate0 = FFN_HIDDEN // LANE
    last_blk = 2 * FFN_HIDDEN // LANE - 1
    n_q = tn // LANE
    gate_blk = lambda rows, q: pl.BlockSpec(
        (None, rows, LANE), lambda j, i: (layer, 0, jnp.minimum(gate0 + n_q * j + q, last_blk)))
    val_blk = lambda rows: pl.BlockSpec((None, rows, tn), lambda j, i: (layer, 0, j))
    act = pl.pallas_call(
        functools.partial(_ffn_up_kernel, n_q=n_q),
        grid=(nj, N_TOK // tu),
        in_specs=[
            pl.BlockSpec((tu, D_MODEL), lambda j, i: (i, 0)),
            pl.BlockSpec((8, D_MODEL), lambda j, i: (jnp.maximum(i * hb - 1, 0), 0)),
            pl.BlockSpec((8, D_MODEL), lambda j, i: (jnp.minimum((i + 1) * hb, N_TOK // 8 - 1), 0)),
            val_blk(D_MODEL), *[gate_blk(D_MODEL, q) for q in range(n_q)],
            val_blk(3), *[gate_blk(3, q) for q in range(n_q)],
            val_blk(1), *[gate_blk(1, q) for q in range(n_q)],
        ],
        out_specs=pl.BlockSpec((tu, tn), lambda j, i: (i, j)),
        out_shape=jax.ShapeDtypeStruct((N_TOK, FFN_HIDDEN), BF16),
        scratch_shapes=[pltpu.VMEM((D_MODEL, tn), BF16), pltpu.VMEM((D_MODEL, tn), BF16)],
        compiler_params=_cparams(("arbitrary", "arbitrary")),
        name="ffn_up",
    )(h, h, h, *([ffn_up] * (1 + n_q)), *([conv_w] * (1 + n_q)), *([conv_b] * (1 + n_q)))
    to = 512
    xa, xb = _ctx_lat_specs(tm, to, grid_rank=2)
    return pl.pallas_call(
        _ffn_down_kernel,
        grid=(D_MODEL // to, N_TOK // tm),
        in_specs=[
            pl.BlockSpec((tm, FFN_HIDDEN), lambda n, i: (i, 0)),
            pl.BlockSpec((FFN_HIDDEN, to), lambda n, i: (0, n)),
            xa, xb,
            _mod_spec(tm, layer, width=to, grid_rank=2),
        ],
        out_specs=[xa, xb],
        out_shape=[jax.ShapeDtypeStruct((N_CTX, D_MODEL), F32), jax.ShapeDtypeStruct((N_LAT, D_MODEL), F32)],
        compiler_params=_cparams(("arbitrary", "arbitrary")),
        name="ffn_down",
    )(act, w_down, x[0], x[1], mod)


def _rope_tables(t):
    rows = t // GRID_W
    row = jnp.repeat(jnp.arange(rows, dtype=F32), GRID_W)
    col = jnp.tile(jnp.arange(GRID_W, dtype=F32), rows)
    half = HEAD_DIM // 2
    inv = ROPE_THETA ** (-jnp.arange(0, half, 2, dtype=F32) / half)

    def cos_sin(pos):
        ang = pos[:, None] * inv[None, :]
        ang = jnp.concatenate([ang, ang], axis=-1)
        return jnp.cos(ang), jnp.sin(ang)

    cos_r, sin_r = cos_sin(row)
    cos_c, sin_c = cos_sin(col)
    cos = jnp.concatenate([cos_r, cos_c], axis=-1)
    sin = jnp.concatenate([sin_r, sin_c], axis=-1)
    sign = jnp.where((jnp.arange(HEAD_DIM) % half) < half // 2, -1.0, 1.0).astype(F32)
    return cos, sin * sign[None, :]


def kernel(x_prompt, x_sample, cache_k, cache_v, state_gla, state_rwkv, c, c_ctx, w_mod, b_mod, norm_mix, w_in, q_norm, k_norm, gla_a_up, gla_a_bias, gla_norm, rwkv_mu, rwkv_w0, rwkv_w_up, rwkv_a0, rwkv_a_up, rwkv_g_up, rwkv_k_xi, rwkv_k_alpha, rwkv_bonus, rwkv_ln_w, rwkv_ln_b, w_br_att, w_br_gla, w_br_rwkv, w_out, norm_ffn, ffn_up, ffn_conv_w, ffn_conv_b, ffn_down):
    x = (x_prompt.reshape(N_CTX, D_MODEL), x_sample.reshape(N_LAT, D_MODEL))
    cvec = jnp.concatenate([c_ctx[None, :], c, jnp.zeros((N_MODROWS - 1 - DEC_BATCH, D_MODEL), F32)], axis=0)
    mod_all = _modulation(cvec, w_mod, b_mod).reshape(DEPTH, N_MODROWS, 6, D_MODEL)
    cos, sin = _rope_tables(DEC_SEQ)
    hr = lax.broadcasted_iota(jnp.int32, (RWKV_W, RWKV_W), 0) // RWKV_HD
    hc = lax.broadcasted_iota(jnp.int32, (RWKV_W, RWKV_W), 1) // RWKV_HD
    bd = (hr == hc).astype(BF16)
    w_in_t = jnp.swapaxes(w_in, 1, 2)

    vecs = lambda a: a.reshape(DEPTH, 1, -1)
    pad_rows = lambda a, n, at=0: jnp.pad(a, ((0, 0), (at, n - a.shape[1] - at), (0, 0)))
    qn, kn, gnorm, norm_mix3, norm_ffn3 = (vecs(a) for a in (q_norm, k_norm, gla_norm, norm_mix, norm_ffn))
    up = gla_a_up.reshape(DEPTH, 2, GLA_RANK, GLA_HEADS, GLA_DK).transpose(0, 1, 3, 2, 4)
    up_pad = jnp.stack([
        jnp.pad(up[:, 0], ((0, 0), (0, 0), (0, LANE - GLA_RANK), (0, 0))),
        jnp.pad(up[:, 1], ((0, 0), (0, 0), (GLA_RANK, LANE - 2 * GLA_RANK), (0, 0))),
    ], axis=1)
    gbias = gla_a_bias.reshape(DEPTH, 2, GLA_HEADS, 1, GLA_DK)
    mu = vecs(jnp.pad(rwkv_mu, ((0, 0), (0, RW_W - RWKV_COLS))))
    wuf = _bf(pad_rows(rwkv_w_up[:, 0], LANE))
    wub = _bf(pad_rows(rwkv_w_up[:, 1], LANE, at=RWKV_W_RANK))
    au = _bf(pad_rows(rwkv_a_up, LANE))
    gu = _bf(pad_rows(rwkv_g_up, 2 * LANE, at=RWKV_A_RANK))
    a0, xi, al, rho, lnw, lnb = (vecs(a) for a in (rwkv_a0, rwkv_k_xi, rwkv_k_alpha, rwkv_bonus, rwkv_ln_w, rwkv_ln_b))
    conv_b = vecs(ffn_conv_b)

    ks, vs, gla_states, rwkv_states = [], [], [], []
    for l in range(DEPTH):
        h = _prenorm(x, norm_mix3, mod_all, l, 0)
        proj = _proj(h, w_in_t, l, 0, MAIN_W, 1024, "proj_main", BF16)
        p_rw, wa, wg, wr, wo = _proj(h, w_in_t, l, COL_RW, RW_W, 512, "proj_rwkv", BF16,
                                     casts=(w_br_att, w_br_gla, w_br_rwkv, w_out))
        gates, wd = _proj(h, w_in_t, l, COL_GATE, GATE_W, GATE_TN, "proj_gates", BF16, casts=(ffn_down,))

        oa_c, k_c, v_c = _attention_ctx(proj, qn, kn, l)
        oa_l = _attention_lat(proj, qn, kn, cos, sin, cache_k, cache_v, l)
        ks.append(k_c)
        vs.append(v_c)

        og_c, gs_c = _gla(proj, up_pad, gbias, gnorm, None, l, SEQ, BATCH, 0, GLA_HEADS)
        og_l, _ = _gla(proj, up_pad, gbias, gnorm, state_gla, l, DEC_SEQ, DEC_BATCH, N_CTX, 2)
        gla_states.append(gs_c)

        r_, k_, v_, kap_, beta_, lw_, g_ = _rwkv_prep(p_rw, mu, wuf, wub, au, gu, rwkv_w0, a0, xi, al, bd, l)
        y_c, rs_c = _rwkv_scan(r_, k_, v_, kap_, beta_, lw_, None, l, SEQ, BATCH, 0, 2)
        y_l, _ = _rwkv_scan(r_, k_, v_, kap_, beta_, lw_, state_rwkv, l, DEC_SEQ, DEC_BATCH, N_CTX, 2)
        rwkv_states.append(rs_c)

        xa, xb, h2 = _merge((oa_c, oa_l), (og_c, og_l), (y_c, y_l), (r_, k_, v_, g_), (rho, lnw, lnb), bd,
                            gates, x, mod_all, norm_ffn3, wa, wg, wr, wo, l)
        x = _ffn(h2, (xa, xb), mod_all, ffn_up, ffn_conv_w, conv_b, wd, l)

    y_prompt = x[0].reshape(BATCH, SEQ, D_MODEL)
    y_sample = x[1].reshape(DEC_BATCH, DEC_SEQ, D_MODEL)
    new_cache_k = jnp.stack(ks, axis=1)
    new_cache_v = jnp.stack(vs, axis=1)
    new_state_gla = jnp.stack(gla_states, axis=1)
    new_state_rwkv = jnp.stack(rwkv_states, axis=1)
    return (y_prompt, y_sample, new_cache_k, new_cache_v, new_state_gla, new_state_rwkv)
```

```python
import functools

import jax
import jax.numpy as jnp
from jax import lax
from jax.experimental import pallas as pl
from jax.experimental.pallas import tpu as pltpu

F32 = jnp.float32
BF16 = jnp.bfloat16

D_MODEL = 2048
BATCH = 16
SEQ = 256
DEPTH = 2
DEC_BATCH = 4
DEC_SEQ = 1024
PAST_LEN = 256
GRID_W = 64
HEAD_DIM = 128
ATT_HEADS = 8
ATT_KV_HEADS = 2
ROPE_THETA = 10000.0
GLA_HEADS = 4
GLA_DK = 128
GLA_RANK = 16
GLA_TAU = 16.0
RWKV_HEADS = 8
RWKV_HD = 64
RWKV_W_RANK = 64
RWKV_A_RANK = 64
RWKV_G_RANK = 128
RWKV_DECAY_SCALE = 0.606531
RWKV_LN_EPS = 64e-5
FFN_HIDDEN = 5504
EPS = 1e-6

ATT_Q_W = ATT_HEADS * HEAD_DIM
ATT_KV_W = ATT_KV_HEADS * HEAD_DIM
GLA_W = GLA_HEADS * GLA_DK
RWKV_W = RWKV_HEADS * RWKV_HD
RWKV_COLS = 3 * RWKV_W + 2 * RWKV_W_RANK + RWKV_A_RANK + RWKV_G_RANK
IN_COLS = ATT_Q_W + 2 * ATT_KV_W + 4 * GLA_W + 2 * GLA_RANK + RWKV_COLS + 3 * D_MODEL

LANE = 128
CHUNK = 64
N_CTX = BATCH * SEQ
N_LAT = DEC_BATCH * DEC_SEQ
N_TOK = N_CTX + N_LAT
N_MODROWS = 8
VMEM_LIMIT = 56 * 1024 * 1024
_HALO = 16

COL_GLA = ATT_Q_W + 2 * ATT_KV_W
COL_GAD = COL_GLA + 4 * GLA_W
COL_RW = COL_GAD + 2 * GLA_RANK
COL_GATE = COL_RW + RWKV_COLS
MAIN_W = 4096
RW_W = 2048
GATE_TN = 768
GATE_W = 3 * D_MODEL

_NT = (((1,), (1,)), ((), ()))
_TN = (((0,), (0,)), ((), ()))
_NN = (((1,), (0,)), ((), ()))


def _bf(x):
    return x.astype(BF16)


def _dg(a, b, dims=_NN):
    return lax.dot_general(a, b, dims, preferred_element_type=F32)


def _mm(a, b, dims=_NN):
    return _dg(_bf(a), _bf(b), dims)


def _split3(x):
    h = x.astype(BF16)
    r = x - h.astype(F32)
    m = r.astype(BF16)
    return h, m, (r - m.astype(F32)).astype(BF16)


def _sigmoid(x):
    return 0.5 * jnp.tanh(0.5 * x) + 0.5


def _rms(x, g):
    return x * lax.rsqrt(jnp.mean(x * x, axis=-1, keepdims=True) + EPS) * g


def _cparams(sem):
    return pltpu.CompilerParams(dimension_semantics=sem, vmem_limit_bytes=VMEM_LIMIT)


def _mod_row(i, tm):
    n_ctx = N_CTX // tm
    return jnp.where(i < n_ctx, 0, 1 + (i - n_ctx) // (DEC_SEQ // tm))


def _ctx_lat_specs(tm, width, grid_rank=1):
    na = N_CTX // tm
    if grid_rank == 1:
        return (pl.BlockSpec((tm, width), lambda i: (jnp.minimum(i, na - 1), 0)),
                pl.BlockSpec((tm, width), lambda i: (jnp.maximum(i - na, 0), 0)))
    return (pl.BlockSpec((tm, width), lambda n, i: (jnp.minimum(i, na - 1), n)),
            pl.BlockSpec((tm, width), lambda n, i: (jnp.maximum(i - na, 0), n)))


def _is_ctx(i, tm):
    return i < N_CTX // tm


def _seq_edges(g, rows):
    lat = g - N_CTX
    start = (g < N_CTX) | ((lat & (DEC_SEQ - 1)) == 0)
    end = (g < N_CTX) | (((lat + rows) & (DEC_SEQ - 1)) == 0)
    if rows < SEQ:
        start = jnp.where(g < N_CTX, (g & (SEQ - 1)) == 0, start)
        end = jnp.where(g < N_CTX, ((g + rows) & (SEQ - 1)) == 0, end)
    return start, end


def _shift_rows(x, before, after):
    n = x.shape[0]
    row = lax.broadcasted_iota(jnp.int32, (n, 1), 0)
    prev = jnp.where(row == 0, before, pltpu.roll(x, 1, 0))
    nxt = jnp.where(row == n - 1, after, pltpu.roll(x, n - 1, 0))
    return prev, nxt


def _mod_kernel(c_ref, w_ref, b_ref, o_ref):
    c = c_ref[...]
    o_ref[...] = _mm(c * _sigmoid(c), w_ref[...]) + b_ref[...]


def _modulation(cvec, w_mod, b_mod):
    tn = 1024
    n = 6 * D_MODEL
    return pl.pallas_call(
        _mod_kernel,
        grid=(DEPTH, n // tn),
        in_specs=[
            pl.BlockSpec((N_MODROWS, D_MODEL), lambda l, j: (0, 0)),
            pl.BlockSpec((None, D_MODEL, tn), lambda l, j: (l, 0, j)),
            pl.BlockSpec((None, 1, tn), lambda l, j: (l, 0, j)),
        ],
        out_specs=pl.BlockSpec((None, N_MODROWS, tn), lambda l, j: (l, 0, j)),
        out_shape=jax.ShapeDtypeStruct((DEPTH, N_MODROWS, n), F32),
        compiler_params=_cparams(("arbitrary", "arbitrary")),
        name="modulation",
    )(cvec, w_mod, b_mod.reshape(DEPTH, 1, n))


def _prenorm_kernel(xa_ref, xb_ref, g_ref, mod_ref, o_ref, *, shift_idx):
    tm = o_ref.shape[0]
    x = jnp.where(_is_ctx(pl.program_id(0), tm), xa_ref[...], xb_ref[...])
    y = _rms(x, g_ref[...])
    sh = mod_ref[shift_idx:shift_idx + 1, :]
    sc = mod_ref[shift_idx + 1:shift_idx + 2, :]
    o_ref[...] = (y * (1.0 + sc) + sh).astype(BF16)


def _layer_vec(width, layer, grid_rank=1):
    if grid_rank == 1:
        return pl.BlockSpec((None, 1, width), lambda i: (layer, 0, 0))
    return pl.BlockSpec((None, 1, width), lambda a, b: (layer, 0, 0))


def _mod_spec(tm, layer, width=D_MODEL, grid_rank=1):
    if grid_rank == 1:
        return pl.BlockSpec((None, None, 6, width), lambda i: (layer, _mod_row(i, tm), 0, 0))
    return pl.BlockSpec((None, None, 6, width), lambda n, i: (layer, _mod_row(i, tm), 0, n))


def _prenorm(x, g_all, mod_all, layer, shift_idx):
    tm = 512
    xa, xb = _ctx_lat_specs(tm, D_MODEL)
    return pl.pallas_call(
        functools.partial(_prenorm_kernel, shift_idx=shift_idx),
        grid=(N_TOK // tm,),
        in_specs=[xa, xb, _layer_vec(D_MODEL, layer), _mod_spec(tm, layer)],
        out_specs=pl.BlockSpec((tm, D_MODEL), lambda i: (i, 0)),
        out_shape=jax.ShapeDtypeStruct((N_TOK, D_MODEL), BF16),
        compiler_params=_cparams(("arbitrary",)),
        name="prenorm",
    )(x[0], x[1], g_all, mod_all)


_CAST_ROWS = 128


def _proj_kernel(*refs, n_w, n_cast, row_shift):
    h_ref = refs[0]
    w_refs = refs[1:1 + n_w]
    cast_in = refs[1 + n_w:1 + n_w + n_cast]
    o_ref = refs[1 + n_w + n_cast]
    cast_out = refs[2 + n_w + n_cast:2 + n_w + 2 * n_cast]
    w_scr = refs[-1]

    @pl.when(pl.program_id(1) == 0)
    def _():
        w = w_refs[0][...] if n_w == 1 else jnp.concatenate([r[...] for r in w_refs], axis=0)
        w_scr[...] = w[row_shift:row_shift + w_scr.shape[0], :].astype(BF16)

    o_ref[...] = _dg(h_ref[...], w_scr[...], _NT).astype(o_ref.dtype)
    for src, dst in zip(cast_in, cast_out):
        dst[...] = src[...].astype(BF16)


def _proj(h, w_in_t, layer, col0, width, tn, name, dtype=F32, casts=()):
    tm = 1024
    off = col0 % tn
    base = col0 - off
    assert width % tn == 0 and off % 8 == 0 and (off == 0 or (tn % off == 0 and IN_COLS % off == 0))
    grid = (width // tn, N_TOK // tm)
    in_specs = [
        pl.BlockSpec((tm, D_MODEL), lambda j, i: (i, 0)),
        pl.BlockSpec((None, tn, D_MODEL), lambda j, i: (layer, base // tn + j, 0)),
    ]
    if off:
        in_specs.append(pl.BlockSpec((None, off, D_MODEL), lambda j, i: (layer, (base + tn * (j + 1)) // off, 0)))
    n_w = len(in_specs) - 1
    out_specs = [pl.BlockSpec((tm, tn), lambda j, i: (i, j))]
    out_shape = [jax.ShapeDtypeStruct((N_TOK, width), dtype)]
    for w in casts:
        rows, cols = w.shape[1:]
        last = rows // _CAST_ROWS - 1
        assert rows % _CAST_ROWS == 0 and last < grid[0] * grid[1]
        step = lambda j, i, last=last: jnp.minimum(j * grid[1] + i, last)
        in_specs.append(pl.BlockSpec((None, _CAST_ROWS, cols), lambda j, i, step=step: (layer, step(j, i), 0)))
        out_specs.append(pl.BlockSpec((_CAST_ROWS, cols), lambda j, i, step=step: (step(j, i), 0)))
        out_shape.append(jax.ShapeDtypeStruct((rows, cols), BF16))
    outs = pl.pallas_call(
        functools.partial(_proj_kernel, n_w=n_w, n_cast=len(casts), row_shift=off),
        grid=grid,
        in_specs=in_specs,
        out_specs=out_specs,
        out_shape=out_shape,
        scratch_shapes=[pltpu.VMEM((tn, D_MODEL), BF16)],
        compiler_params=_cparams(("arbitrary", "arbitrary")),
        name=name,
    )(h, *([w_in_t] * n_w), *casts)
    return outs[0] if not casts else outs


_ATT_SCALE = HEAD_DIM ** -0.5
_Q_PER_KV = ATT_HEADS // ATT_KV_HEADS


def _rope(x, cos, sin_signed):
    lane = lax.broadcasted_iota(jnp.int32, x.shape, 1)
    partner = jnp.where((lane & 63) < 32, pltpu.roll(x, 96, 1), pltpu.roll(x, 32, 1))
    return x * cos + partner * sin_signed


def _softmax_pv(s, v_ones):
    p = _bf(jnp.exp(s - jnp.max(s, axis=-1, keepdims=True)))
    oa = _dg(p, v_ones)
    return oa[:, :HEAD_DIM] / oa[:, HEAD_DIM:]


def _attn_ctx_kernel(q_ref, k_ref, v_ref, qn_ref, kn_ref, o_ref, ko_ref, vo_ref):
    k = _rms(k_ref[...].astype(F32), kn_ref[...])
    v = v_ref[...]
    ko_ref[...] = k
    vo_ref[...] = v.astype(F32)
    kb = _bf(k)
    v_ones = jnp.concatenate([v, jnp.ones(v.shape, BF16)], axis=1)
    heads = range(_Q_PER_KV)
    qs = [_bf(_rms(q_ref[:, h * HEAD_DIM:(h + 1) * HEAD_DIM].astype(F32), qn_ref[...])) for h in heads]
    ss = [_dg(qs[h], kb, _NT) * _ATT_SCALE for h in heads]
    o_ref[...] = jnp.concatenate([_softmax_pv(s, v_ones) for s in ss], axis=1).astype(BF16)


def _attn_lat_kernel(q_ref, k_ref, v_ref, qn_ref, kn_ref, cos_ref, sin_ref, ck_ref, cv_ref, o_ref):
    cos = cos_ref[...]
    sin = sin_ref[...]
    k = _rope(_rms(k_ref[...].astype(F32), kn_ref[...]), cos, sin)
    kb = jnp.concatenate([_bf(ck_ref[...]), _bf(k)], axis=0)
    vb = jnp.concatenate([_bf(cv_ref[...]), v_ref[...]], axis=0)
    v_ones = jnp.concatenate([vb, jnp.ones(vb.shape, BF16)], axis=1)
    rb = 256
    blocks = range(DEC_SEQ // rb)

    def scores(h):
        q = q_ref[:, h * HEAD_DIM:(h + 1) * HEAD_DIM].astype(F32)
        q = _bf(_rope(_rms(q, qn_ref[...]), cos, sin))
        return [_dg(q[r * rb:(r + 1) * rb], kb, _NT) * _ATT_SCALE for r in blocks]

    ss_next = scores(0)
    for h in range(_Q_PER_KV):
        ss = ss_next
        if h + 1 < _Q_PER_KV:
            ss_next = scores(h + 1)
        o = jnp.concatenate([_softmax_pv(s, v_ones) for s in ss], axis=0)
        o_ref[:, h * HEAD_DIM:(h + 1) * HEAD_DIM] = o.astype(BF16)


def _attention_ctx(proj, qn, kn, layer):
    blk = lambda w, f: pl.BlockSpec((SEQ, w), f)
    vec = _layer_vec(HEAD_DIM, layer, grid_rank=2)
    kv_out = pl.BlockSpec((None, None, SEQ, HEAD_DIM), lambda b, g: (b, g, 0, 0))
    qw = _Q_PER_KV * HEAD_DIM
    return pl.pallas_call(
        _attn_ctx_kernel,
        grid=(BATCH, ATT_KV_HEADS),
        in_specs=[
            blk(qw, lambda b, g: (b, g)),
            blk(HEAD_DIM, lambda b, g: (b, ATT_HEADS + g)),
            blk(HEAD_DIM, lambda b, g: (b, ATT_HEADS + ATT_KV_HEADS + g)),
            vec, vec,
        ],
        out_specs=[blk(qw, lambda b, g: (b, g)), kv_out, kv_out],
        out_shape=[
            jax.ShapeDtypeStruct((N_CTX, ATT_Q_W), BF16),
            jax.ShapeDtypeStruct((BATCH, ATT_KV_HEADS, SEQ, HEAD_DIM), F32),
            jax.ShapeDtypeStruct((BATCH, ATT_KV_HEADS, SEQ, HEAD_DIM), F32),
        ],
        compiler_params=_cparams(("arbitrary", "arbitrary")),
        name="attention_ctx",
    )(proj, proj, proj, qn, kn)


def _attention_lat(proj, qn, kn, cos, sin, cache_k, cache_v, layer):
    rb = N_CTX // DEC_SEQ
    qw = _Q_PER_KV * HEAD_DIM
    blk = lambda w, f: pl.BlockSpec((DEC_SEQ, w), f)
    vec = _layer_vec(HEAD_DIM, layer, grid_rank=2)
    tab = pl.BlockSpec((DEC_SEQ, HEAD_DIM), lambda b, g: (0, 0))
    cache = pl.BlockSpec((None, None, None, PAST_LEN, HEAD_DIM), lambda b, g: (b, layer, g, 0, 0))
    return pl.pallas_call(
        _attn_lat_kernel,
        grid=(DEC_BATCH, ATT_KV_HEADS),
        in_specs=[
            blk(qw, lambda b, g: (rb + b, g)),
            blk(HEAD_DIM, lambda b, g: (rb + b, ATT_HEADS + g)),
            blk(HEAD_DIM, lambda b, g: (rb + b, ATT_HEADS + ATT_KV_HEADS + g)),
            vec, vec, tab, tab, cache, cache,
        ],
        out_specs=blk(qw, lambda b, g: (b, g)),
        out_shape=jax.ShapeDtypeStruct((N_LAT, ATT_Q_W), BF16),
        compiler_params=_cparams(("arbitrary", "arbitrary")),
        name="attention_lat",
    )(proj, proj, proj, qn, kn, cos, sin, cache_k, cache_v)


def _tri(n, upper):
    r = lax.broadcasted_iota(jnp.int32, (n, n), 0)
    c = lax.broadcasted_iota(jnp.int32, (n, n), 1)
    return (r <= c) if upper else (r >= c)


def _gla_kernel(*refs, seq, has_state):
    if has_state:
        q_ref, k_ref, v_ref, gg_ref, gad_ref, up_ref, bias_ref, gn_ref, s0_ref, o_ref, st_ref = refs
    else:
        q_ref, k_ref, v_ref, gg_ref, gad_ref, up_ref, bias_ref, gn_ref, o_ref, st_ref = refs
        s0_ref = None
    nc = seq // CHUNK
    nh = q_ref.shape[1] // LANE
    c3 = (nc, CHUNK, LANE)
    stack = lambda x: jnp.concatenate([x[:, h * LANE:(h + 1) * LANE].reshape(c3) for h in range(nh)], axis=0)
    gad = gad_ref[...]
    q3 = stack(q_ref[...].astype(F32) * (GLA_DK ** -0.5))
    k3 = stack(k_ref[...].astype(F32))
    v3 = stack(v_ref[...])
    bdot = lambda a, b, ca, cb: lax.dot_general(a, b, (((ca,), (cb,)), ((0,), (0,))), preferred_element_type=F32)

    qe, oi, kv, dec = [], [], [], []
    for d in range(2):
        incl = _tri(CHUNK, d == 1)
        tri = jnp.broadcast_to(jnp.where(incl, 1.0, 0.0).astype(BF16)[None], (nh * nc, CHUNK, CHUNK))
        la = jnp.concatenate(
            [(jax.nn.log_sigmoid(_dg(gad, _bf(up_ref[d, h])) + bias_ref[d, h]) / GLA_TAU).reshape(c3)
             for h in range(nh)], axis=0)
        hi, mid, lo = _split3(la)
        cum = bdot(tri, hi, 2, 1) + (bdot(tri, mid, 2, 1) + bdot(tri, lo, 2, 1))
        tot = cum[:, 0:1, :] if d else cum[:, CHUNK - 1:CHUNK, :]
        qe_d = _bf(q3 * jnp.exp(cum))
        ke = _bf(k3 * jnp.exp(-cum))
        kl = _bf(k3 * jnp.exp(tot - cum))
        att = jnp.where(incl[None], bdot(qe_d, ke, 2, 2), 0.0)
        qe.append(qe_d)
        oi.append(bdot(_bf(att), v3, 2, 1))
        kv.append(bdot(v3, kl, 1, 1))
        dec.append(jnp.exp(tot))

    zero = jnp.zeros((GLA_DK, GLA_DK), F32)
    st = [[s0_ref[d, h].T if has_state else zero for h in range(nh)] for d in range(2)]
    o_f = [[None] * nc for _ in range(nh)]
    o_b = [[None] * nc for _ in range(nh)]
    for t in range(nc):
        nf, nb = t, nc - 1 - t
        for h in range(nh):
            o_f[h][nf] = oi[0][h * nc + nf] + _dg(qe[0][h * nc + nf], _bf(st[0][h]), _NT)
            o_b[h][nb] = oi[1][h * nc + nb] + _dg(qe[1][h * nc + nb], _bf(st[1][h]), _NT)
        for h in range(nh):
            st[0][h] = st[0][h] * dec[0][h * nc + nf] + kv[0][h * nc + nf]
            st[1][h] = st[1][h] * dec[1][h * nc + nb] + kv[1][h * nc + nb]
    outs = []
    for h in range(nh):
        st_ref[0, h] = st[0][h].T
        st_ref[1, h] = st[1][h].T
        outs.append(_rms(jnp.concatenate([a + b for a, b in zip(o_f[h], o_b[h])], axis=0), gn_ref[...]))
    g = gg_ref[...].astype(F32)
    o_ref[...] = (jnp.concatenate(outs, axis=1) * (g * _sigmoid(g))).astype(o_ref.dtype)


def _gla(proj, up_pad, bias, gnorm, s0, layer, seq, nb, row0, nh):
    rb = row0 // seq
    w = nh * LANE
    c0 = COL_GLA // w
    per = GLA_HEADS // nh
    blk = lambda c: pl.BlockSpec((seq, w), lambda b, h: (rb + b, c0 + c * per + h))
    has_state = s0 is not None
    in_specs = [
        blk(0), blk(1), blk(2), blk(3),
        pl.BlockSpec((seq, LANE), lambda b, h: (rb + b, COL_GAD // LANE)),
        pl.BlockSpec((None, 2, nh, LANE, LANE), lambda b, h: (layer, 0, h, 0, 0)),
        pl.BlockSpec((None, 2, nh, 1, LANE), lambda b, h: (layer, 0, h, 0, 0)),
        _layer_vec(LANE, layer, grid_rank=2),
    ]
    args = [proj, proj, proj, proj, proj, up_pad, bias, gnorm]
    if has_state:
        in_specs.append(pl.BlockSpec((None, None, 2, nh, GLA_DK, GLA_DK),
                                     lambda b, h: (b, layer, 0, h, 0, 0)))
        args.append(s0)
    return pl.pallas_call(
        functools.partial(_gla_kernel, seq=seq, has_state=has_state),
        grid=(nb, per),
        in_specs=in_specs,
        out_specs=[
            pl.BlockSpec((seq, w), lambda b, h: (b, h)),
            pl.BlockSpec((None, 2, nh, GLA_DK, GLA_DK), lambda b, h: (b, 0, h, 0, 0)),
        ],
        out_shape=[
            jax.ShapeDtypeStruct((nb * seq, GLA_W), BF16),
            jax.ShapeDtypeStruct((nb, 2, GLA_HEADS, GLA_DK, GLA_DK), F32),
        ],
        compiler_params=_cparams(("arbitrary", "arbitrary")),
        name="gla_lat" if has_state else "gla_ctx",
    )(*args)


def _segsum(x, bd_ref):
    h = x.astype(BF16)
    m = (x - h.astype(F32)).astype(BF16)
    bd = bd_ref[...]
    return _dg(h, bd) + _dg(m, bd)


def _cumsum(tri_bf, x):
    h, m, l = _split3(x)
    return _dg(tri_bf, h) + (_dg(tri_bf, m) + _dg(tri_bf, l))


def _rwkv_prep_kernel(x_ref, xp_ref, xn_ref, mu_ref, wuf_ref, wub_ref, au_ref,
                      gu_ref, w0_ref, a0_ref, xi_ref, al_ref, bd_ref,
                      r_ref, k_ref, v_ref, kap_ref, beta_ref, lw_ref, g_ref):
    tm = x_ref.shape[0]
    start, end = _seq_edges(pl.program_id(0) * tm, tm)
    x = x_ref[...].astype(F32)
    before = jnp.where(start, 0.0, xp_ref[...].astype(F32)[_HALO - 1:_HALO, :])
    after = jnp.where(end, 0.0, xn_ref[...].astype(F32)[0:1, :])
    prev, nxt = _shift_rows(x, before, after)
    rw = x + (0.5 * (prev + nxt) - x) * mu_ref[...]
    w = RWKV_W
    rr = rw[:, 0:w]
    rk = rw[:, w:2 * w]
    rv = rw[:, 2 * w:3 * w]
    rwd = _bf(jnp.tanh(rw[:, 3 * w:3 * w + LANE]))
    rad = _bf(rw[:, 3 * w + LANE:3 * w + 2 * LANE])
    rgd = _bf(_sigmoid(rw[:, 3 * w + LANE:3 * w + 3 * LANE]))
    lw_ref[0] = -RWKV_DECAY_SCALE * _sigmoid(w0_ref[0:1, :] + _dg(rwd, wuf_ref[...]))
    lw_ref[1] = -RWKV_DECAY_SCALE * _sigmoid(w0_ref[1:2, :] + _dg(rwd, wub_ref[...]))
    a = _sigmoid(a0_ref[...] + _dg(rad, au_ref[...]))
    g_ref[...] = _dg(rgd, gu_ref[...])
    kap = rk * xi_ref[...]
    kap = kap * lax.rsqrt(_segsum(kap * kap, bd_ref) + EPS)
    r_ref[...] = rr
    k_ref[...] = rk * (1.0 + (a - 1.0) * al_ref[...])
    v_ref[...] = rv
    kap_ref[...] = kap
    beta_ref[...] = kap * a


def _rwkv_prep(proj, mu, wuf, wub, au, gu, w0, a0, xi, al, bd, layer):
    tm = 256
    nt = N_TOK // tm
    hb = tm // _HALO
    full = lambda shape: (pl.BlockSpec((None,) + shape, lambda i: (layer,) + (0,) * len(shape)))
    tok = pl.BlockSpec((tm, RWKV_W), lambda i: (i, 0))
    tshape = jax.ShapeDtypeStruct((N_TOK, RWKV_W), F32)
    return pl.pallas_call(
        _rwkv_prep_kernel,
        grid=(nt,),
        in_specs=[
            pl.BlockSpec((tm, RW_W), lambda i: (i, 0)),
            pl.BlockSpec((_HALO, RW_W), lambda i: (jnp.maximum(i * hb - 1, 0), 0)),
            pl.BlockSpec((_HALO, RW_W), lambda i: (jnp.minimum((i + 1) * hb, N_TOK // _HALO - 1), 0)),
            full((1, RW_W)),
            full((LANE, RWKV_W)), full((LANE, RWKV_W)), full((LANE, RWKV_W)), full((2 * LANE, RWKV_W)),
            full((2, RWKV_W)), full((1, RWKV_W)), full((1, RWKV_W)), full((1, RWKV_W)),
            pl.BlockSpec((RWKV_W, RWKV_W), lambda i: (0, 0)),
        ],
        out_specs=[tok, tok, tok, tok, tok,
                   pl.BlockSpec((2, tm, RWKV_W), lambda i: (0, i, 0)), tok],
        out_shape=[tshape, tshape, tshape, tshape, tshape,
                   jax.ShapeDtypeStruct((2, N_TOK, RWKV_W), F32), tshape],
        compiler_params=_cparams(("arbitrary",)),
        name="rwkv_prep",
    )(proj, proj, proj, mu, wuf, wub, au, gu, w0, a0, xi, al, bd)


def _unit_tri_inverse_corr(lows):
    c = [-low for low in lows]
    pb = [_bf(low) for low in lows]
    p = [_dg(b, b) for b in pb]
    span = 2
    while span < CHUNK:
        pb = [_bf(x) for x in p]
        c = [a + x + _dg(_bf(a), xb) for a, x, xb in zip(c, p, pb)]
        span *= 2
        if span < CHUNK:
            p = [_dg(xb, xb) for xb in pb]
    return c


def _rwkv_scan_kernel(*refs, seq, nseq, has_state):
    if has_state:
        r_ref, k_ref, v_ref, kap_ref, beta_ref, lw_ref, s0_ref, y_ref, st_ref, s_scr = refs
    else:
        r_ref, k_ref, v_ref, kap_ref, beta_ref, lw_ref, y_ref, st_ref, s_scr = refs
    n_chunks = seq // CHUNK
    c64 = CHUNK
    rr = lax.broadcasted_iota(jnp.int32, (c64, c64), 0)
    cc = lax.broadcasted_iota(jnp.int32, (c64, c64), 1)
    incl = (rr >= cc, rr <= cc)
    strict = (rr > cc, rr < cc)
    tri = tuple(jnp.where(m, 1.0, 0.0).astype(BF16) for m in incl)
    if has_state:
        s_scr[...] = s0_ref[...]
    else:
        s_scr[...] = jnp.zeros_like(s_scr)
    heads = range(RWKV_HEADS)
    cut = lambda x: [x[:, h * RWKV_HD:(h + 1) * RWKV_HD] for h in heads]
    groups = [(q, d) for q in range(nseq) for d in range(2)]

    def body(i, carry):
        lhs, rhs, upd, v_h, et_h, s, ms, mi, rows_d = [], [], [], [], [], [], [], [], []
        for q, d in groups:
            n = (n_chunks - 1 - i) if d else i
            rows = pl.ds(pl.multiple_of(q * seq + n * c64, c64), c64)
            lw = lw_ref[d, rows, :]
            cum = _cumsum(tri[d], lw)
            tot = cum[0:1, :] if d else cum[c64 - 1:c64, :]
            e_neg = jnp.exp(-cum)
            e_tot = jnp.exp(tot)
            kt = kap_ref[rows, :] * jnp.exp(cum - lw)
            bt = beta_ref[rows, :] * e_neg
            kk = k_ref[rows, :] * e_neg
            rt = r_ref[rows, :] * jnp.exp(cum)
            lhs += cut(_bf(jnp.concatenate([kt, rt], axis=0)))
            rhs += cut(_bf(jnp.concatenate([bt, kk], axis=0)))
            upd += cut(_bf(jnp.concatenate([kk * e_tot, -(bt * e_tot)], axis=0)))
            v_h += cut(v_ref[rows, :])
            et_h += cut(e_tot)
            s += [s_scr[q, d, h] for h in heads]
            ms += [strict[d]] * RWKV_HEADS
            mi += [incl[d]] * RWKV_HEADS
            rows_d.append(rows)
        chains = range(len(groups) * RWKV_HEADS)
        a1 = [_dg(lhs[c], jnp.concatenate([rhs[c], _bf(s[c])], axis=0), _NT) for c in chains]
        low = [jnp.where(ms[c], a1[c][:c64, 0:c64], 0.0) for c in chains]
        g1 = [jnp.where(ms[c], a1[c][:c64, c64:2 * c64], 0.0) for c in chains]
        corr = _unit_tri_inverse_corr(low)
        z = [a1[c][:c64, 2 * c64:] + _mm(g1[c], v_h[c]) for c in chains]
        u = [z[c] + _mm(corr[c], z[c]) for c in chains]
        vu = [_bf(jnp.concatenate([v_h[c], u[c]], axis=0)) for c in chains]
        g23 = [_bf(jnp.concatenate([jnp.where(mi[c], a1[c][c64:, c64:2 * c64], 0.0),
                                    jnp.where(mi[c], -a1[c][c64:, 0:c64], 0.0)], axis=1)) for c in chains]
        y = [a1[c][c64:, 2 * c64:] + _dg(g23[c], vu[c]) for c in chains]
        for c in chains:
            q, d = groups[c // RWKV_HEADS]
            s_scr[q, d, c % RWKV_HEADS] = s[c] * et_h[c] + _dg(vu[c], upd[c], _TN)
        for gi, (q, d) in enumerate(groups):
            y_ref[d, rows_d[gi], :] = jnp.concatenate(y[gi * RWKV_HEADS:(gi + 1) * RWKV_HEADS], axis=1)
        return carry

    lax.fori_loop(0, n_chunks, body, 0)
    st_ref[...] = s_scr[...]


def _rwkv_scan(r, k, v, kap, beta, lw, s0, layer, seq, nb, row0, nseq):
    rows = nseq * seq
    rb = row0 // rows
    has_state = s0 is not None
    mode = dict(pipeline_mode=pl.Buffered(1)) if rows * RWKV_W * 4 > (2 << 20) else {}
    tok = pl.BlockSpec((rows, RWKV_W), lambda b: (rb + b, 0), **mode)
    in_specs = [tok, tok, tok, tok, tok,
                pl.BlockSpec((2, rows, RWKV_W), lambda b: (0, rb + b, 0), **mode)]
    args = [r, k, v, kap, beta, lw]
    st_shape = (2, RWKV_HEADS, RWKV_HD, RWKV_HD)
    if has_state:
        in_specs.append(pl.BlockSpec((nseq, None) + st_shape, lambda b: (b, layer, 0, 0, 0, 0)))
        args.append(s0)
    return pl.pallas_call(
        functools.partial(_rwkv_scan_kernel, seq=seq, nseq=nseq, has_state=has_state),
        grid=(nb // nseq,),
        in_specs=in_specs,
        out_specs=[
            pl.BlockSpec((2, rows, RWKV_W), lambda b: (0, b, 0), **mode),
            pl.BlockSpec((nseq,) + st_shape, lambda b: (b, 0, 0, 0, 0)),
        ],
        out_shape=[
            jax.ShapeDtypeStruct((2, nb * seq, RWKV_W), F32),
            jax.ShapeDtypeStruct((nb,) + st_shape, F32),
        ],
        scratch_shapes=[pltpu.VMEM((nseq,) + st_shape, F32)],
        compiler_params=_cparams(("arbitrary",)),
        name="rwkv_scan_lat" if has_state else "rwkv_scan_ctx",
    )(*args)


def _rwkv_out(ys, r, k, v, g, rho, lnw, lnb, bd_ref):
    bonus = _segsum(r * k * rho, bd_ref) * v
    y = ys[0] + ys[1] + bonus
    inv_n = 1.0 / RWKV_HD
    yc = y - _segsum(y, bd_ref) * inv_n
    var = _segsum(yc * yc, bd_ref) * inv_n
    yn = yc * lax.rsqrt(var + RWKV_LN_EPS)
    return _bf((yn * lnw + lnb) * g)


def _merge_kernel(oaa_ref, oab_ref, oga_ref, ogb_ref, yra_ref, yrb_ref, r_ref, k_ref, v_ref, g_ref,
                  rho_ref, lnw_ref, lnb_ref, bd_ref, gate_ref, xa_ref, xb_ref, mod_ref, nf_ref,
                  wa_ref, wg_ref, wr_ref, wo_ref, ya_ref, yb_ref, h_ref):
    d = D_MODEL
    tm = h_ref.shape[0]
    ctx = _is_ctx(pl.program_id(0), tm)
    pick = lambda a, b: jnp.where(ctx, a[...], b[...])
    o_rw = _rwkv_out(pick(yra_ref, yrb_ref), r_ref[...], k_ref[...], v_ref[...], g_ref[...],
                     rho_ref[...], lnw_ref[...], lnb_ref[...], bd_ref)
    gate = lambda k: _sigmoid(gate_ref[:, k * d:(k + 1) * d].astype(F32))
    merged = gate(0) * _dg(pick(oaa_ref, oab_ref), wa_ref[...])
    merged += gate(1) * _dg(pick(oga_ref, ogb_ref), wg_ref[...])
    merged += gate(2) * _dg(o_rw, wr_ref[...])
    y = pick(xa_ref, xb_ref) + mod_ref[2:3, :] * _dg(_bf(merged), wo_ref[...])
    h_ref[...] = (_rms(y, nf_ref[...]) * (1.0 + mod_ref[4:5, :]) + mod_ref[3:4, :]).astype(BF16)

    @pl.when(ctx)
    def _():
        ya_ref[...] = y

    @pl.when(jnp.logical_not(ctx))
    def _():
        yb_ref[...] = y


def _merge(o_att, o_gla, y_rw, rwkv_parts, rwkv_vecs, bd, gates, x, mod, norm_ffn, wa, wg, wr, wo, layer):
    tm = 256
    na = N_CTX // tm
    const = lambda r, c: pl.BlockSpec((r, c), lambda i: (0, 0), pipeline_mode=pl.Buffered(1))
    xa, xb = _ctx_lat_specs(tm, D_MODEL)
    tok = pl.BlockSpec((tm, RWKV_W), lambda i: (i, 0))
    vec = _layer_vec(RWKV_W, layer)
    return pl.pallas_call(
        _merge_kernel,
        grid=(N_TOK // tm,),
        in_specs=[
            *_ctx_lat_specs(tm, ATT_Q_W),
            *_ctx_lat_specs(tm, GLA_W),
            pl.BlockSpec((2, tm, RWKV_W), lambda i: (0, jnp.minimum(i, na - 1), 0)),
            pl.BlockSpec((2, tm, RWKV_W), lambda i: (0, jnp.maximum(i - na, 0), 0)),
            tok, tok, tok, tok, vec, vec, vec,
            pl.BlockSpec((RWKV_W, RWKV_W), lambda i: (0, 0)),
            pl.BlockSpec((tm, GATE_W), lambda i: (i, 0)),
            xa, xb,
            _mod_spec(tm, layer),
            _layer_vec(D_MODEL, layer),
            const(ATT_Q_W, D_MODEL), const(GLA_W, D_MODEL), const(RWKV_W, D_MODEL), const(D_MODEL, D_MODEL),
        ],
        out_specs=[xa, xb, pl.BlockSpec((tm, D_MODEL), lambda i: (i, 0))],
        out_shape=[jax.ShapeDtypeStruct((N_CTX, D_MODEL), F32), jax.ShapeDtypeStruct((N_LAT, D_MODEL), F32),
                   jax.ShapeDtypeStruct((N_TOK, D_MODEL), BF16)],
        compiler_params=_cparams(("arbitrary",)),
        name="merge_out",
    )(o_att[0], o_att[1], o_gla[0], o_gla[1], y_rw[0], y_rw[1], *rwkv_parts, *rwkv_vecs, bd, gates,
      x[0], x[1], mod, norm_ffn, wa, wg, wr, wo)


_FFN_TN = 512
_FFN_SUB = 256
_FFN_RB = 256


def _ffn_up_kernel(*refs, n_q):
    h_ref, hp_ref, hn_ref = refs[:3]
    groups = [refs[3 + k * (1 + n_q):3 + (k + 1) * (1 + n_q)] for k in range(3)]
    (wv_ref, *wg_refs), (cwv_ref, *cwg_refs), (cbv_ref, *cbg_refs) = groups
    o_ref, wv_scr, wg_scr = refs[3 + 3 * (1 + n_q):]
    tm = h_ref.shape[0]
    i = pl.program_id(1)
    lanes = lambda parts: jnp.concatenate([r[...] for r in parts], axis=1)

    @pl.when(i == 0)
    def _():
        wv_scr[...] = wv_ref[...].astype(BF16)
        wg_scr[...] = lanes(wg_refs).astype(BF16)

    cwv = cwv_ref[...]
    cwg = lanes(cwg_refs)
    cbv = cbv_ref[...]
    cbg = lanes(cbg_refs)

    h = h_ref[...]
    hp = hp_ref[...]
    hn = hn_ref[...]
    n_sub = _FFN_TN // _FFN_SUB
    n_rb = tm // _FFN_RB
    edges = [_seq_edges(i * tm + r * _FFN_RB, _FFN_RB) for r in range(n_rb)]

    def up(w_scr, s):
        w = w_scr[:, s * _FFN_SUB:(s + 1) * _FFN_SUB]
        return _dg(h, w), _dg(hp, w)[7:8, :], _dg(hn, w)[0:1, :]

    def conv(us, cw, cb, s, r):
        u_all, u_before, u_after = us
        cols = slice(s * _FFN_SUB, (s + 1) * _FFN_SUB)
        lo, hi = r * _FFN_RB, (r + 1) * _FFN_RB
        u = u_all[lo:hi]
        start, end = edges[r]
        before = jnp.where(start, 0.0, u_before if r == 0 else u_all[lo - 1:lo])
        after = jnp.where(end, 0.0, u_after if r == n_rb - 1 else u_all[hi:hi + 1])
        prev, nxt = _shift_rows(u, before, after)
        return cw[0:1, cols] * prev + cw[1:2, cols] * u + cw[2:3, cols] * nxt + cb[:, cols]

    ups = [(up(wv_scr, s), up(wg_scr, s)) for s in range(n_sub)]
    for s in range(n_sub):
        for r in range(n_rb):
            val = conv(ups[s][0], cwv, cbv, s, r)
            gate = conv(ups[s][1], cwg, cbg, s, r)
            o_ref[r * _FFN_RB:(r + 1) * _FFN_RB, s * _FFN_SUB:(s + 1) * _FFN_SUB] = _bf(gate * _sigmoid(gate) * val)


def _ffn_down_kernel(a_ref, w_ref, xa_ref, xb_ref, mod_ref, ya_ref, yb_ref):
    tm = a_ref.shape[0]
    ctx = _is_ctx(pl.program_id(1), tm)
    y = jnp.where(ctx, xa_ref[...], xb_ref[...]) + mod_ref[5:6, :] * _dg(a_ref[...], w_ref[...])

    @pl.when(ctx)
    def _():
        ya_ref[...] = y

    @pl.when(jnp.logical_not(ctx))
    def _():
        yb_ref[...] = y


def _ffn(h, x, mod, ffn_up, conv_w, conv_b, w_down, layer):
    tm, tn = 1024, _FFN_TN
    tu = 2048
    nj = pl.cdiv(FFN_HIDDEN, tn)
    hb = tu // 8
    gate0 = FFN_HIDDEN // LANE
    last_blk = 2 * FFN_HIDDEN // LANE - 1
    n_q = tn // LANE
    gate_blk = lambda rows, q: pl.BlockSpec(
        (None, rows, LANE), lambda j, i: (layer, 0, jnp.minimum(gate0 + n_q * j + q, last_blk)))
    val_blk = lambda rows: pl.BlockSpec((None, rows, tn), lambda j, i: (layer, 0, j))
    act = pl.pallas_call(
        functools.partial(_ffn_up_kernel, n_q=n_q),
        grid=(nj, N_TOK // tu),
        in_specs=[
            pl.BlockSpec((tu, D_MODEL), lambda j, i: (i, 0)),
            pl.BlockSpec((8, D_MODEL), lambda j, i: (jnp.maximum(i * hb - 1, 0), 0)),
            pl.BlockSpec((8, D_MODEL), lambda j, i: (jnp.minimum((i + 1) * hb, N_TOK // 8 - 1), 0)),
            val_blk(D_MODEL), *[gate_blk(D_MODEL, q) for q in range(n_q)],
            val_blk(3), *[gate_blk(3, q) for q in range(n_q)],
            val_blk(1), *[gate_blk(1, q) for q in range(n_q)],
        ],
        out_specs=pl.BlockSpec((tu, tn), lambda j, i: (i, j)),
        out_shape=jax.ShapeDtypeStruct((N_TOK, FFN_HIDDEN), BF16),
        scratch_shapes=[pltpu.VMEM((D_MODEL, tn), BF16), pltpu.VMEM((D_MODEL, tn), BF16)],
        compiler_params=_cparams(("arbitrary", "arbitrary")),
        name="ffn_up",
    )(h, h, h, *([ffn_up] * (1 + n_q)), *([conv_w] * (1 + n_q)), *([conv_b] * (1 + n_q)))
    to = 512
    xa, xb = _ctx_lat_specs(tm, to, grid_rank=2)
    return pl.pallas_call(
        _ffn_down_kernel,
        grid=(D_MODEL // to, N_TOK // tm),
        in_specs=[
            pl.BlockSpec((tm, FFN_HIDDEN), lambda n, i: (i, 0)),
            pl.BlockSpec((FFN_HIDDEN, to), lambda n, i: (0, n)),
            xa, xb,
            _mod_spec(tm, layer, width=to, grid_rank=2),
        ],
        out_specs=[xa, xb],
        out_shape=[jax.ShapeDtypeStruct((N_CTX, D_MODEL), F32), jax.ShapeDtypeStruct((N_LAT, D_MODEL), F32)],
        compiler_params=_cparams(("arbitrary", "arbitrary")),
        name="ffn_down",
    )(act, w_down, x[0], x[1], mod)


def _rope_tables(t):
    rows = t // GRID_W
    row = jnp.repeat(jnp.arange(rows, dtype=F32), GRID_W)
    col = jnp.tile(jnp.arange(GRID_W, dtype=F32), rows)
    half = HEAD_DIM // 2
    inv = ROPE_THETA ** (-jnp.arange(0, half, 2, dtype=F32) / half)

    def cos_sin(pos):
        ang = pos[:, None] * inv[None, :]
        ang = jnp.concatenate([ang, ang], axis=-1)
        return jnp.cos(ang), jnp.sin(ang)

    cos_r, sin_r = cos_sin(row)
    cos_c, sin_c = cos_sin(col)
    cos = jnp.concatenate([cos_r, cos_c], axis=-1)
    sin = jnp.concatenate([sin_r, sin_c], axis=-1)
    sign = jnp.where((jnp.arange(HEAD_DIM) % half) < half // 2, -1.0, 1.0).astype(F32)
    return cos, sin * sign[None, :]


def kernel(x_prompt, x_sample, cache_k, cache_v, state_gla, state_rwkv, c, c_ctx, w_mod, b_mod, norm_mix, w_in, q_norm, k_norm, gla_a_up, gla_a_bias, gla_norm, rwkv_mu, rwkv_w0, rwkv_w_up, rwkv_a0, rwkv_a_up, rwkv_g_up, rwkv_k_xi, rwkv_k_alpha, rwkv_bonus, rwkv_ln_w, rwkv_ln_b, w_br_att, w_br_gla, w_br_rwkv, w_out, norm_ffn, ffn_up, ffn_conv_w, ffn_conv_b, ffn_down):
    x = (x_prompt.reshape(N_CTX, D_MODEL), x_sample.reshape(N_LAT, D_MODEL))
    cvec = jnp.concatenate([c_ctx[None, :], c, jnp.zeros((N_MODROWS - 1 - DEC_BATCH, D_MODEL), F32)], axis=0)
    mod_all = _modulation(cvec, w_mod, b_mod).reshape(DEPTH, N_MODROWS, 6, D_MODEL)
    cos, sin = _rope_tables(DEC_SEQ)
    hr = lax.broadcasted_iota(jnp.int32, (RWKV_W, RWKV_W), 0) // RWKV_HD
    hc = lax.broadcasted_iota(jnp.int32, (RWKV_W, RWKV_W), 1) // RWKV_HD
    bd = (hr == hc).astype(BF16)
    w_in_t = jnp.swapaxes(w_in, 1, 2)

    vecs = lambda a: a.reshape(DEPTH, 1, -1)
    pad_rows = lambda a, n, at=0: jnp.pad(a, ((0, 0), (at, n - a.shape[1] - at), (0, 0)))
    qn, kn, gnorm, norm_mix3, norm_ffn3 = (vecs(a) for a in (q_norm, k_norm, gla_norm, norm_mix, norm_ffn))
    up = gla_a_up.reshape(DEPTH, 2, GLA_RANK, GLA_HEADS, GLA_DK).transpose(0, 1, 3, 2, 4)
    up_pad = jnp.stack([
        jnp.pad(up[:, 0], ((0, 0), (0, 0), (0, LANE - GLA_RANK), (0, 0))),
        jnp.pad(up[:, 1], ((0, 0), (0, 0), (GLA_RANK, LANE - 2 * GLA_RANK), (0, 0))),
    ], axis=1)
    gbias = gla_a_bias.reshape(DEPTH, 2, GLA_HEADS, 1, GLA_DK)
    mu = vecs(jnp.pad(rwkv_mu, ((0, 0), (0, RW_W - RWKV_COLS))))
    wuf = _bf(pad_rows(rwkv_w_up[:, 0], LANE))
    wub = _bf(pad_rows(rwkv_w_up[:, 1], LANE, at=RWKV_W_RANK))
    au = _bf(pad_rows(rwkv_a_up, LANE))
    gu = _bf(pad_rows(rwkv_g_up, 2 * LANE, at=RWKV_A_RANK))
    a0, xi, al, rho, lnw, lnb = (vecs(a) for a in (rwkv_a0, rwkv_k_xi, rwkv_k_alpha, rwkv_bonus, rwkv_ln_w, rwkv_ln_b))
    conv_b = vecs(ffn_conv_b)

    ks, vs, gla_states, rwkv_states = [], [], [], []
    for l in range(DEPTH):
        h = _prenorm(x, norm_mix3, mod_all, l, 0)
        proj = _proj(h, w_in_t, l, 0, MAIN_W, 1024, "proj_main", BF16)
        p_rw, wa, wg, wr, wo = _proj(h, w_in_t, l, COL_RW, RW_W, 512, "proj_rwkv", BF16,
                                     casts=(w_br_att, w_br_gla, w_br_rwkv, w_out))
        gates, wd = _proj(h, w_in_t, l, COL_GATE, GATE_W, GATE_TN, "proj_gates", BF16, casts=(ffn_down,))

        oa_c, k_c, v_c = _attention_ctx(proj, qn, kn, l)
        oa_l = _attention_lat(proj, qn, kn, cos, sin, cache_k, cache_v, l)
        ks.append(k_c)
        vs.append(v_c)

        og_c, gs_c = _gla(proj, up_pad, gbias, gnorm, None, l, SEQ, BATCH, 0, GLA_HEADS)
        og_l, _ = _gla(proj, up_pad, gbias, gnorm, state_gla, l, DEC_SEQ, DEC_BATCH, N_CTX, 2)
        gla_states.append(gs_c)

        r_, k_, v_, kap_, beta_, lw_, g_ = _rwkv_prep(p_rw, mu, wuf, wub, au, gu, rwkv_w0, a0, xi, al, bd, l)
        y_c, rs_c = _rwkv_scan(r_, k_, v_, kap_, beta_, lw_, None, l, SEQ, BATCH, 0, 2)
        y_l, _ = _rwkv_scan(r_, k_, v_, kap_, beta_, lw_, state_rwkv, l, DEC_SEQ, DEC_BATCH, N_CTX, 2)
        rwkv_states.append(rs_c)

        xa, xb, h2 = _merge((oa_c, oa_l), (og_c, og_l), (y_c, y_l), (r_, k_, v_, g_), (rho, lnw, lnb), bd,
                            gates, x, mod_all, norm_ffn3, wa, wg, wr, wo, l)
        x = _ffn(h2, (xa, xb), mod_all, ffn_up, ffn_conv_w, conv_b, wd, l)

    y_prompt = x[0].reshape(BATCH, SEQ, D_MODEL)
    y_sample = x[1].reshape(DEC_BATCH, DEC_SEQ, D_MODEL)
    new_cache_k = jnp.stack(ks, axis=1)
    new_cache_v = jnp.stack(vs, axis=1)
    new_state_gla = jnp.stack(gla_states, axis=1)
    new_state_rwkv = jnp.stack(rwkv_states, axis=1)
    return (y_prompt, y_sample, new_cache_k, new_cache_v, new_state_gla, new_state_rwkv)
```

```python
import functools

import jax
import jax.numpy as jnp
from jax import lax
from jax.experimental import pallas as pl
from jax.experimental.pallas import tpu as pltpu

F32 = jnp.float32
BF16 = jnp.bfloat16

D_MODEL = 2048
BATCH = 16
SEQ = 256
DEPTH = 2
DEC_BATCH = 4
DEC_SEQ = 1024
PAST_LEN = 256
GRID_W = 64
HEAD_DIM = 128
ATT_HEADS = 8
ATT_KV_HEADS = 2
ROPE_THETA = 10000.0
GLA_HEADS = 4
GLA_DK = 128
GLA_RANK = 16
GLA_TAU = 16.0
RWKV_HEADS = 8
RWKV_HD = 64
RWKV_W_RANK = 64
RWKV_A_RANK = 64
RWKV_G_RANK = 128
RWKV_DECAY_SCALE = 0.606531
RWKV_LN_EPS = 64e-5
FFN_HIDDEN = 5504
EPS = 1e-6

ATT_Q_W = ATT_HEADS * HEAD_DIM
ATT_KV_W = ATT_KV_HEADS * HEAD_DIM
GLA_W = GLA_HEADS * GLA_DK
RWKV_W = RWKV_HEADS * RWKV_HD
RWKV_COLS = 3 * RWKV_W + 2 * RWKV_W_RANK + RWKV_A_RANK + RWKV_G_RANK
IN_COLS = ATT_Q_W + 2 * ATT_KV_W + 4 * GLA_W + 2 * GLA_RANK + RWKV_COLS + 3 * D_MODEL

LANE = 128
CHUNK = 64
N_CTX = BATCH * SEQ
N_LAT = DEC_BATCH * DEC_SEQ
N_TOK = N_CTX + N_LAT
N_MODROWS = 8
VMEM_LIMIT = 56 * 1024 * 1024
_HALO = 16

COL_GLA = ATT_Q_W + 2 * ATT_KV_W
COL_GAD = COL_GLA + 4 * GLA_W
COL_RW = COL_GAD + 2 * GLA_RANK
COL_GATE = COL_RW + RWKV_COLS
MAIN_W = 4096
RW_W = 2048
GATE_TN = 768
GATE_W = 3 * D_MODEL

_NT = (((1,), (1,)), ((), ()))
_TN = (((0,), (0,)), ((), ()))
_NN = (((1,), (0,)), ((), ()))


def _bf(x):
    return x.astype(BF16)


def _dg(a, b, dims=_NN):
    return lax.dot_general(a, b, dims, preferred_element_type=F32)


def _mm(a, b, dims=_NN):
    return _dg(_bf(a), _bf(b), dims)


def _split3(x):
    h = x.astype(BF16)
    r = x - h.astype(F32)
    m = r.astype(BF16)
    return h, m, (r - m.astype(F32)).astype(BF16)


def _sigmoid(x):
    return 0.5 * jnp.tanh(0.5 * x) + 0.5


def _rms(x, g):
    return x * lax.rsqrt(jnp.mean(x * x, axis=-1, keepdims=True) + EPS) * g


def _cparams(sem):
    return pltpu.CompilerParams(dimension_semantics=sem, vmem_limit_bytes=VMEM_LIMIT)


def _mod_row(i, tm):
    n_ctx = N_CTX // tm
    return jnp.where(i < n_ctx, 0, 1 + (i - n_ctx) // (DEC_SEQ // tm))


def _ctx_lat_specs(tm, width, grid_rank=1):
    na = N_CTX // tm
    if grid_rank == 1:
        return (pl.BlockSpec((tm, width), lambda i: (jnp.minimum(i, na - 1), 0)),
                pl.BlockSpec((tm, width), lambda i: (jnp.maximum(i - na, 0), 0)))
    return (pl.BlockSpec((tm, width), lambda n, i: (jnp.minimum(i, na - 1), n)),
            pl.BlockSpec((tm, width), lambda n, i: (jnp.maximum(i - na, 0), n)))


def _is_ctx(i, tm):
    return i < N_CTX // tm


def _seq_edges(g, rows):
    lat = g - N_CTX
    start = (g < N_CTX) | ((lat & (DEC_SEQ - 1)) == 0)
    end = (g < N_CTX) | (((lat + rows) & (DEC_SEQ - 1)) == 0)
    if rows < SEQ:
        start = jnp.where(g < N_CTX, (g & (SEQ - 1)) == 0, start)
        end = jnp.where(g < N_CTX, ((g + rows) & (SEQ - 1)) == 0, end)
    return start, end


def _shift_rows(x, before, after):
    n = x.shape[0]
    row = lax.broadcasted_iota(jnp.int32, (n, 1), 0)
    prev = jnp.where(row == 0, before, pltpu.roll(x, 1, 0))
    nxt = jnp.where(row == n - 1, after, pltpu.roll(x, n - 1, 0))
    return prev, nxt


def _mod_kernel(c_ref, w_ref, b_ref, o_ref):
    c = c_ref[...]
    o_ref[...] = _mm(c * _sigmoid(c), w_ref[...]) + b_ref[...]


def _modulation(cvec, w_mod, b_mod):
    tn = 1024
    n = 6 * D_MODEL
    return pl.pallas_call(
        _mod_kernel,
        grid=(DEPTH, n // tn),
        in_specs=[
            pl.BlockSpec((N_MODROWS, D_MODEL), lambda l, j: (0, 0)),
            pl.BlockSpec((None, D_MODEL, tn), lambda l, j: (l, 0, j)),
            pl.BlockSpec((None, 1, tn), lambda l, j: (l, 0, j)),
        ],
        out_specs=pl.BlockSpec((None, N_MODROWS, tn), lambda l, j: (l, 0, j)),
        out_shape=jax.ShapeDtypeStruct((DEPTH, N_MODROWS, n), F32),
        compiler_params=_cparams(("arbitrary", "arbitrary")),
        name="modulation",
    )(cvec, w_mod, b_mod.reshape(DEPTH, 1, n))


def _prenorm_kernel(xa_ref, xb_ref, g_ref, mod_ref, o_ref, *, shift_idx):
    tm = o_ref.shape[0]
    x = jnp.where(_is_ctx(pl.program_id(0), tm), xa_ref[...], xb_ref[...])
    y = _rms(x, g_ref[...])
    sh = mod_ref[shift_idx:shift_idx + 1, :]
    sc = mod_ref[shift_idx + 1:shift_idx + 2, :]
    o_ref[...] = (y * (1.0 + sc) + sh).astype(BF16)


def _layer_vec(width, layer, grid_rank=1):
    if grid_rank == 1:
        return pl.BlockSpec((None, 1, width), lambda i: (layer, 0, 0))
    return pl.BlockSpec((None, 1, width), lambda a, b: (layer, 0, 0))


def _mod_spec(tm, layer, width=D_MODEL, grid_rank=1):
    if grid_rank == 1:
        return pl.BlockSpec((None, None, 6, width), lambda i: (layer, _mod_row(i, tm), 0, 0))
    return pl.BlockSpec((None, None, 6, width), lambda n, i: (layer, _mod_row(i, tm), 0, n))


def _prenorm(x, g_all, mod_all, layer, shift_idx):
    tm = 512
    xa, xb = _ctx_lat_specs(tm, D_MODEL)
    return pl.pallas_call(
        functools.partial(_prenorm_kernel, shift_idx=shift_idx),
        grid=(N_TOK // tm,),
        in_specs=[xa, xb, _layer_vec(D_MODEL, layer), _mod_spec(tm, layer)],
        out_specs=pl.BlockSpec((tm, D_MODEL), lambda i: (i, 0)),
        out_shape=jax.ShapeDtypeStruct((N_TOK, D_MODEL), BF16),
        compiler_params=_cparams(("arbitrary",)),
        name="prenorm",
    )(x[0], x[1], g_all, mod_all)


_CAST_ROWS = 128


def _proj_kernel(*refs, n_w, n_cast, row_shift):
    h_ref = refs[0]
    w_refs = refs[1:1 + n_w]
    cast_in = refs[1 + n_w:1 + n_w + n_cast]
    o_ref = refs[1 + n_w + n_cast]
    cast_out = refs[2 + n_w + n_cast:2 + n_w + 2 * n_cast]
    w_scr = refs[-1]

    @pl.when(pl.program_id(1) == 0)
    def _():
        w = w_refs[0][...] if n_w == 1 else jnp.concatenate([r[...] for r in w_refs], axis=0)
        w_scr[...] = w[row_shift:row_shift + w_scr.shape[0], :].astype(BF16)

    o_ref[...] = _dg(h_ref[...], w_scr[...], _NT).astype(o_ref.dtype)
    for src, dst in zip(cast_in, cast_out):
        dst[...] = src[...].astype(BF16)


def _proj(h, w_in_t, layer, col0, width, tn, name, dtype=F32, casts=(), tm=1024):
    off = col0 % tn
    base = col0 - off
    assert width % tn == 0 and off % 8 == 0 and (off == 0 or (tn % off == 0 and IN_COLS % off == 0))
    grid = (width // tn, N_TOK // tm)
    in_specs = [
        pl.BlockSpec((tm, D_MODEL), lambda j, i: (i, 0)),
        pl.BlockSpec((None, tn, D_MODEL), lambda j, i: (layer, base // tn + j, 0)),
    ]
    if off:
        in_specs.append(pl.BlockSpec((None, off, D_MODEL), lambda j, i: (layer, (base + tn * (j + 1)) // off, 0)))
    n_w = len(in_specs) - 1
    out_specs = [pl.BlockSpec((tm, tn), lambda j, i: (i, j))]
    out_shape = [jax.ShapeDtypeStruct((N_TOK, width), dtype)]
    for w in casts:
        rows, cols = w.shape[1:]
        last = rows // _CAST_ROWS - 1
        assert rows % _CAST_ROWS == 0 and last < grid[0] * grid[1]
        step = lambda j, i, last=last: jnp.minimum(j * grid[1] + i, last)
        in_specs.append(pl.BlockSpec((None, _CAST_ROWS, cols), lambda j, i, step=step: (layer, step(j, i), 0)))
        out_specs.append(pl.BlockSpec((_CAST_ROWS, cols), lambda j, i, step=step: (step(j, i), 0)))
        out_shape.append(jax.ShapeDtypeStruct((rows, cols), BF16))
    outs = pl.pallas_call(
        functools.partial(_proj_kernel, n_w=n_w, n_cast=len(casts), row_shift=off),
        grid=grid,
        in_specs=in_specs,
        out_specs=out_specs,
        out_shape=out_shape,
        scratch_shapes=[pltpu.VMEM((tn, D_MODEL), BF16)],
        compiler_params=_cparams(("arbitrary", "arbitrary")),
        name=name,
    )(h, *([w_in_t] * n_w), *casts)
    return outs[0] if not casts else outs


_ATT_SCALE = HEAD_DIM ** -0.5
_Q_PER_KV = ATT_HEADS // ATT_KV_HEADS


def _rope(x, cos, sin_signed):
    lane = lax.broadcasted_iota(jnp.int32, x.shape, 1)
    partner = jnp.where((lane & 63) < 32, pltpu.roll(x, 96, 1), pltpu.roll(x, 32, 1))
    return x * cos + partner * sin_signed


def _softmax_pv(s, v_ones):
    p = _bf(jnp.exp(s - jnp.max(s, axis=-1, keepdims=True)))
    oa = _dg(p, v_ones)
    return oa[:, :HEAD_DIM] / oa[:, HEAD_DIM:]


def _attn_ctx_kernel(q_ref, k_ref, v_ref, qn_ref, kn_ref, o_ref, ko_ref, vo_ref):
    k = _rms(k_ref[...].astype(F32), kn_ref[...])
    v = v_ref[...]
    ko_ref[...] = k
    vo_ref[...] = v.astype(F32)
    kb = _bf(k)
    v_ones = jnp.concatenate([v, jnp.ones(v.shape, BF16)], axis=1)
    heads = range(_Q_PER_KV)
    qs = [_bf(_rms(q_ref[:, h * HEAD_DIM:(h + 1) * HEAD_DIM].astype(F32), qn_ref[...])) for h in heads]
    ss = [_dg(qs[h], kb, _NT) * _ATT_SCALE for h in heads]
    o_ref[...] = jnp.concatenate([_softmax_pv(s, v_ones) for s in ss], axis=1).astype(BF16)


def _attn_lat_kernel(q_ref, k_ref, v_ref, qn_ref, kn_ref, cos_ref, sin_ref, ck_ref, cv_ref, o_ref):
    cos = cos_ref[...]
    sin = sin_ref[...]
    k = _rope(_rms(k_ref[...].astype(F32), kn_ref[...]), cos, sin)
    kb = jnp.concatenate([_bf(ck_ref[...]), _bf(k)], axis=0)
    vb = jnp.concatenate([_bf(cv_ref[...]), v_ref[...]], axis=0)
    v_ones = jnp.concatenate([vb, jnp.ones(vb.shape, BF16)], axis=1)
    rb = 256
    blocks = range(DEC_SEQ // rb)

    def scores(h):
        q = q_ref[:, h * HEAD_DIM:(h + 1) * HEAD_DIM].astype(F32)
        q = _bf(_rope(_rms(q, qn_ref[...]), cos, sin))
        return [_dg(q[r * rb:(r + 1) * rb], kb, _NT) * _ATT_SCALE for r in blocks]

    ss_next = scores(0)
    for h in range(_Q_PER_KV):
        ss = ss_next
        if h + 1 < _Q_PER_KV:
            ss_next = scores(h + 1)
        o = jnp.concatenate([_softmax_pv(s, v_ones) for s in ss], axis=0)
        o_ref[:, h * HEAD_DIM:(h + 1) * HEAD_DIM] = o.astype(BF16)


def _attention_ctx(proj, qn, kn, layer):
    blk = lambda w, f: pl.BlockSpec((SEQ, w), f)
    vec = _layer_vec(HEAD_DIM, layer, grid_rank=2)
    kv_out = pl.BlockSpec((None, None, SEQ, HEAD_DIM), lambda b, g: (b, g, 0, 0))
    qw = _Q_PER_KV * HEAD_DIM
    return pl.pallas_call(
        _attn_ctx_kernel,
        grid=(BATCH, ATT_KV_HEADS),
        in_specs=[
            blk(qw, lambda b, g: (b, g)),
            blk(HEAD_DIM, lambda b, g: (b, ATT_HEADS + g)),
            blk(HEAD_DIM, lambda b, g: (b, ATT_HEADS + ATT_KV_HEADS + g)),
            vec, vec,
        ],
        out_specs=[blk(qw, lambda b, g: (b, g)), kv_out, kv_out],
        out_shape=[
            jax.ShapeDtypeStruct((N_CTX, ATT_Q_W), BF16),
            jax.ShapeDtypeStruct((BATCH, ATT_KV_HEADS, SEQ, HEAD_DIM), F32),
            jax.ShapeDtypeStruct((BATCH, ATT_KV_HEADS, SEQ, HEAD_DIM), F32),
        ],
        compiler_params=_cparams(("arbitrary", "arbitrary")),
        name="attention_ctx",
    )(proj, proj, proj, qn, kn)


def _attention_lat(proj, qn, kn, cos, sin, cache_k, cache_v, layer):
    rb = N_CTX // DEC_SEQ
    qw = _Q_PER_KV * HEAD_DIM
    blk = lambda w, f: pl.BlockSpec((DEC_SEQ, w), f)
    vec = _layer_vec(HEAD_DIM, layer, grid_rank=2)
    tab = pl.BlockSpec((DEC_SEQ, HEAD_DIM), lambda b, g: (0, 0))
    cache = pl.BlockSpec((None, None, None, PAST_LEN, HEAD_DIM), lambda b, g: (b, layer, g, 0, 0))
    return pl.pallas_call(
        _attn_lat_kernel,
        grid=(DEC_BATCH, ATT_KV_HEADS),
        in_specs=[
            blk(qw, lambda b, g: (rb + b, g)),
            blk(HEAD_DIM, lambda b, g: (rb + b, ATT_HEADS + g)),
            blk(HEAD_DIM, lambda b, g: (rb + b, ATT_HEADS + ATT_KV_HEADS + g)),
            vec, vec, tab, tab, cache, cache,
        ],
        out_specs=blk(qw, lambda b, g: (b, g)),
        out_shape=jax.ShapeDtypeStruct((N_LAT, ATT_Q_W), BF16),
        compiler_params=_cparams(("arbitrary", "arbitrary")),
        name="attention_lat",
    )(proj, proj, proj, qn, kn, cos, sin, cache_k, cache_v)


def _tri(n, upper):
    r = lax.broadcasted_iota(jnp.int32, (n, n), 0)
    c = lax.broadcasted_iota(jnp.int32, (n, n), 1)
    return (r <= c) if upper else (r >= c)


def _gla_kernel(*refs, seq, has_state):
    if has_state:
        q_ref, k_ref, v_ref, gg_ref, gad_ref, up_ref, bias_ref, gn_ref, s0_ref, o_ref, st_ref = refs
    else:
        q_ref, k_ref, v_ref, gg_ref, gad_ref, up_ref, bias_ref, gn_ref, o_ref, st_ref = refs
        s0_ref = None
    nc = seq // CHUNK
    nh = q_ref.shape[1] // LANE
    c3 = (nc, CHUNK, LANE)
    stack = lambda x: jnp.concatenate([x[:, h * LANE:(h + 1) * LANE].reshape(c3) for h in range(nh)], axis=0)
    gad = gad_ref[...]
    q3 = stack(q_ref[...].astype(F32) * (GLA_DK ** -0.5))
    k3 = stack(k_ref[...].astype(F32))
    v3 = stack(v_ref[...])
    bdot = lambda a, b, ca, cb: lax.dot_general(a, b, (((ca,), (cb,)), ((0,), (0,))), preferred_element_type=F32)

    qe, oi, kv, dec = [], [], [], []
    for d in range(2):
        incl = _tri(CHUNK, d == 1)
        tri = jnp.broadcast_to(jnp.where(incl, 1.0, 0.0).astype(BF16)[None], (nh * nc, CHUNK, CHUNK))
        la = jnp.concatenate(
            [(jax.nn.log_sigmoid(_dg(gad, _bf(up_ref[d, h])) + bias_ref[d, h]) / GLA_TAU).reshape(c3)
             for h in range(nh)], axis=0)
        hi, mid, lo = _split3(la)
        cum = bdot(tri, hi, 2, 1) + (bdot(tri, mid, 2, 1) + bdot(tri, lo, 2, 1))
        tot = cum[:, 0:1, :] if d else cum[:, CHUNK - 1:CHUNK, :]
        qe_d = _bf(q3 * jnp.exp(cum))
        ke = _bf(k3 * jnp.exp(-cum))
        kl = _bf(k3 * jnp.exp(tot - cum))
        att = jnp.where(incl[None], bdot(qe_d, ke, 2, 2), 0.0)
        qe.append(qe_d)
        oi.append(bdot(_bf(att), v3, 2, 1))
        kv.append(bdot(v3, kl, 1, 1))
        dec.append(jnp.exp(tot))

    zero = jnp.zeros((GLA_DK, GLA_DK), F32)
    st = [[s0_ref[d, h].T if has_state else zero for h in range(nh)] for d in range(2)]
    o_f = [[None] * nc for _ in range(nh)]
    o_b = [[None] * nc for _ in range(nh)]
    for t in range(nc):
        nf, nb = t, nc - 1 - t
        for h in range(nh):
            o_f[h][nf] = oi[0][h * nc + nf] + _dg(qe[0][h * nc + nf], _bf(st[0][h]), _NT)
            o_b[h][nb] = oi[1][h * nc + nb] + _dg(qe[1][h * nc + nb], _bf(st[1][h]), _NT)
        for h in range(nh):
            st[0][h] = st[0][h] * dec[0][h * nc + nf] + kv[0][h * nc + nf]
            st[1][h] = st[1][h] * dec[1][h * nc + nb] + kv[1][h * nc + nb]
    outs = []
    for h in range(nh):
        st_ref[0, h] = st[0][h].T
        st_ref[1, h] = st[1][h].T
        outs.append(_rms(jnp.concatenate([a + b for a, b in zip(o_f[h], o_b[h])], axis=0), gn_ref[...]))
    g = gg_ref[...].astype(F32)
    o_ref[...] = (jnp.concatenate(outs, axis=1) * (g * _sigmoid(g))).astype(o_ref.dtype)


def _gla(proj, up_pad, bias, gnorm, s0, layer, seq, nb, row0, nh):
    rb = row0 // seq
    w = nh * LANE
    c0 = COL_GLA // w
    per = GLA_HEADS // nh
    blk = lambda c: pl.BlockSpec((seq, w), lambda b, h: (rb + b, c0 + c * per + h))
    has_state = s0 is not None
    in_specs = [
        blk(0), blk(1), blk(2), blk(3),
        pl.BlockSpec((seq, LANE), lambda b, h: (rb + b, COL_GAD // LANE)),
        pl.BlockSpec((None, 2, nh, LANE, LANE), lambda b, h: (layer, 0, h, 0, 0)),
        pl.BlockSpec((None, 2, nh, 1, LANE), lambda b, h: (layer, 0, h, 0, 0)),
        _layer_vec(LANE, layer, grid_rank=2),
    ]
    args = [proj, proj, proj, proj, proj, up_pad, bias, gnorm]
    if has_state:
        in_specs.append(pl.BlockSpec((None, None, 2, nh, GLA_DK, GLA_DK),
                                     lambda b, h: (b, layer, 0, h, 0, 0)))
        args.append(s0)
    return pl.pallas_call(
        functools.partial(_gla_kernel, seq=seq, has_state=has_state),
        grid=(nb, per),
        in_specs=in_specs,
        out_specs=[
            pl.BlockSpec((seq, w), lambda b, h: (b, h)),
            pl.BlockSpec((None, 2, nh, GLA_DK, GLA_DK), lambda b, h: (b, 0, h, 0, 0)),
        ],
        out_shape=[
            jax.ShapeDtypeStruct((nb * seq, GLA_W), BF16),
            jax.ShapeDtypeStruct((nb, 2, GLA_HEADS, GLA_DK, GLA_DK), F32),
        ],
        compiler_params=_cparams(("arbitrary", "arbitrary")),
        name="gla_lat" if has_state else "gla_ctx",
    )(*args)


def _segsum(x, bd_ref):
    h = x.astype(BF16)
    m = (x - h.astype(F32)).astype(BF16)
    bd = bd_ref[...]
    return _dg(h, bd) + _dg(m, bd)


def _cumsum(tri_bf, x):
    h, m, l = _split3(x)
    return _dg(tri_bf, h) + (_dg(tri_bf, m) + _dg(tri_bf, l))


def _rwkv_prep_kernel(x_ref, xp_ref, xn_ref, mu_ref, wuf_ref, wub_ref, au_ref,
                      gu_ref, w0_ref, a0_ref, xi_ref, al_ref, bd_ref,
                      r_ref, k_ref, v_ref, kap_ref, beta_ref, lw_ref, g_ref):
    tm = x_ref.shape[0]
    start, end = _seq_edges(pl.program_id(0) * tm, tm)
    x = x_ref[...].astype(F32)
    before = jnp.where(start, 0.0, xp_ref[...].astype(F32)[_HALO - 1:_HALO, :])
    after = jnp.where(end, 0.0, xn_ref[...].astype(F32)[0:1, :])
    prev, nxt = _shift_rows(x, before, after)
    rw = x + (0.5 * (prev + nxt) - x) * mu_ref[...]
    w = RWKV_W
    rr = rw[:, 0:w]
    rk = rw[:, w:2 * w]
    rv = rw[:, 2 * w:3 * w]
    rwd = _bf(jnp.tanh(rw[:, 3 * w:3 * w + LANE]))
    rad = _bf(rw[:, 3 * w + LANE:3 * w + 2 * LANE])
    rgd = _bf(_sigmoid(rw[:, 3 * w + LANE:3 * w + 3 * LANE]))
    lw_ref[0] = -RWKV_DECAY_SCALE * _sigmoid(w0_ref[0:1, :] + _dg(rwd, wuf_ref[...]))
    lw_ref[1] = -RWKV_DECAY_SCALE * _sigmoid(w0_ref[1:2, :] + _dg(rwd, wub_ref[...]))
    a = _sigmoid(a0_ref[...] + _dg(rad, au_ref[...]))
    g_ref[...] = _dg(rgd, gu_ref[...])
    kap = rk * xi_ref[...]
    kap = kap * lax.rsqrt(_segsum(kap * kap, bd_ref) + EPS)
    r_ref[...] = rr
    k_ref[...] = rk * (1.0 + (a - 1.0) * al_ref[...])
    v_ref[...] = rv
    kap_ref[...] = kap
    beta_ref[...] = kap * a


def _rwkv_prep(proj, mu, wuf, wub, au, gu, w0, a0, xi, al, bd, layer):
    tm = 256
    nt = N_TOK // tm
    hb = tm // _HALO
    full = lambda shape: (pl.BlockSpec((None,) + shape, lambda i: (layer,) + (0,) * len(shape)))
    tok = pl.BlockSpec((tm, RWKV_W), lambda i: (i, 0))
    tshape = jax.ShapeDtypeStruct((N_TOK, RWKV_W), F32)
    return pl.pallas_call(
        _rwkv_prep_kernel,
        grid=(nt,),
        in_specs=[
            pl.BlockSpec((tm, RW_W), lambda i: (i, 0)),
            pl.BlockSpec((_HALO, RW_W), lambda i: (jnp.maximum(i * hb - 1, 0), 0)),
            pl.BlockSpec((_HALO, RW_W), lambda i: (jnp.minimum((i + 1) * hb, N_TOK // _HALO - 1), 0)),
            full((1, RW_W)),
            full((LANE, RWKV_W)), full((LANE, RWKV_W)), full((LANE, RWKV_W)), full((2 * LANE, RWKV_W)),
            full((2, RWKV_W)), full((1, RWKV_W)), full((1, RWKV_W)), full((1, RWKV_W)),
            pl.BlockSpec((RWKV_W, RWKV_W), lambda i: (0, 0)),
        ],
        out_specs=[tok, tok, tok, tok, tok,
                   pl.BlockSpec((2, tm, RWKV_W), lambda i: (0, i, 0)), tok],
        out_shape=[tshape, tshape, tshape, tshape, tshape,
                   jax.ShapeDtypeStruct((2, N_TOK, RWKV_W), F32), tshape],
        compiler_params=_cparams(("arbitrary",)),
        name="rwkv_prep",
    )(proj, proj, proj, mu, wuf, wub, au, gu, w0, a0, xi, al, bd)


def _unit_tri_inverse_corr(lows):
    c = [-low for low in lows]
    pb = [_bf(low) for low in lows]
    p = [_dg(b, b) for b in pb]
    span = 2
    while span < CHUNK:
        pb = [_bf(x) for x in p]
        c = [a + x + _dg(_bf(a), xb) for a, x, xb in zip(c, p, pb)]
        span *= 2
        if span < CHUNK:
            p = [_dg(xb, xb) for xb in pb]
    return c


def _rwkv_scan_kernel(*refs, seq, nseq, has_state):
    if has_state:
        r_ref, k_ref, v_ref, kap_ref, beta_ref, lw_ref, s0_ref, y_ref, st_ref, s_scr = refs
    else:
        r_ref, k_ref, v_ref, kap_ref, beta_ref, lw_ref, y_ref, st_ref, s_scr = refs
    n_chunks = seq // CHUNK
    c64 = CHUNK
    rr = lax.broadcasted_iota(jnp.int32, (c64, c64), 0)
    cc = lax.broadcasted_iota(jnp.int32, (c64, c64), 1)
    incl = (rr >= cc, rr <= cc)
    strict = (rr > cc, rr < cc)
    tri = tuple(jnp.where(m, 1.0, 0.0).astype(BF16) for m in incl)
    if has_state:
        s_scr[...] = s0_ref[...]
    else:
        s_scr[...] = jnp.zeros_like(s_scr)
    heads = range(RWKV_HEADS)
    cut = lambda x: [x[:, h * RWKV_HD:(h + 1) * RWKV_HD] for h in heads]
    groups = [(q, d) for q in range(nseq) for d in range(2)]

    def body(i, carry):
        lhs, rhs, upd, v_h, et_h, s, ms, mi, rows_d = [], [], [], [], [], [], [], [], []
        for q, d in groups:
            n = (n_chunks - 1 - i) if d else i
            rows = pl.ds(pl.multiple_of(q * seq + n * c64, c64), c64)
            lw = lw_ref[d, rows, :]
            cum = _cumsum(tri[d], lw)
            tot = cum[0:1, :] if d else cum[c64 - 1:c64, :]
            e_neg = jnp.exp(-cum)
            e_tot = jnp.exp(tot)
            kt = kap_ref[rows, :] * jnp.exp(cum - lw)
            bt = beta_ref[rows, :] * e_neg
            kk = k_ref[rows, :] * e_neg
            rt = r_ref[rows, :] * jnp.exp(cum)
            lhs += cut(_bf(jnp.concatenate([kt, rt], axis=0)))
            rhs += cut(_bf(jnp.concatenate([bt, kk], axis=0)))
            upd += cut(_bf(jnp.concatenate([kk * e_tot, -(bt * e_tot)], axis=0)))
            v_h += cut(v_ref[rows, :])
            et_h += cut(e_tot)
            s += [s_scr[q, d, h] for h in heads]
            ms += [strict[d]] * RWKV_HEADS
            mi += [incl[d]] * RWKV_HEADS
            rows_d.append(rows)
        chains = range(len(groups) * RWKV_HEADS)
        a1 = [_dg(lhs[c], jnp.concatenate([rhs[c], _bf(s[c])], axis=0), _NT) for c in chains]
        low = [jnp.where(ms[c], a1[c][:c64, 0:c64], 0.0) for c in chains]
        g1 = [jnp.where(ms[c], a1[c][:c64, c64:2 * c64], 0.0) for c in chains]
        corr = _unit_tri_inverse_corr(low)
        z = [a1[c][:c64, 2 * c64:] + _mm(g1[c], v_h[c]) for c in chains]
        u = [z[c] + _mm(corr[c], z[c]) for c in chains]
        vu = [_bf(jnp.concatenate([v_h[c], u[c]], axis=0)) for c in chains]
        g23 = [_bf(jnp.concatenate([jnp.where(mi[c], a1[c][c64:, c64:2 * c64], 0.0),
                                    jnp.where(mi[c], -a1[c][c64:, 0:c64], 0.0)], axis=1)) for c in chains]
        y = [a1[c][c64:, 2 * c64:] + _dg(g23[c], vu[c]) for c in chains]
        for c in chains:
            q, d = groups[c // RWKV_HEADS]
            s_scr[q, d, c % RWKV_HEADS] = s[c] * et_h[c] + _dg(vu[c], upd[c], _TN)
        for gi, (q, d) in enumerate(groups):
            y_ref[d, rows_d[gi], :] = jnp.concatenate(y[gi * RWKV_HEADS:(gi + 1) * RWKV_HEADS], axis=1)
        return carry

    lax.fori_loop(0, n_chunks, body, 0)
    st_ref[...] = s_scr[...]


def _rwkv_scan(r, k, v, kap, beta, lw, s0, layer, seq, nb, row0, nseq):
    rows = nseq * seq
    rb = row0 // rows
    has_state = s0 is not None
    mode = dict(pipeline_mode=pl.Buffered(1)) if rows * RWKV_W * 4 > (2 << 20) else {}
    tok = pl.BlockSpec((rows, RWKV_W), lambda b: (rb + b, 0), **mode)
    in_specs = [tok, tok, tok, tok, tok,
                pl.BlockSpec((2, rows, RWKV_W), lambda b: (0, rb + b, 0), **mode)]
    args = [r, k, v, kap, beta, lw]
    st_shape = (2, RWKV_HEADS, RWKV_HD, RWKV_HD)
    if has_state:
        in_specs.append(pl.BlockSpec((nseq, None) + st_shape, lambda b: (b, layer, 0, 0, 0, 0)))
        args.append(s0)
    return pl.pallas_call(
        functools.partial(_rwkv_scan_kernel, seq=seq, nseq=nseq, has_state=has_state),
        grid=(nb // nseq,),
        in_specs=in_specs,
        out_specs=[
            pl.BlockSpec((2, rows, RWKV_W), lambda b: (0, b, 0), **mode),
            pl.BlockSpec((nseq,) + st_shape, lambda b: (b, 0, 0, 0, 0)),
        ],
        out_shape=[
            jax.ShapeDtypeStruct((2, nb * seq, RWKV_W), F32),
            jax.ShapeDtypeStruct((nb,) + st_shape, F32),
        ],
        scratch_shapes=[pltpu.VMEM((nseq,) + st_shape, F32)],
        compiler_params=_cparams(("arbitrary",)),
        name="rwkv_scan_lat" if has_state else "rwkv_scan_ctx",
    )(*args)


def _rwkv_out(ys, r, k, v, g, rho, lnw, lnb, bd_ref):
    bonus = _segsum(r * k * rho, bd_ref) * v
    y = ys[0] + ys[1] + bonus
    inv_n = 1.0 / RWKV_HD
    yc = y - _segsum(y, bd_ref) * inv_n
    var = _segsum(yc * yc, bd_ref) * inv_n
    yn = yc * lax.rsqrt(var + RWKV_LN_EPS)
    return _bf((yn * lnw + lnb) * g)


def _merge_kernel(oaa_ref, oab_ref, oga_ref, ogb_ref, yra_ref, yrb_ref, r_ref, k_ref, v_ref, g_ref,
                  rho_ref, lnw_ref, lnb_ref, bd_ref, gate_ref, xa_ref, xb_ref, mod_ref, nf_ref,
                  wa_ref, wg_ref, wr_ref, wo_ref, ya_ref, yb_ref, h_ref):
    d = D_MODEL
    tm = h_ref.shape[0]
    ctx = _is_ctx(pl.program_id(0), tm)
    pick = lambda a, b: jnp.where(ctx, a[...], b[...])
    o_rw = _rwkv_out(pick(yra_ref, yrb_ref), r_ref[...], k_ref[...], v_ref[...], g_ref[...],
                     rho_ref[...], lnw_ref[...], lnb_ref[...], bd_ref)
    gate = lambda k: _sigmoid(gate_ref[:, k * d:(k + 1) * d].astype(F32))
    merged = gate(0) * _dg(pick(oaa_ref, oab_ref), wa_ref[...])
    merged += gate(1) * _dg(pick(oga_ref, ogb_ref), wg_ref[...])
    merged += gate(2) * _dg(o_rw, wr_ref[...])
    y = pick(xa_ref, xb_ref) + mod_ref[2:3, :] * _dg(_bf(merged), wo_ref[...])
    h_ref[...] = (_rms(y, nf_ref[...]) * (1.0 + mod_ref[4:5, :]) + mod_ref[3:4, :]).astype(BF16)

    @pl.when(ctx)
    def _():
        ya_ref[...] = y

    @pl.when(jnp.logical_not(ctx))
    def _():
        yb_ref[...] = y


def _merge(o_att, o_gla, y_rw, rwkv_parts, rwkv_vecs, bd, gates, x, mod, norm_ffn, wa, wg, wr, wo, layer):
    tm = 256
    na = N_CTX // tm
    const = lambda r, c: pl.BlockSpec((r, c), lambda i: (0, 0), pipeline_mode=pl.Buffered(1))
    xa, xb = _ctx_lat_specs(tm, D_MODEL)
    tok = pl.BlockSpec((tm, RWKV_W), lambda i: (i, 0))
    vec = _layer_vec(RWKV_W, layer)
    return pl.pallas_call(
        _merge_kernel,
        grid=(N_TOK // tm,),
        in_specs=[
            *_ctx_lat_specs(tm, ATT_Q_W),
            *_ctx_lat_specs(tm, GLA_W),
            pl.BlockSpec((2, tm, RWKV_W), lambda i: (0, jnp.minimum(i, na - 1), 0)),
            pl.BlockSpec((2, tm, RWKV_W), lambda i: (0, jnp.maximum(i - na, 0), 0)),
            tok, tok, tok, tok, vec, vec, vec,
            pl.BlockSpec((RWKV_W, RWKV_W), lambda i: (0, 0)),
            pl.BlockSpec((tm, GATE_W), lambda i: (i, 0)),
            xa, xb,
            _mod_spec(tm, layer),
            _layer_vec(D_MODEL, layer),
            const(ATT_Q_W, D_MODEL), const(GLA_W, D_MODEL), const(RWKV_W, D_MODEL), const(D_MODEL, D_MODEL),
        ],
        out_specs=[xa, xb, pl.BlockSpec((tm, D_MODEL), lambda i: (i, 0))],
        out_shape=[jax.ShapeDtypeStruct((N_CTX, D_MODEL), F32), jax.ShapeDtypeStruct((N_LAT, D_MODEL), F32),
                   jax.ShapeDtypeStruct((N_TOK, D_MODEL), BF16)],
        compiler_params=_cparams(("arbitrary",)),
        name="merge_out",
    )(o_att[0], o_att[1], o_gla[0], o_gla[1], y_rw[0], y_rw[1], *rwkv_parts, *rwkv_vecs, bd, gates,
      x[0], x[1], mod, norm_ffn, wa, wg, wr, wo)


_FFN_TN = 512
_FFN_SUB = 256
_FFN_RB = 256


def _ffn_up_kernel(*refs, n_q):
    h_ref, hp_ref, hn_ref = refs[:3]
    groups = [refs[3 + k * (1 + n_q):3 + (k + 1) * (1 + n_q)] for k in range(3)]
    (wv_ref, *wg_refs), (cwv_ref, *cwg_refs), (cbv_ref, *cbg_refs) = groups
    o_ref, wv_scr, wg_scr = refs[3 + 3 * (1 + n_q):]
    tm = h_ref.shape[0]
    i = pl.program_id(1)
    lanes = lambda parts: jnp.concatenate([r[...] for r in parts], axis=1)

    @pl.when(i == 0)
    def _():
        wv_scr[...] = wv_ref[...].astype(BF16)
        wg_scr[...] = lanes(wg_refs).astype(BF16)

    cwv = cwv_ref[...]
    cwg = lanes(cwg_refs)
    cbv = cbv_ref[...]
    cbg = lanes(cbg_refs)

    h = h_ref[...]
    hp = hp_ref[...]
    hn = hn_ref[...]
    n_sub = _FFN_TN // _FFN_SUB
    n_rb = tm // _FFN_RB
    edges = [_seq_edges(i * tm + r * _FFN_RB, _FFN_RB) for r in range(n_rb)]

    def up(w_scr, s):
        w = w_scr[:, s * _FFN_SUB:(s + 1) * _FFN_SUB]
        return _dg(h, w), _dg(hp, w)[7:8, :], _dg(hn, w)[0:1, :]

    def conv(us, cw, cb, s, r):
        u_all, u_before, u_after = us
        cols = slice(s * _FFN_SUB, (s + 1) * _FFN_SUB)
        lo, hi = r * _FFN_RB, (r + 1) * _FFN_RB
        u = u_all[lo:hi]
        start, end = edges[r]
        before = jnp.where(start, 0.0, u_before if r == 0 else u_all[lo - 1:lo])
        after = jnp.where(end, 0.0, u_after if r == n_rb - 1 else u_all[hi:hi + 1])
        prev, nxt = _shift_rows(u, before, after)
        return cw[0:1, cols] * prev + cw[1:2, cols] * u + cw[2:3, cols] * nxt + cb[:, cols]

    ups = [(up(wv_scr, s), up(wg_scr, s)) for s in range(n_sub)]
    for s in range(n_sub):
        for r in range(n_rb):
            val = conv(ups[s][0], cwv, cbv, s, r)
            gate = conv(ups[s][1], cwg, cbg, s, r)
            o_ref[r * _FFN_RB:(r + 1) * _FFN_RB, s * _FFN_SUB:(s + 1) * _FFN_SUB] = _bf(gate * _sigmoid(gate) * val)


def _ffn_down_kernel(a_ref, w_ref, xa_ref, xb_ref, mod_ref, ya_ref, yb_ref):
    tm = a_ref.shape[0]
    ctx = _is_ctx(pl.program_id(1), tm)
    y = jnp.where(ctx, xa_ref[...], xb_ref[...]) + mod_ref[5:6, :] * _dg(a_ref[...], w_ref[...])

    @pl.when(ctx)
    def _():
        ya_ref[...] = y

    @pl.when(jnp.logical_not(ctx))
    def _():
        yb_ref[...] = y


def _ffn(h, x, mod, ffn_up, conv_w, conv_b, w_down, layer):
    tm, tn = 1024, _FFN_TN
    tu = 2048
    nj = pl.cdiv(FFN_HIDDEN, tn)
    hb = tu // 8
    gate0 = FFN_HIDDEN // LANE
    last_blk = 2 * FFN_HIDDEN // LANE - 1
    n_q = tn // LANE
    gate_blk = lambda rows, q: pl.BlockSpec(
        (None, rows, LANE), lambda j, i: (layer, 0, jnp.minimum(gate0 + n_q * j + q, last_blk)))
    val_blk = lambda rows: pl.BlockSpec((None, rows, tn), lambda j, i: (layer, 0, j))
    act = pl.pallas_call(
        functools.partial(_ffn_up_kernel, n_q=n_q),
        grid=(nj, N_TOK // tu),
        in_specs=[
            pl.BlockSpec((tu, D_MODEL), lambda j, i: (i, 0)),
            pl.BlockSpec((8, D_MODEL), lambda j, i: (jnp.maximum(i * hb - 1, 0), 0)),
            pl.BlockSpec((8, D_MODEL), lambda j, i: (jnp.minimum((i + 1) * hb, N_TOK // 8 - 1), 0)),
            val_blk(D_MODEL), *[gate_blk(D_MODEL, q) for q in range(n_q)],
            val_blk(3), *[gate_blk(3, q) for q in range(n_q)],
            val_blk(1), *[gate_blk(1, q) for q in range(n_q)],
        ],
        out_specs=pl.BlockSpec((tu, tn), lambda j, i: (i, j)),
        out_shape=jax.ShapeDtypeStruct((N_TOK, FFN_HIDDEN), BF16),
        scratch_shapes=[pltpu.VMEM((D_MODEL, tn), BF16), pltpu.VMEM((D_MODEL, tn), BF16)],
        compiler_params=_cparams(("arbitrary", "arbitrary")),
        name="ffn_up",
    )(h, h, h, *([ffn_up] * (1 + n_q)), *([conv_w] * (1 + n_q)), *([conv_b] * (1 + n_q)))
    to = 512
    xa, xb = _ctx_lat_specs(tm, to, grid_rank=2)
    return pl.pallas_call(
        _ffn_down_kernel,
        grid=(D_MODEL // to, N_TOK // tm),
        in_specs=[
            pl.BlockSpec((tm, FFN_HIDDEN), lambda n, i: (i, 0)),
            pl.BlockSpec((FFN_HIDDEN, to), lambda n, i: (0, n)),
            xa, xb,
            _mod_spec(tm, layer, width=to, grid_rank=2),
        ],
        out_specs=[xa, xb],
        out_shape=[jax.ShapeDtypeStruct((N_CTX, D_MODEL), F32), jax.ShapeDtypeStruct((N_LAT, D_MODEL), F32)],
        compiler_params=_cparams(("arbitrary", "arbitrary")),
        name="ffn_down",
    )(act, w_down, x[0], x[1], mod)


def _rope_tables(t):
    rows = t // GRID_W
    row = jnp.repeat(jnp.arange(rows, dtype=F32), GRID_W)
    col = jnp.tile(jnp.arange(GRID_W, dtype=F32), rows)
    half = HEAD_DIM // 2
    inv = ROPE_THETA ** (-jnp.arange(0, half, 2, dtype=F32) / half)

    def cos_sin(pos):
        ang = pos[:, None] * inv[None, :]
        ang = jnp.concatenate([ang, ang], axis=-1)
        return jnp.cos(ang), jnp.sin(ang)

    cos_r, sin_r = cos_sin(row)
    cos_c, sin_c = cos_sin(col)
    cos = jnp.concatenate([cos_r, cos_c], axis=-1)
    sin = jnp.concatenate([sin_r, sin_c], axis=-1)
    sign = jnp.where((jnp.arange(HEAD_DIM) % half) < half // 2, -1.0, 1.0).astype(F32)
    return cos, sin * sign[None, :]


def kernel(x_prompt, x_sample, cache_k, cache_v, state_gla, state_rwkv, c, c_ctx, w_mod, b_mod, norm_mix, w_in, q_norm, k_norm, gla_a_up, gla_a_bias, gla_norm, rwkv_mu, rwkv_w0, rwkv_w_up, rwkv_a0, rwkv_a_up, rwkv_g_up, rwkv_k_xi, rwkv_k_alpha, rwkv_bonus, rwkv_ln_w, rwkv_ln_b, w_br_att, w_br_gla, w_br_rwkv, w_out, norm_ffn, ffn_up, ffn_conv_w, ffn_conv_b, ffn_down):
    x = (x_prompt.reshape(N_CTX, D_MODEL), x_sample.reshape(N_LAT, D_MODEL))
    cvec = jnp.concatenate([c_ctx[None, :], c, jnp.zeros((N_MODROWS - 1 - DEC_BATCH, D_MODEL), F32)], axis=0)
    mod_all = _modulation(cvec, w_mod, b_mod).reshape(DEPTH, N_MODROWS, 6, D_MODEL)
    cos, sin = _rope_tables(DEC_SEQ)
    hr = lax.broadcasted_iota(jnp.int32, (RWKV_W, RWKV_W), 0) // RWKV_HD
    hc = lax.broadcasted_iota(jnp.int32, (RWKV_W, RWKV_W), 1) // RWKV_HD
    bd = (hr == hc).astype(BF16)
    w_in_t = jnp.swapaxes(w_in, 1, 2)

    vecs = lambda a: a.reshape(DEPTH, 1, -1)
    pad_rows = lambda a, n, at=0: jnp.pad(a, ((0, 0), (at, n - a.shape[1] - at), (0, 0)))
    qn, kn, gnorm, norm_mix3, norm_ffn3 = (vecs(a) for a in (q_norm, k_norm, gla_norm, norm_mix, norm_ffn))
    up = gla_a_up.reshape(DEPTH, 2, GLA_RANK, GLA_HEADS, GLA_DK).transpose(0, 1, 3, 2, 4)
    up_pad = jnp.stack([
        jnp.pad(up[:, 0], ((0, 0), (0, 0), (0, LANE - GLA_RANK), (0, 0))),
        jnp.pad(up[:, 1], ((0, 0), (0, 0), (GLA_RANK, LANE - 2 * GLA_RANK), (0, 0))),
    ], axis=1)
    gbias = gla_a_bias.reshape(DEPTH, 2, GLA_HEADS, 1, GLA_DK)
    mu = vecs(jnp.pad(rwkv_mu, ((0, 0), (0, RW_W - RWKV_COLS))))
    wuf = _bf(pad_rows(rwkv_w_up[:, 0], LANE))
    wub = _bf(pad_rows(rwkv_w_up[:, 1], LANE, at=RWKV_W_RANK))
    au = _bf(pad_rows(rwkv_a_up, LANE))
    gu = _bf(pad_rows(rwkv_g_up, 2 * LANE, at=RWKV_A_RANK))
    a0, xi, al, rho, lnw, lnb = (vecs(a) for a in (rwkv_a0, rwkv_k_xi, rwkv_k_alpha, rwkv_bonus, rwkv_ln_w, rwkv_ln_b))
    conv_b = vecs(ffn_conv_b)

    ks, vs, gla_states, rwkv_states = [], [], [], []
    for l in range(DEPTH):
        h = _prenorm(x, norm_mix3, mod_all, l, 0)
        proj = _proj(h, w_in_t, l, 0, MAIN_W, 1024, "proj_main", BF16, tm=2048)
        p_rw, wa, wg, wr, wo = _proj(h, w_in_t, l, COL_RW, RW_W, 512, "proj_rwkv", BF16,
                                     casts=(w_br_att, w_br_gla, w_br_rwkv, w_out), tm=2048)
        gates, wd = _proj(h, w_in_t, l, COL_GATE, GATE_W, GATE_TN, "proj_gates", BF16, casts=(ffn_down,))

        oa_c, k_c, v_c = _attention_ctx(proj, qn, kn, l)
        oa_l = _attention_lat(proj, qn, kn, cos, sin, cache_k, cache_v, l)
        ks.append(k_c)
        vs.append(v_c)

        og_c, gs_c = _gla(proj, up_pad, gbias, gnorm, None, l, SEQ, BATCH, 0, GLA_HEADS)
        og_l, _ = _gla(proj, up_pad, gbias, gnorm, state_gla, l, DEC_SEQ, DEC_BATCH, N_CTX, 2)
        gla_states.append(gs_c)

        r_, k_, v_, kap_, beta_, lw_, g_ = _rwkv_prep(p_rw, mu, wuf, wub, au, gu, rwkv_w0, a0, xi, al, bd, l)
        y_c, rs_c = _rwkv_scan(r_, k_, v_, kap_, beta_, lw_, None, l, SEQ, BATCH, 0, 2)
        y_l, _ = _rwkv_scan(r_, k_, v_, kap_, beta_, lw_, state_rwkv, l, DEC_SEQ, DEC_BATCH, N_CTX, 2)
        rwkv_states.append(rs_c)

        xa, xb, h2 = _merge((oa_c, oa_l), (og_c, og_l), (y_c, y_l), (r_, k_, v_, g_), (rho, lnw, lnb), bd,
                            gates, x, mod_all, norm_ffn3, wa, wg, wr, wo, l)
        x = _ffn(h2, (xa, xb), mod_all, ffn_up, ffn_conv_w, conv_b, wd, l)

    y_prompt = x[0].reshape(BATCH, SEQ, D_MODEL)
    y_sample = x[1].reshape(DEC_BATCH, DEC_SEQ, D_MODEL)
    new_cache_k = jnp.stack(ks, axis=1)
    new_cache_v = jnp.stack(vs, axis=1)
    new_state_gla = jnp.stack(gla_states, axis=1)
    new_state_rwkv = jnp.stack(rwkv_states, axis=1)
    return (y_prompt, y_sample, new_cache_k, new_cache_v, new_state_gla, new_state_rwkv)
```
